```python
import jax, jax.numpy as jnp
from jax import lax
import numpy as np

D_MODEL = 1024
BATCH = 4
SEQ = 8192
DEPTH = 4

Q_BLOCK = 128
SB_HEADS = 8
SB_HEAD_DIM = 64
MLA_HEADS = 8
MLA_NOPE_DIM = 64
MLA_ROPE_DIM = 32
MLA_V_DIM = 64
MLA_Q_RANK = 768
MLA_KV_RANK = 256
ROPE_THETA = 10000.0
CONV_WIDTH = D_MODEL
CONV_K = 3
D_FF = 2816
LN_EPS = 1e-5
RMS_EPS = 1e-6
DEEPNORM_ALPHA = (2 * DEPTH) ** 0.25
DEEPNORM_BETA = (8 * DEPTH) ** -0.25
N_EVEN = (DEPTH + 1) // 2
N_ODD = DEPTH // 2
SB_WIDTH = SB_HEADS * SB_HEAD_DIM
MLA_WIDTH = MLA_HEADS * MLA_V_DIM
MIX_IN_SPLITS = (SB_WIDTH, 2 * SB_WIDTH, 3 * SB_WIDTH,
                 3 * SB_WIDTH + MLA_Q_RANK, 3 * SB_WIDTH + MLA_Q_RANK + MLA_KV_RANK)
MIX_IN_WIDTH = 3 * SB_WIDTH + MLA_Q_RANK + MLA_KV_RANK + MLA_ROPE_DIM

kernel_name = "hybrid_stickbreak_mla_shortconv_macaron"


def layer_norm(x, g, b):
    xf = x.astype(jnp.float32)
    mu = jnp.mean(xf, -1, keepdims=True)
    var = jnp.mean(jnp.square(xf - mu), -1, keepdims=True)
    y = (xf - mu) * lax.rsqrt(var + LN_EPS) * g.astype(jnp.float32) + b.astype(jnp.float32)
    return y.astype(x.dtype)


def rms_norm(x, g):
    xf = x.astype(jnp.float32)
    y = xf * lax.rsqrt(jnp.mean(xf * xf, -1, keepdims=True) + RMS_EPS) * g.astype(jnp.float32)
    return y.astype(x.dtype)


def swiglu(x, w_gate, w_up, w_down):
    return (jax.nn.silu(x @ w_gate) * (x @ w_up)) @ w_down


def rope_tables(seq, dtype):
    inv_freq = ROPE_THETA ** (-jnp.arange(0, MLA_ROPE_DIM, 2, dtype=jnp.float32) / MLA_ROPE_DIM)
    ang = jnp.arange(seq, dtype=jnp.float32)[:, None] * inv_freq[None, :]
    return jnp.cos(ang).astype(dtype), jnp.sin(ang).astype(dtype)


def apply_rope(x, cos, sin):
    half = x.shape[-1] // 2
    x1, x2 = x[..., :half], x[..., half:]
    return jnp.concatenate([x1 * cos - x2 * sin, x2 * cos + x1 * sin], axis=-1)


def to_blocks(t):
    b, s = t.shape[:2]
    return t.reshape(b, s // Q_BLOCK, Q_BLOCK, *t.shape[2:]).swapaxes(0, 1)


def from_blocks(t):
    nb, b, qb = t.shape[:3]
    return t.swapaxes(0, 1).reshape(b, nb * qb, *t.shape[3:])


def stick_breaking_attention(q, k, v):
    seq, d = q.shape[1], q.shape[-1]
    scale = d ** -0.5
    key_pos = jnp.arange(seq)

    def one_block(args):
        qb, blk = args
        q_pos = blk * Q_BLOCK + jnp.arange(Q_BLOCK)
        z = jnp.einsum('bqhd,bkhd->bhqk', qb, k, preferred_element_type=jnp.float32) * scale
        past = key_pos[None, :] < q_pos[:, None]
        log_beta = jax.nn.log_sigmoid(z)
        log_not = jnp.where(past, log_beta - z, 0.0)
        later = lax.cumsum(log_not, axis=3, reverse=True) - log_not
        w = jnp.where(past, jnp.exp(log_beta + later), 0.0)
        return jnp.einsum('bhqk,bkhd->bqhd', w.astype(v.dtype), v)

    out = lax.map(one_block, (to_blocks(q), jnp.arange(seq // Q_BLOCK)))
    return from_blocks(out)


def mla_attention(q_nope, q_rope, k_nope, k_rope, v):
    seq = q_nope.shape[1]
    scale = (MLA_NOPE_DIM + MLA_ROPE_DIM) ** -0.5
    key_pos = jnp.arange(seq)

    def one_block(args):
        qn, qr, blk = args
        q_pos = blk * Q_BLOCK + jnp.arange(Q_BLOCK)
        s = (jnp.einsum('bqhd,bkhd->bhqk', qn, k_nope, preferred_element_type=jnp.float32)
             + jnp.einsum('bqhr,bkr->bhqk', qr, k_rope, preferred_element_type=jnp.float32)) * scale
        s = jnp.where(key_pos[None, :] <= q_pos[:, None], s, -jnp.inf)
        p = jax.nn.softmax(s, axis=-1)
        return jnp.einsum('bhqk,bkhd->bqhd', p.astype(v.dtype), v)

    out = lax.map(one_block, (to_blocks(q_nope), to_blocks(q_rope), jnp.arange(seq // Q_BLOCK)))
    return from_blocks(out)


def attention_group_mixer(x, w_in, q_norm_g, w_uq, kv_norm_g, w_ukv, w_out, cos, sin):
    b, s, _ = x.shape
    q_sb, k_sb, v_sb, c_q, c_kv, k_rope = jnp.split(x @ w_in, MIX_IN_SPLITS, axis=-1)
    sb_shape = (b, s, SB_HEADS, SB_HEAD_DIM)
    out_sb = stick_breaking_attention(q_sb.reshape(sb_shape), k_sb.reshape(sb_shape),
                                      v_sb.reshape(sb_shape)).reshape(b, s, SB_WIDTH)
    q = (rms_norm(c_q, q_norm_g) @ w_uq).reshape(b, s, MLA_HEADS, MLA_NOPE_DIM + MLA_ROPE_DIM)
    q_nope = q[..., :MLA_NOPE_DIM]
    q_rope = apply_rope(q[..., MLA_NOPE_DIM:], cos[:, None, :], sin[:, None, :])
    kv = (rms_norm(c_kv, kv_norm_g) @ w_ukv).reshape(b, s, MLA_HEADS, MLA_NOPE_DIM + MLA_V_DIM)
    k_nope, v = kv[..., :MLA_NOPE_DIM], kv[..., MLA_NOPE_DIM:]
    k_rope = apply_rope(k_rope, cos, sin)
    out_mla = mla_attention(q_nope, q_rope, k_nope, k_rope, v).reshape(b, s, MLA_WIDTH)
    return jnp.concatenate([out_sb, out_mla], axis=-1) @ w_out


def short_conv_mixer(x, w_in, conv_w, w_out):
    gate_b, gate_c, h = jnp.split(x @ w_in, 3, axis=-1)
    u = gate_c * h
    conv = lax.conv_general_dilated(
        u, conv_w[:, None, :], window_strides=(1,), padding=[(CONV_K - 1, 0)],
        dimension_numbers=('NWC', 'WIO', 'NWC'), feature_group_count=CONV_WIDTH)
    return (gate_b * conv) @ w_out


def _dense(key, shape, fan_in, scale=1.0):
    return jax.random.normal(key, shape, jnp.float32) * (scale * fan_in ** -0.5)


def setup_inputs(seed: int = 0) -> dict:
    key = jax.random.key(seed)
    ks = jax.random.split(key, 16)
    n = lambda k, shape: jax.random.normal(k, shape, jnp.float32)
    mix_out_in = SB_WIDTH + MLA_WIDTH
    return {
        "x": n(ks[0], (BATCH, SEQ, D_MODEL)),
        "ln_g": 1.0 + 0.02 * n(ks[1], (DEPTH, 3, D_MODEL)),
        "ln_b": 0.02 * n(ks[2], (DEPTH, 3, D_MODEL)),
        "ffn_w_gate": _dense(ks[3], (DEPTH, 2, D_MODEL, D_FF), D_MODEL),
        "ffn_w_up": _dense(ks[4], (DEPTH, 2, D_MODEL, D_FF), D_MODEL),
        "ffn_w_down": _dense(ks[5], (DEPTH, 2, D_FF, D_MODEL), D_FF, DEEPNORM_BETA),
        "mix_w_in": _dense(ks[6], (N_EVEN, D_MODEL, MIX_IN_WIDTH), D_MODEL),
        "mla_q_norm_g": 1.0 + 0.02 * n(ks[7], (N_EVEN, MLA_Q_RANK)),
        "mla_w_uq": _dense(ks[8], (N_EVEN, MLA_Q_RANK, MLA_HEADS * (MLA_NOPE_DIM + MLA_ROPE_DIM)), MLA_Q_RANK),
        "mla_kv_norm_g": 1.0 + 0.02 * n(ks[9], (N_EVEN, MLA_KV_RANK)),
        "mla_w_ukv": _dense(ks[10], (N_EVEN, MLA_KV_RANK, MLA_HEADS * (MLA_NOPE_DIM + MLA_V_DIM)), MLA_KV_RANK),
        "mix_w_out": _dense(ks[11], (N_EVEN, mix_out_in, D_MODEL), mix_out_in, DEEPNORM_BETA),
        "conv_w_in": _dense(ks[12], (N_ODD, D_MODEL, 3 * CONV_WIDTH), D_MODEL),
        "conv_w": _dense(ks[13], (N_ODD, CONV_K, CONV_WIDTH), CONV_K),
        "conv_w_out": _dense(ks[14], (N_ODD, CONV_WIDTH, D_MODEL), CONV_WIDTH, DEEPNORM_BETA),
    }


def reference(x, ln_g, ln_b, ffn_w_gate, ffn_w_up, ffn_w_down, mix_w_in, mla_q_norm_g, mla_w_uq,
              mla_kv_norm_g, mla_w_ukv, mix_w_out, conv_w_in, conv_w, conv_w_out):
    cos, sin = rope_tables(x.shape[1], x.dtype)
    for layer in range(DEPTH):
        j = layer // 2
        f1 = swiglu(x, ffn_w_gate[layer, 0], ffn_w_up[layer, 0], ffn_w_down[layer, 0])
        x = layer_norm(DEEPNORM_ALPHA * x + 0.5 * f1, ln_g[layer, 0], ln_b[layer, 0])
        if layer % 2 == 0:
            m = attention_group_mixer(x, mix_w_in[j], mla_q_norm_g[j], mla_w_uq[j], mla_kv_norm_g[j],
                                      mla_w_ukv[j], mix_w_out[j], cos, sin)
        else:
            m = short_conv_mixer(x, conv_w_in[j], conv_w[j], conv_w_out[j])
        x = layer_norm(DEEPNORM_ALPHA * x + m, ln_g[layer, 1], ln_b[layer, 1])
        f2 = swiglu(x, ffn_w_gate[layer, 1], ffn_w_up[layer, 1], ffn_w_down[layer, 1])
        x = layer_norm(DEEPNORM_ALPHA * x + 0.5 * f2, ln_g[layer, 2], ln_b[layer, 2])
    return x
```

```python
import functools

import jax
import jax.numpy as jnp
from jax import lax
from jax.experimental import pallas as pl
from jax.experimental.pallas import tpu as pltpu

D_MODEL = 1024
DEPTH = 4
SB_HEADS = 8
SB_HEAD_DIM = 64
MLA_HEADS = 8
MLA_NOPE_DIM = 64
MLA_ROPE_DIM = 32
MLA_V_DIM = 64
MLA_Q_RANK = 768
MLA_KV_RANK = 256
ROPE_THETA = 10000.0
CONV_K = 3
D_FF = 2816
LN_EPS = 1e-5
RMS_EPS = 1e-6
ALPHA = (2 * DEPTH) ** 0.25
SB_WIDTH = SB_HEADS * SB_HEAD_DIM
MLA_WIDTH = MLA_HEADS * MLA_V_DIM

LANES = 128
HEAD_PAIRS = SB_HEADS // 2
MLA_QK_PAD = LANES
FF_CHUNK = 256
ROW_TILE = 512
ATTN_TILE = 256
VMEM_LIMIT = 56 * 1024 * 1024

BF16 = jnp.bfloat16
F32 = jnp.float32
NEG_BIG = -1e30
LOG2E = 1.4426950408889634


def _dot(a, b):
    return jnp.dot(a, b, preferred_element_type=F32)


def _dot_nt(a, b):
    return lax.dot_general(a, b, (((1,), (1,)), ((), ())), preferred_element_type=F32)


def _layer_norm(y, g, b):
    mu = jnp.mean(y, axis=-1, keepdims=True)
    d = y - mu
    var = jnp.mean(d * d, axis=-1, keepdims=True)
    return d * lax.rsqrt(var + LN_EPS) * g + b


def _rms_norm(c, g):
    return c * lax.rsqrt(jnp.mean(c * c, axis=-1, keepdims=True) + RMS_EPS) * g


def _params(*sem):
    return pltpu.CompilerParams(dimension_semantics=sem, vmem_limit_bytes=VMEM_LIMIT)


def _resident(shape):
    nd = len(shape)
    return pl.BlockSpec(shape, lambda *_: (0,) * nd, pipeline_mode=pl.Buffered(1))


def _ffn_kernel(x_ref, wg_ref, wu_ref, wd_ref, g_ref, b_ref, o_ref, h_ref):
    x = x_ref[...]
    xb = x.astype(BF16)
    for c in range(D_FF // FF_CHUNK):
        sl = slice(c * FF_CHUNK, (c + 1) * FF_CHUNK)
        gate = _dot(xb, wg_ref[:, sl])
        up = _dot(xb, wu_ref[:, sl])
        h_ref[:, sl] = (gate * (1.0 / (1.0 + jnp.exp(-gate))) * up).astype(BF16)
    f = _dot(h_ref[...], wd_ref[...])
    o_ref[...] = _layer_norm(ALPHA * x + 0.5 * f, g_ref[...], b_ref[...])


def _ffn(x, wg, wu, wd, g, b):
    n = x.shape[0]
    row = pl.BlockSpec((ROW_TILE, D_MODEL), lambda i: (i, 0))
    return pl.pallas_call(
        _ffn_kernel,
        out_shape=jax.ShapeDtypeStruct((n, D_MODEL), F32),
        grid=(n // ROW_TILE,),
        in_specs=[row, _resident(wg.shape), _resident(wu.shape), _resident(wd.shape),
                  _resident(g.shape), _resident(b.shape)],
        out_specs=row,
        scratch_shapes=[pltpu.VMEM((ROW_TILE, D_FF), BF16)],
        compiler_params=_params("parallel"),
        name="ffn",
    )(x, wg, wu, wd, g, b)


_X_SB = 3 * SB_WIDTH
_X_CQ = _X_SB + MLA_Q_RANK
_X_CKV = _X_CQ + MLA_KV_RANK
_X_KRA = _X_CKV + LANES
_X_KRB = _X_KRA + LANES
_QK_WIDTH = MLA_HEADS * MLA_QK_PAD


def _proj_even_kernel(x_ref, wx_ref, gq_ref, wqa_ref, wqb_ref, gkv_ref, wk_ref, wv_ref,
                      cos_ref, sin_ref, sb_ref, q_ref, k_ref, v_ref):
    xb = x_ref[...].astype(BF16)
    cos = cos_ref[...]
    sin = sin_ref[...]
    sb_ref[:, :SB_WIDTH] = (_dot(xb, wx_ref[:, :SB_WIDTH]) * (SB_HEAD_DIM ** -0.5)).astype(BF16)
    sb_ref[:, SB_WIDTH:_X_SB] = _dot(xb, wx_ref[:, SB_WIDTH:_X_SB]).astype(BF16)
    cq = _rms_norm(_dot(xb, wx_ref[:, _X_SB:_X_CQ]), gq_ref[...]).astype(BF16)
    for h in range(MLA_HEADS):
        sl = slice(h * MLA_QK_PAD, (h + 1) * MLA_QK_PAD)
        q_ref[:, sl] = (_dot(cq, wqa_ref[:, sl]) * cos + _dot(cq, wqb_ref[:, sl]) * sin).astype(BF16)
    ckv = _rms_norm(_dot(xb, wx_ref[:, _X_CQ:_X_CKV]), gkv_ref[...]).astype(BF16)
    kr = _dot(xb, wx_ref[:, _X_CKV:_X_KRA]) * cos + _dot(xb, wx_ref[:, _X_KRA:_X_KRB]) * sin
    for h in range(MLA_HEADS):
        sl = slice(h * MLA_QK_PAD, (h + 1) * MLA_QK_PAD)
        k_ref[:, sl] = (_dot(ckv, wk_ref[:, sl]) + kr).astype(BF16)
    v_ref[...] = _dot(ckv, wv_ref[...]).astype(BF16)


def _proj_even(x, wx, gq, wqa, wqb, gkv, wk, wv, cos_t, sin_t, seq):
    n = x.shape[0]
    tiles_per_seq = seq // ROW_TILE
    row = lambda w: pl.BlockSpec((ROW_TILE, w), lambda i: (i, 0))
    pos = pl.BlockSpec((ROW_TILE, LANES), lambda i: (i % tiles_per_seq, 0))
    return pl.pallas_call(
        _proj_even_kernel,
        out_shape=(jax.ShapeDtypeStruct((n, _X_SB), BF16),
                   jax.ShapeDtypeStruct((n, _QK_WIDTH), BF16),
                   jax.ShapeDtypeStruct((n, _QK_WIDTH), BF16),
                   jax.ShapeDtypeStruct((n, MLA_WIDTH), BF16)),
        grid=(n // ROW_TILE,),
        in_specs=[row(D_MODEL), _resident(wx.shape), _resident(gq.shape), _resident(wqa.shape),
                  _resident(wqb.shape), _resident(gkv.shape), _resident(wk.shape),
                  _resident(wv.shape), pos, pos],
        out_specs=(row(_X_SB), row(_QK_WIDTH), row(_QK_WIDTH), row(MLA_WIDTH)),
        compiler_params=_params("parallel"),
        name="proj_even",
    )(x, wx, gq, wqa, wqb, gkv, wk, wv, cos_t, sin_t)


def _sb_attn_kernel(q_ref, k_ref, v_ref, o_ref):
    t = ATTN_TILE
    i = pl.program_id(2)
    q = q_ref[...]
    lane = lax.broadcasted_iota(jnp.int32, (t, LANES), 1)
    row = lax.broadcasted_iota(jnp.int32, (t, t), 0)
    col = lax.broadcasted_iota(jnp.int32, (t, t), 1)
    past = col < row
    tri = jnp.where(row > col, 1.0, 0.0).astype(BF16)
    head_lanes = [lane < SB_HEAD_DIM, lane >= SB_HEAD_DIM]
    qh = [jnp.where(m, q, jnp.zeros_like(q)) for m in head_lanes]

    def block(j, carry, diagonal):
        start = pl.multiple_of(j * t, t)
        k = k_ref[pl.ds(start, t), :]
        v = v_ref[pl.ds(start, t), :]
        out = []
        for h in range(2):
            acc, later_blocks = carry[h]
            z = _dot_nt(qh[h], k)
            log_beta = jnp.minimum(z, 0.0) - jnp.log(1.0 + jnp.exp(-jnp.abs(z)))
            log_not = log_beta - z
            if diagonal:
                log_not = jnp.where(past, log_not, 0.0)
            hi = log_not.astype(BF16)
            lo = (log_not - hi.astype(F32)).astype(BF16)
            later = _dot(hi, tri) + _dot(lo, tri) + later_blocks
            w = jnp.exp(log_beta + later)
            if diagonal:
                w = jnp.where(past, w, 0.0)
            acc = acc + _dot(w.astype(BF16), v)
            later_blocks = later_blocks + jnp.sum(log_not, axis=-1, keepdims=True)
            out.append((acc, later_blocks))
        return tuple(out)

    zero = (jnp.zeros((t, LANES), F32), jnp.zeros((t, 1), F32))
    carry = block(i, (zero, zero), True)
    carry = lax.fori_loop(0, i, lambda n, c: block(i - 1 - n, c, False), carry)
    o_ref[...] = jnp.where(head_lanes[0], carry[0][0], carry[1][0]).astype(o_ref.dtype)


def _sb_attn(qkv, batch, seq):
    n = qkv.shape[0]
    nq = seq // ATTN_TILE
    return pl.pallas_call(
        _sb_attn_kernel,
        out_shape=jax.ShapeDtypeStruct((n, SB_WIDTH), BF16),
        grid=(batch, HEAD_PAIRS, nq),
        in_specs=[pl.BlockSpec((ATTN_TILE, LANES), lambda b, p, i: (b * nq + i, p)),
                  pl.BlockSpec((seq, LANES), lambda b, p, i: (b, HEAD_PAIRS + p)),
                  pl.BlockSpec((seq, LANES), lambda b, p, i: (b, 2 * HEAD_PAIRS + p))],
        out_specs=pl.BlockSpec((ATTN_TILE, LANES), lambda b, p, i: (b * nq + i, p)),
        compiler_params=_params("parallel", "parallel", "parallel"),
        name="sb_attn",
    )(qkv, qkv, qkv)


def _mla_attn_kernel(q_ref, k_ref, v_ref, o_ref):
    t = ATTN_TILE
    i = pl.program_id(2)
    c2 = (MLA_NOPE_DIM + MLA_ROPE_DIM) ** -0.5 * LOG2E
    lane = lax.broadcasted_iota(jnp.int32, (t, LANES), 1)
    row = lax.broadcasted_iota(jnp.int32, (t, t), 0)
    col = lax.broadcasted_iota(jnp.int32, (t, t), 1)
    visible = col <= row
    qh = [q_ref[:, h * MLA_QK_PAD:(h + 1) * MLA_QK_PAD] for h in range(2)]

    def block(j, carry, diagonal):
        start = pl.multiple_of(j * t, t)
        v = v_ref[pl.ds(start, t), :]
        out = []
        for h in range(2):
            m, l, acc = carry[h]
            k = k_ref[pl.ds(start, t), h * MLA_QK_PAD:(h + 1) * MLA_QK_PAD]
            s = _dot_nt(qh[h], k)
            if diagonal:
                s = jnp.where(visible, s, NEG_BIG)
            m_new = jnp.maximum(m, jnp.max(s, axis=-1, keepdims=True))
            p = jnp.exp2((s - m_new) * c2)
            rescale = jnp.exp2((m - m_new) * c2)
            l = rescale * l + jnp.sum(p, axis=-1, keepdims=True)
            acc = rescale * acc + _dot(p.astype(BF16), v)
            out.append((m_new, l, acc))
        return tuple(out)

    init = (jnp.full((t, 1), NEG_BIG, F32), jnp.zeros((t, 1), F32), jnp.zeros((t, LANES), F32))
    carry = block(i, (init, init), True)
    carry = lax.fori_loop(0, i, lambda n, c: block(i - 1 - n, c, False), carry)
    o0 = carry[0][2] / carry[0][1]
    o1 = carry[1][2] / carry[1][1]
    o_ref[...] = jnp.where(lane < MLA_V_DIM, o0, o1).astype(o_ref.dtype)


def _mla_attn(q, k, v, batch, seq):
    n = q.shape[0]
    nq = seq // ATTN_TILE
    return pl.pallas_call(
        _mla_attn_kernel,
        out_shape=jax.ShapeDtypeStruct((n, MLA_WIDTH), BF16),
        grid=(batch, HEAD_PAIRS, nq),
        in_specs=[pl.BlockSpec((ATTN_TILE, 2 * MLA_QK_PAD), lambda b, p, i: (b * nq + i, p)),
                  pl.BlockSpec((seq, 2 * MLA_QK_PAD), lambda b, p, i: (b, p)),
                  pl.BlockSpec((seq, LANES), lambda b, p, i: (b, p))],
        out_specs=pl.BlockSpec((ATTN_TILE, LANES), lambda b, p, i: (b * nq + i, p)),
        compiler_params=_params("parallel", "parallel", "parallel"),
        name="mla_attn",
    )(q, k, v)


def _post_even_kernel(x_ref, sb_ref, mla_ref, w_ref, g_ref, b_ref, o_ref):
    m = _dot(sb_ref[...], w_ref[:SB_WIDTH, :]) + _dot(mla_ref[...], w_ref[SB_WIDTH:, :])
    o_ref[...] = _layer_norm(ALPHA * x_ref[...] + m, g_ref[...], b_ref[...])


def _post_even(x, sb, mla, w, g, b):
    n = x.shape[0]
    row = lambda wd: pl.BlockSpec((ROW_TILE, wd), lambda i: (i, 0))
    return pl.pallas_call(
        _post_even_kernel,
        out_shape=jax.ShapeDtypeStruct((n, D_MODEL), F32),
        grid=(n // ROW_TILE,),
        in_specs=[row(D_MODEL), row(SB_WIDTH), row(MLA_WIDTH), _resident(w.shape),
                  _resident(g.shape), _resident(b.shape)],
        out_specs=row(D_MODEL),
        compiler_params=_params("parallel"),
        name="post_even",
    )(x, sb, mla, w, g, b)


HALO = 8


def _conv_mixer_kernel(tiles_per_seq, x_ref, win_ref, cw_ref, wout_ref, g_ref, b_ref, o_ref, tail_ref):
    i = pl.program_id(0)
    x = x_ref[...]
    xb = x.astype(BF16)
    gate_b = _dot(xb, win_ref[:, :D_MODEL])
    u = _dot(xb, win_ref[:, D_MODEL:2 * D_MODEL]) * _dot(xb, win_ref[:, 2 * D_MODEL:])
    @pl.when(i % tiles_per_seq == 0)
    def _():
        tail_ref[...] = jnp.zeros_like(tail_ref)

    prev = tail_ref[...]
    tail_ref[...] = u[ROW_TILE - HALO:, :]
    ext = jnp.concatenate([prev, u], axis=0)
    cw = cw_ref[...]
    conv = (cw[0:1, :] * ext[HALO - 2:HALO - 2 + ROW_TILE, :]
            + cw[1:2, :] * ext[HALO - 1:HALO - 1 + ROW_TILE, :]
            + cw[2:3, :] * u)
    m = _dot((gate_b * conv).astype(BF16), wout_ref[...])
    o_ref[...] = _layer_norm(ALPHA * x + m, g_ref[...], b_ref[...])


def _conv_mixer(x, w_in, conv_w, w_out, g, b, seq):
    n = x.shape[0]
    row = pl.BlockSpec((ROW_TILE, D_MODEL), lambda i: (i, 0))
    return pl.pallas_call(
        functools.partial(_conv_mixer_kernel, seq // ROW_TILE),
        out_shape=jax.ShapeDtypeStruct((n, D_MODEL), F32),
        grid=(n // ROW_TILE,),
        in_specs=[row, _resident(w_in.shape), _resident(conv_w.shape), _resident(w_out.shape),
                  _resident(g.shape), _resident(b.shape)],
        out_specs=row,
        scratch_shapes=[pltpu.VMEM((HALO, D_MODEL), F32)],
        compiler_params=_params("arbitrary"),
        name="conv_mixer",
    )(x, w_in, conv_w, w_out, g, b)


def _rope_tables(seq):
    half = MLA_ROPE_DIM // 2
    inv_freq = ROPE_THETA ** (-jnp.arange(0, MLA_ROPE_DIM, 2, dtype=F32) / MLA_ROPE_DIM)
    ang = jnp.arange(seq, dtype=F32)[:, None] * inv_freq[None, :]
    cos, sin = jnp.cos(ang), jnp.sin(ang)
    ones = jnp.ones((seq, MLA_NOPE_DIM), F32)
    pad = jnp.zeros((seq, MLA_QK_PAD - MLA_NOPE_DIM - 2 * half), F32)
    cos_t = jnp.concatenate([ones, cos, cos, pad], axis=1)
    sin_t = jnp.concatenate([0.0 * ones, -sin, sin, pad], axis=1)
    return cos_t, sin_t


def _even_weights(w_in, w_uq, w_ukv):
    half = MLA_ROPE_DIM // 2
    qk = MLA_NOPE_DIM + MLA_ROPE_DIM
    zeros = lambda r, c: jnp.zeros((r, c), F32)
    w_kr = w_in[:, _X_CKV:]
    k1, k2 = w_kr[:, :half], w_kr[:, half:]
    nope_pad = zeros(D_MODEL, MLA_NOPE_DIM)
    tail_pad = zeros(D_MODEL, MLA_QK_PAD - qk)
    wx = jnp.concatenate([w_in[:, :_X_CKV], nope_pad, k1, k2, tail_pad,
                          nope_pad, k2, k1, tail_pad], axis=1)
    wq = w_uq.reshape(MLA_Q_RANK, MLA_HEADS, qk)
    q_nope, q1, q2 = wq[..., :MLA_NOPE_DIM], wq[..., MLA_NOPE_DIM:MLA_NOPE_DIM + half], wq[..., MLA_NOPE_DIM + half:]
    zq = lambda c: jnp.zeros((MLA_Q_RANK, MLA_HEADS, c), F32)
    wqa = jnp.concatenate([q_nope, q1, q2, zq(MLA_QK_PAD - qk)], axis=-1).reshape(MLA_Q_RANK, _QK_WIDTH)
    wqb = jnp.concatenate([zq(MLA_NOPE_DIM), q2, q1, zq(MLA_QK_PAD - qk)], axis=-1).reshape(MLA_Q_RANK, _QK_WIDTH)
    wkv = w_ukv.reshape(MLA_KV_RANK, MLA_HEADS, MLA_NOPE_DIM + MLA_V_DIM)
    wk = jnp.concatenate([wkv[..., :MLA_NOPE_DIM],
                          jnp.zeros((MLA_KV_RANK, MLA_HEADS, MLA_QK_PAD - MLA_NOPE_DIM), F32)],
                         axis=-1).reshape(MLA_KV_RANK, _QK_WIDTH)
    wv = wkv[..., MLA_NOPE_DIM:].reshape(MLA_KV_RANK, MLA_WIDTH)
    return [w.astype(BF16) for w in (wx, wqa, wqb, wk, wv)]


def kernel(x, ln_g, ln_b, ffn_w_gate, ffn_w_up, ffn_w_down, mix_w_in, mla_q_norm_g, mla_w_uq,
           mla_kv_norm_g, mla_w_ukv, mix_w_out, conv_w_in, conv_w, conv_w_out):
    batch, seq, _ = x.shape
    assert seq % ROW_TILE == 0 and seq % ATTN_TILE == 0
    cos_t, sin_t = _rope_tables(seq)
    h = x.reshape(batch * seq, D_MODEL)
    vec = lambda a: a.reshape(1, -1)
    for layer in range(DEPTH):
        j = layer // 2
        ln = lambda s: (vec(ln_g[layer, s]), vec(ln_b[layer, s]))
        ffn_w = lambda s: (ffn_w_gate[layer, s].astype(BF16), ffn_w_up[layer, s].astype(BF16),
                           ffn_w_down[layer, s].astype(BF16))
        h = _ffn(h, *ffn_w(0), *ln(0))
        if layer % 2 == 0:
            wx, wqa, wqb, wk, wv = _even_weights(mix_w_in[j], mla_w_uq[j], mla_w_ukv[j])
            sb_qkv, q, k, v = _proj_even(h, wx, vec(mla_q_norm_g[j]), wqa, wqb, vec(mla_kv_norm_g[j]),
                                         wk, wv, cos_t, sin_t, seq)
            out_sb = _sb_attn(sb_qkv, batch, seq)
            out_mla = _mla_attn(q, k, v, batch, seq)
            h = _post_even(h, out_sb, out_mla, mix_w_out[j].astype(BF16), *ln(1))
        else:
            h = _conv_mixer(h, conv_w_in[j].astype(BF16), conv_w[j], conv_w_out[j].astype(BF16), *ln(1), seq)
        h = _ffn(h, *ffn_w(1), *ln(2))
    return h.reshape(batch, seq, D_MODEL)
```

```python
import functools

import jax
import jax.numpy as jnp
from jax import lax
from jax.experimental import pallas as pl
from jax.experimental.pallas import tpu as pltpu

D_MODEL = 1024
DEPTH = 4
SB_HEADS = 8
SB_HEAD_DIM = 64
MLA_HEADS = 8
MLA_NOPE_DIM = 64
MLA_ROPE_DIM = 32
MLA_V_DIM = 64
MLA_Q_RANK = 768
MLA_KV_RANK = 256
ROPE_THETA = 10000.0
CONV_K = 3
D_FF = 2816
LN_EPS = 1e-5
RMS_EPS = 1e-6
ALPHA = (2 * DEPTH) ** 0.25
SB_WIDTH = SB_HEADS * SB_HEAD_DIM
MLA_WIDTH = MLA_HEADS * MLA_V_DIM

LANES = 128
HEAD_PAIRS = SB_HEADS // 2
MLA_QK_PAD = LANES
FF_CHUNK = 256
ROW_TILE = 512
Q_TILE = 512
K_TILE = 256
VMEM_LIMIT = 56 * 1024 * 1024

BF16 = jnp.bfloat16
F32 = jnp.float32
NEG_BIG = -1e30
LOG2E = 1.4426950408889634


def _dot(a, b):
    return jnp.dot(a, b, preferred_element_type=F32)


def _dot_nt(a, b):
    return lax.dot_general(a, b, (((1,), (1,)), ((), ())), preferred_element_type=F32)


def _layer_norm(y, g, b):
    mu = jnp.mean(y, axis=-1, keepdims=True)
    d = y - mu
    var = jnp.mean(d * d, axis=-1, keepdims=True)
    return d * lax.rsqrt(var + LN_EPS) * g + b


def _rms_norm(c, g):
    return c * lax.rsqrt(jnp.mean(c * c, axis=-1, keepdims=True) + RMS_EPS) * g


def _params(*sem):
    return pltpu.CompilerParams(dimension_semantics=sem, vmem_limit_bytes=VMEM_LIMIT)


def _resident(shape):
    nd = len(shape)
    return pl.BlockSpec(shape, lambda *_: (0,) * nd, pipeline_mode=pl.Buffered(1))


def _ffn_kernel(x_ref, wg_ref, wu_ref, wd_ref, g_ref, b_ref, o_ref, h_ref):
    x = x_ref[...]
    xb = x.astype(BF16)
    for c in range(D_FF // FF_CHUNK):
        sl = slice(c * FF_CHUNK, (c + 1) * FF_CHUNK)
        gate = _dot(xb, wg_ref[:, sl])
        up = _dot(xb, wu_ref[:, sl])
        h_ref[:, sl] = (gate * (1.0 / (1.0 + jnp.exp(-gate))) * up).astype(BF16)
    f = _dot(h_ref[...], wd_ref[...])
    o_ref[...] = _layer_norm(ALPHA * x + 0.5 * f, g_ref[...], b_ref[...])


def _ffn(x, wg, wu, wd, g, b):
    n = x.shape[0]
    row = pl.BlockSpec((ROW_TILE, D_MODEL), lambda i: (i, 0))
    return pl.pallas_call(
        _ffn_kernel,
        out_shape=jax.ShapeDtypeStruct((n, D_MODEL), F32),
        grid=(n // ROW_TILE,),
        in_specs=[row, _resident(wg.shape), _resident(wu.shape), _resident(wd.shape),
                  _resident(g.shape), _resident(b.shape)],
        out_specs=row,
        scratch_shapes=[pltpu.VMEM((ROW_TILE, D_FF), BF16)],
        compiler_params=_params("parallel"),
        name="ffn",
    )(x, wg, wu, wd, g, b)


_X_QK = 2 * SB_WIDTH
_X_CQ = _X_QK + MLA_Q_RANK
_X_CKV = _X_CQ + MLA_KV_RANK
_X_KRA = _X_CKV + LANES
_X_KRB = _X_KRA + LANES
_QK_WIDTH = MLA_HEADS * MLA_QK_PAD


def _proj_even_kernel(x_ref, wx_ref, wsv_ref, gq_ref, wqa_ref, wqb_ref, gkv_ref, wk_ref, wv_ref,
                      cos_ref, sin_ref, sb_ref, sbv_ref, q_ref, k_ref, v_ref):
    xb = x_ref[...].astype(BF16)
    cos = cos_ref[...]
    sin = sin_ref[...]
    sb_ref[:, :SB_WIDTH] = (_dot(xb, wx_ref[:, :SB_WIDTH]) * (SB_HEAD_DIM ** -0.5)).astype(BF16)
    sb_ref[:, SB_WIDTH:] = _dot(xb, wx_ref[:, SB_WIDTH:_X_QK]).astype(BF16)
    sbv_ref[...] = _dot_nt(wsv_ref[...], xb).astype(BF16)
    cq = _rms_norm(_dot(xb, wx_ref[:, _X_QK:_X_CQ]), gq_ref[...]).astype(BF16)
    for h in range(MLA_HEADS):
        sl = slice(h * MLA_QK_PAD, (h + 1) * MLA_QK_PAD)
        q_ref[:, sl] = (_dot(cq, wqa_ref[:, sl]) * cos + _dot(cq, wqb_ref[:, sl]) * sin).astype(BF16)
    ckv = _rms_norm(_dot(xb, wx_ref[:, _X_CQ:_X_CKV]), gkv_ref[...]).astype(BF16)
    kr = _dot(xb, wx_ref[:, _X_CKV:_X_KRA]) * cos + _dot(xb, wx_ref[:, _X_KRA:_X_KRB]) * sin
    for h in range(MLA_HEADS):
        sl = slice(h * MLA_QK_PAD, (h + 1) * MLA_QK_PAD)
        k_ref[:, sl] = (_dot(ckv, wk_ref[:, sl]) + kr).astype(BF16)
    v_ref[...] = _dot_nt(wv_ref[...], ckv).astype(BF16)


def _proj_even(x, wx, wsv, gq, wqa, wqb, gkv, wk, wv, cos_t, sin_t, seq):
    n = x.shape[0]
    tiles_per_seq = seq // ROW_TILE
    row = lambda w: pl.BlockSpec((ROW_TILE, w), lambda i: (i, 0))
    col = lambda h: pl.BlockSpec((h, ROW_TILE), lambda i: (0, i))
    pos = pl.BlockSpec((ROW_TILE, LANES), lambda i: (i % tiles_per_seq, 0))
    return pl.pallas_call(
        _proj_even_kernel,
        out_shape=(jax.ShapeDtypeStruct((n, _X_QK), BF16),
                   jax.ShapeDtypeStruct((SB_WIDTH, n), BF16),
                   jax.ShapeDtypeStruct((n, _QK_WIDTH), BF16),
                   jax.ShapeDtypeStruct((n, _QK_WIDTH), BF16),
                   jax.ShapeDtypeStruct((MLA_WIDTH, n), BF16)),
        grid=(n // ROW_TILE,),
        in_specs=[row(D_MODEL), _resident(wx.shape), _resident(wsv.shape), _resident(gq.shape),
                  _resident(wqa.shape), _resident(wqb.shape), _resident(gkv.shape),
                  _resident(wk.shape), _resident(wv.shape), pos, pos],
        out_specs=(row(_X_QK), col(SB_WIDTH), row(_QK_WIDTH), row(_QK_WIDTH), col(MLA_WIDTH)),
        compiler_params=_params("parallel"),
        name="proj_even",
    )(x, wx, wsv, gq, wqa, wqb, gkv, wk, wv, cos_t, sin_t)


_DIAG_BLOCKS = Q_TILE // K_TILE


def _sweep_keys(i, scores, consume):
    last = (i + 1) * _DIAG_BLOCKS - 1
    scores(last, True, 0)
    for d in range(1, _DIAG_BLOCKS):
        scores(last - d, True, d % 2)
        consume(last - d + 1, (d - 1) % 2)
    held = (_DIAG_BLOCKS - 1) % 2

    def body(n, carry):
        j = i * _DIAG_BLOCKS - 2 * n
        scores(j - 1, False, 1 - held)
        consume(j, held)
        scores(j - 2, False, held)
        consume(j - 1, 1 - held)
        return carry

    lax.fori_loop(0, i * (_DIAG_BLOCKS // 2), body, 0)
    consume(0, held)


def _hidden(i, j, visible_when_equal):
    key = j * K_TILE + lax.broadcasted_iota(jnp.int32, (K_TILE, Q_TILE), 0)
    qry = i * Q_TILE + lax.broadcasted_iota(jnp.int32, (K_TILE, Q_TILE), 1)
    return key > qry if visible_when_equal else key >= qry


def _key_block(j):
    return pl.ds(pl.multiple_of(j * K_TILE, K_TILE), K_TILE)


def _merge_heads(o0, o1):
    half = LANES // 2
    return jnp.concatenate([o0[:half, :], o1[half:, :]], axis=0).T


def _attn_call(body, name, q, q_spec, k, k_spec, v_t, batch, seq, width, state_shapes):
    n = q.shape[0]
    nq = seq // Q_TILE
    return pl.pallas_call(
        body,
        out_shape=jax.ShapeDtypeStruct((n, width), BF16),
        grid=(batch, HEAD_PAIRS, nq),
        in_specs=[q_spec, k_spec, pl.BlockSpec((LANES, seq), lambda b, p, i: (p, b))],
        out_specs=pl.BlockSpec((Q_TILE, LANES), lambda b, p, i: (b * nq + i, p)),
        scratch_shapes=[pltpu.VMEM((2, 2, K_TILE, Q_TILE), F32)] + state_shapes,
        compiler_params=_params("parallel", "parallel", "parallel"),
        name=name,
    )(q, k, v_t)


def _sb_attn_kernel(q_ref, k_ref, vt_ref, o_ref, z_ref, acc_ref, later_ref):
    i = pl.program_id(2)
    q = q_ref[...]
    lane = lax.broadcasted_iota(jnp.int32, (Q_TILE, LANES), 1)
    qh = [jnp.where(lane < SB_HEAD_DIM, q, jnp.zeros_like(q)),
          jnp.where(lane >= SB_HEAD_DIM, q, jnp.zeros_like(q))]
    r = lax.broadcasted_iota(jnp.int32, (K_TILE, K_TILE), 0)
    c = lax.broadcasted_iota(jnp.int32, (K_TILE, K_TILE), 1)
    tri = jnp.where(c > r, 1.0, 0.0).astype(BF16)
    acc_ref[...] = jnp.zeros_like(acc_ref)
    later_ref[...] = jnp.zeros_like(later_ref)

    def scores(j, masked, slot):
        k = k_ref[_key_block(j), :]
        for h in range(2):
            z = _dot_nt(k, qh[h])
            if masked:
                z = jnp.where(_hidden(i, j, visible_when_equal=False), NEG_BIG, z)
            z_ref[slot, h] = z

    def consume(j, slot):
        vt = vt_ref[:, _key_block(j)]
        for h in range(2):
            z = z_ref[slot, h]
            log_beta = jnp.minimum(z, 0.0) - jnp.log(1.0 + jnp.exp(-jnp.abs(z)))
            log_not = log_beta - z
            hi = log_not.astype(BF16)
            lo = (log_not - hi.astype(F32)).astype(BF16)
            later = _dot(tri, hi) + _dot(tri, lo) + later_ref[h]
            w = jnp.exp(log_beta + later)
            acc_ref[h] += _dot(vt, w.astype(BF16))
            later_ref[h] += jnp.sum(log_not, axis=0, keepdims=True)

    _sweep_keys(i, scores, consume)
    o_ref[...] = _merge_heads(acc_ref[0], acc_ref[1]).astype(o_ref.dtype)


def _sb_attn(qk, v_t, batch, seq):
    nq = seq // Q_TILE
    return _attn_call(
        _sb_attn_kernel, "sb_attn",
        qk, pl.BlockSpec((Q_TILE, LANES), lambda b, p, i: (b * nq + i, p)),
        qk, pl.BlockSpec((seq, LANES), lambda b, p, i: (b, HEAD_PAIRS + p)),
        v_t, batch, seq, SB_WIDTH,
        [pltpu.VMEM((2, LANES, Q_TILE), F32), pltpu.VMEM((2, 1, Q_TILE), F32)])


def _mla_attn_kernel(q_ref, k_ref, vt_ref, o_ref, s_ref, acc_ref, m_ref, l_ref):
    i = pl.program_id(2)
    c2 = (MLA_NOPE_DIM + MLA_ROPE_DIM) ** -0.5 * LOG2E
    qh = [q_ref[:, h * MLA_QK_PAD:(h + 1) * MLA_QK_PAD] for h in range(2)]
    acc_ref[...] = jnp.zeros_like(acc_ref)
    l_ref[...] = jnp.zeros_like(l_ref)
    m_ref[...] = jnp.full_like(m_ref, NEG_BIG)

    def scores(j, masked, slot):
        for h in range(2):
            s = _dot_nt(k_ref[_key_block(j), h * MLA_QK_PAD:(h + 1) * MLA_QK_PAD], qh[h])
            if masked:
                s = jnp.where(_hidden(i, j, visible_when_equal=True), NEG_BIG, s)
            s_ref[slot, h] = s

    def consume(j, slot):
        vt = vt_ref[:, _key_block(j)]
        for h in range(2):
            s = s_ref[slot, h]
            m = m_ref[h]
            m_new = jnp.maximum(m, jnp.max(s, axis=0, keepdims=True))
            p = jnp.exp2((s - m_new) * c2)
            rescale = jnp.exp2((m - m_new) * c2)
            m_ref[h] = m_new
            l_ref[h] = rescale * l_ref[h] + jnp.sum(p, axis=0, keepdims=True)
            acc_ref[h] = rescale * acc_ref[h] + _dot(vt, p.astype(BF16))

    _sweep_keys(i, scores, consume)
    o_ref[...] = _merge_heads(acc_ref[0] / l_ref[0], acc_ref[1] / l_ref[1]).astype(o_ref.dtype)


def _mla_attn(q, k, v_t, batch, seq):
    nq = seq // Q_TILE
    return _attn_call(
        _mla_attn_kernel, "mla_attn",
        q, pl.BlockSpec((Q_TILE, 2 * MLA_QK_PAD), lambda b, p, i: (b * nq + i, p)),
        k, pl.BlockSpec((seq, 2 * MLA_QK_PAD), lambda b, p, i: (b, p)),
        v_t, batch, seq, MLA_WIDTH,
        [pltpu.VMEM((2, LANES, Q_TILE), F32), pltpu.VMEM((2, 1, Q_TILE), F32),
         pltpu.VMEM((2, 1, Q_TILE), F32)])


def _post_even_kernel(x_ref, sb_ref, mla_ref, w_ref, g_ref, b_ref, o_ref):
    m = _dot(sb_ref[...], w_ref[:SB_WIDTH, :]) + _dot(mla_ref[...], w_ref[SB_WIDTH:, :])
    o_ref[...] = _layer_norm(ALPHA * x_ref[...] + m, g_ref[...], b_ref[...])


def _post_even(x, sb, mla, w, g, b):
    n = x.shape[0]
    row = lambda wd: pl.BlockSpec((ROW_TILE, wd), lambda i: (i, 0))
    return pl.pallas_call(
        _post_even_kernel,
        out_shape=jax.ShapeDtypeStruct((n, D_MODEL), F32),
        grid=(n // ROW_TILE,),
        in_specs=[row(D_MODEL), row(SB_WIDTH), row(MLA_WIDTH), _resident(w.shape),
                  _resident(g.shape), _resident(b.shape)],
        out_specs=row(D_MODEL),
        compiler_params=_params("parallel"),
        name="post_even",
    )(x, sb, mla, w, g, b)


HALO = 8


def _conv_mixer_kernel(tiles_per_seq, x_ref, win_ref, cw_ref, wout_ref, g_ref, b_ref, o_ref, tail_ref):
    i = pl.program_id(0)
    x = x_ref[...]
    xb = x.astype(BF16)
    gate_b = _dot(xb, win_ref[:, :D_MODEL])
    u = _dot(xb, win_ref[:, D_MODEL:2 * D_MODEL]) * _dot(xb, win_ref[:, 2 * D_MODEL:])

    @pl.when(i % tiles_per_seq == 0)
    def _():
        tail_ref[...] = jnp.zeros_like(tail_ref)

    prev = tail_ref[...]
    tail_ref[...] = u[ROW_TILE - HALO:, :]
    ext = jnp.concatenate([prev, u], axis=0)
    cw = cw_ref[...]
    conv = (cw[0:1, :] * ext[HALO - 2:HALO - 2 + ROW_TILE, :]
            + cw[1:2, :] * ext[HALO - 1:HALO - 1 + ROW_TILE, :]
            + cw[2:3, :] * u)
    m = _dot((gate_b * conv).astype(BF16), wout_ref[...])
    o_ref[...] = _layer_norm(ALPHA * x + m, g_ref[...], b_ref[...])


def _conv_mixer(x, w_in, conv_w, w_out, g, b, seq):
    n = x.shape[0]
    row = pl.BlockSpec((ROW_TILE, D_MODEL), lambda i: (i, 0))
    return pl.pallas_call(
        functools.partial(_conv_mixer_kernel, seq // ROW_TILE),
        out_shape=jax.ShapeDtypeStruct((n, D_MODEL), F32),
        grid=(n // ROW_TILE,),
        in_specs=[row, _resident(w_in.shape), _resident(conv_w.shape), _resident(w_out.shape),
                  _resident(g.shape), _resident(b.shape)],
        out_specs=row,
        scratch_shapes=[pltpu.VMEM((HALO, D_MODEL), F32)],
        compiler_params=_params("arbitrary"),
        name="conv_mixer",
    )(x, w_in, conv_w, w_out, g, b)


def _rope_tables(seq):
    half = MLA_ROPE_DIM // 2
    inv_freq = ROPE_THETA ** (-jnp.arange(0, MLA_ROPE_DIM, 2, dtype=F32) / MLA_ROPE_DIM)
    ang = jnp.arange(seq, dtype=F32)[:, None] * inv_freq[None, :]
    cos, sin = jnp.cos(ang), jnp.sin(ang)
    ones = jnp.ones((seq, MLA_NOPE_DIM), F32)
    pad = jnp.zeros((seq, MLA_QK_PAD - MLA_NOPE_DIM - 2 * half), F32)
    cos_t = jnp.concatenate([ones, cos, cos, pad], axis=1)
    sin_t = jnp.concatenate([0.0 * ones, -sin, sin, pad], axis=1)
    return cos_t, sin_t


def _even_weights(w_in, w_uq, w_ukv):
    half = MLA_ROPE_DIM // 2
    qk = MLA_NOPE_DIM + MLA_ROPE_DIM
    zeros = lambda r, c: jnp.zeros((r, c), F32)
    sb_v = slice(_X_QK, _X_QK + SB_WIDTH)
    lat = slice(_X_QK + SB_WIDTH, _X_QK + SB_WIDTH + MLA_Q_RANK + MLA_KV_RANK)
    w_kr = w_in[:, lat.stop:]
    k1, k2 = w_kr[:, :half], w_kr[:, half:]
    nope_pad = zeros(D_MODEL, MLA_NOPE_DIM)
    tail_pad = zeros(D_MODEL, MLA_QK_PAD - qk)
    wx = jnp.concatenate([w_in[:, :_X_QK], w_in[:, lat], nope_pad, k1, k2, tail_pad,
                          nope_pad, k2, k1, tail_pad], axis=1)
    wsv = w_in[:, sb_v].T
    wq = w_uq.reshape(MLA_Q_RANK, MLA_HEADS, qk)
    q_nope, q1, q2 = wq[..., :MLA_NOPE_DIM], wq[..., MLA_NOPE_DIM:MLA_NOPE_DIM + half], wq[..., MLA_NOPE_DIM + half:]
    zq = lambda c: jnp.zeros((MLA_Q_RANK, MLA_HEADS, c), F32)
    wqa = jnp.concatenate([q_nope, q1, q2, zq(MLA_QK_PAD - qk)], axis=-1).reshape(MLA_Q_RANK, _QK_WIDTH)
    wqb = jnp.concatenate([zq(MLA_NOPE_DIM), q2, q1, zq(MLA_QK_PAD - qk)], axis=-1).reshape(MLA_Q_RANK, _QK_WIDTH)
    wkv = w_ukv.reshape(MLA_KV_RANK, MLA_HEADS, MLA_NOPE_DIM + MLA_V_DIM)
    wk = jnp.concatenate([wkv[..., :MLA_NOPE_DIM],
                          jnp.zeros((MLA_KV_RANK, MLA_HEADS, MLA_QK_PAD - MLA_NOPE_DIM), F32)],
                         axis=-1).reshape(MLA_KV_RANK, _QK_WIDTH)
    wv = wkv[..., MLA_NOPE_DIM:].reshape(MLA_KV_RANK, MLA_WIDTH).T
    return [w.astype(BF16) for w in (wx, wsv, wqa, wqb, wk, wv)]


def kernel(x, ln_g, ln_b, ffn_w_gate, ffn_w_up, ffn_w_down, mix_w_in, mla_q_norm_g, mla_w_uq,
           mla_kv_norm_g, mla_w_ukv, mix_w_out, conv_w_in, conv_w, conv_w_out):
    batch, seq, _ = x.shape
    assert seq % ROW_TILE == 0 and seq % Q_TILE == 0 and Q_TILE % (2 * K_TILE) == 0
    cos_t, sin_t = _rope_tables(seq)
    h = x.reshape(batch * seq, D_MODEL)
    vec = lambda a: a.reshape(1, -1)
    for layer in range(DEPTH):
        j = layer // 2
        ln = lambda s: (vec(ln_g[layer, s]), vec(ln_b[layer, s]))
        ffn_w = lambda s: (ffn_w_gate[layer, s].astype(BF16), ffn_w_up[layer, s].astype(BF16),
                           ffn_w_down[layer, s].astype(BF16))
        h = _ffn(h, *ffn_w(0), *ln(0))
        if layer % 2 == 0:
            wx, wsv, wqa, wqb, wk, wv = _even_weights(mix_w_in[j], mla_w_uq[j], mla_w_ukv[j])
            sb_qk, sb_vt, q, k, v_t = _proj_even(h, wx, wsv, vec(mla_q_norm_g[j]), wqa, wqb,
                                                 vec(mla_kv_norm_g[j]), wk, wv, cos_t, sin_t, seq)
            out_sb = _sb_attn(sb_qk, sb_vt, batch, seq)
            out_mla = _mla_attn(q, k, v_t, batch, seq)
            h = _post_even(h, out_sb, out_mla, mix_w_out[j].astype(BF16), *ln(1))
        else:
            h = _conv_mixer(h, conv_w_in[j].astype(BF16), conv_w[j], conv_w_out[j].astype(BF16), *ln(1), seq)
        h = _ffn(h, *ffn_w(1), *ln(2))
    return h.reshape(batch, seq, D_MODEL)
```

```python
import functools

import jax
import jax.numpy as jnp
from jax import lax
from jax.experimental import pallas as pl
from jax.experimental.pallas import tpu as pltpu

D_MODEL = 1024
DEPTH = 4
SB_HEADS = 8
SB_HEAD_DIM = 64
MLA_HEADS = 8
MLA_NOPE_DIM = 64
MLA_ROPE_DIM = 32
MLA_V_DIM = 64
MLA_Q_RANK = 768
MLA_KV_RANK = 256
ROPE_THETA = 10000.0
CONV_K = 3
D_FF = 2816
LN_EPS = 1e-5
RMS_EPS = 1e-6
ALPHA = (2 * DEPTH) ** 0.25
SB_WIDTH = SB_HEADS * SB_HEAD_DIM
MLA_WIDTH = MLA_HEADS * MLA_V_DIM

LANES = 128
HEAD_PAIRS = SB_HEADS // 2
MLA_QK_PAD = LANES
FF_CHUNK = 256
ROW_TILE = 512
Q_TILE = 512
K_TILE = 256
VMEM_LIMIT = 56 * 1024 * 1024

BF16 = jnp.bfloat16
F32 = jnp.float32
NEG_BIG = -1e30
LOG2E = 1.4426950408889634


def _dot(a, b):
    return jnp.dot(a, b, preferred_element_type=F32)


def _dot_nt(a, b):
    return lax.dot_general(a, b, (((1,), (1,)), ((), ())), preferred_element_type=F32)


def _layer_norm(y, g, b):
    mu = jnp.mean(y, axis=-1, keepdims=True)
    d = y - mu
    var = jnp.mean(d * d, axis=-1, keepdims=True)
    return d * lax.rsqrt(var + LN_EPS) * g + b


def _rms_norm(c, g):
    return c * lax.rsqrt(jnp.mean(c * c, axis=-1, keepdims=True) + RMS_EPS) * g


def _params(*sem):
    return pltpu.CompilerParams(dimension_semantics=sem, vmem_limit_bytes=VMEM_LIMIT)


def _resident(shape):
    nd = len(shape)
    return pl.BlockSpec(shape, lambda *_: (0,) * nd, pipeline_mode=pl.Buffered(1))


def _ffn_kernel(x_ref, wg_ref, wu_ref, wd_ref, g_ref, b_ref, o_ref, h_ref):
    x = x_ref[...]
    xb = x.astype(BF16)
    for c in range(D_FF // FF_CHUNK):
        sl = slice(c * FF_CHUNK, (c + 1) * FF_CHUNK)
        gate = _dot(xb, wg_ref[:, sl])
        up = _dot(xb, wu_ref[:, sl])
        h_ref[:, sl] = (gate * (1.0 / (1.0 + jnp.exp(-gate))) * up).astype(BF16)
    f = _dot(h_ref[...], wd_ref[...])
    o_ref[...] = _layer_norm(ALPHA * x + 0.5 * f, g_ref[...], b_ref[...])


def _ffn(x, wg, wu, wd, g, b):
    n = x.shape[0]
    row = pl.BlockSpec((ROW_TILE, D_MODEL), lambda i: (i, 0))
    return pl.pallas_call(
        _ffn_kernel,
        out_shape=jax.ShapeDtypeStruct((n, D_MODEL), F32),
        grid=(n // ROW_TILE,),
        in_specs=[row, _resident(wg.shape), _resident(wu.shape), _resident(wd.shape),
                  _resident(g.shape), _resident(b.shape)],
        out_specs=row,
        scratch_shapes=[pltpu.VMEM((ROW_TILE, D_FF), BF16)],
        compiler_params=_params("parallel"),
        name="ffn",
    )(x, wg, wu, wd, g, b)


_X_QK = 2 * SB_WIDTH
_X_CQ = _X_QK + MLA_Q_RANK
_X_CKV = _X_CQ + MLA_KV_RANK
_X_KRA = _X_CKV + LANES
_X_KRB = _X_KRA + LANES
_QK_WIDTH = MLA_HEADS * MLA_QK_PAD


def _proj_even_kernel(x_ref, wx_ref, wsv_ref, gq_ref, wqa_ref, wqb_ref, gkv_ref, wk_ref, wv_ref,
                      cos_ref, sin_ref, sb_ref, sbv_ref, q_ref, k_ref, v_ref):
    xb = x_ref[...].astype(BF16)
    cos = cos_ref[...]
    sin = sin_ref[...]
    sb_ref[:, :SB_WIDTH] = (_dot(xb, wx_ref[:, :SB_WIDTH]) * (SB_HEAD_DIM ** -0.5)).astype(BF16)
    sb_ref[:, SB_WIDTH:] = _dot(xb, wx_ref[:, SB_WIDTH:_X_QK]).astype(BF16)
    sbv_ref[...] = _dot_nt(wsv_ref[...], xb).astype(BF16)
    cq = _rms_norm(_dot(xb, wx_ref[:, _X_QK:_X_CQ]), gq_ref[...]).astype(BF16)
    for h in range(MLA_HEADS):
        sl = slice(h * MLA_QK_PAD, (h + 1) * MLA_QK_PAD)
        q_ref[:, sl] = (_dot(cq, wqa_ref[:, sl]) * cos + _dot(cq, wqb_ref[:, sl]) * sin).astype(BF16)
    ckv = _rms_norm(_dot(xb, wx_ref[:, _X_CQ:_X_CKV]), gkv_ref[...]).astype(BF16)
    kr = _dot(xb, wx_ref[:, _X_CKV:_X_KRA]) * cos + _dot(xb, wx_ref[:, _X_KRA:_X_KRB]) * sin
    for h in range(MLA_HEADS):
        sl = slice(h * MLA_QK_PAD, (h + 1) * MLA_QK_PAD)
        k_ref[:, sl] = (_dot(ckv, wk_ref[:, sl]) + kr).astype(BF16)
    v_ref[...] = _dot_nt(wv_ref[...], ckv).astype(BF16)


def _proj_even(x, wx, wsv, gq, wqa, wqb, gkv, wk, wv, cos_t, sin_t, seq):
    n = x.shape[0]
    tiles_per_seq = seq // ROW_TILE
    row = lambda w: pl.BlockSpec((ROW_TILE, w), lambda i: (i, 0))
    col = lambda h: pl.BlockSpec((h, ROW_TILE), lambda i: (0, i))
    pos = pl.BlockSpec((ROW_TILE, LANES), lambda i: (i % tiles_per_seq, 0))
    return pl.pallas_call(
        _proj_even_kernel,
        out_shape=(jax.ShapeDtypeStruct((n, _X_QK), BF16),
                   jax.ShapeDtypeStruct((SB_WIDTH, n), BF16),
                   jax.ShapeDtypeStruct((n, _QK_WIDTH), BF16),
                   jax.ShapeDtypeStruct((n, _QK_WIDTH), BF16),
                   jax.ShapeDtypeStruct((MLA_WIDTH, n), BF16)),
        grid=(n // ROW_TILE,),
        in_specs=[row(D_MODEL), _resident(wx.shape), _resident(wsv.shape), _resident(gq.shape),
                  _resident(wqa.shape), _resident(wqb.shape), _resident(gkv.shape),
                  _resident(wk.shape), _resident(wv.shape), pos, pos],
        out_specs=(row(_X_QK), col(SB_WIDTH), row(_QK_WIDTH), row(_QK_WIDTH), col(MLA_WIDTH)),
        compiler_params=_params("parallel"),
        name="proj_even",
    )(x, wx, wsv, gq, wqa, wqb, gkv, wk, wv, cos_t, sin_t)


_DIAG_BLOCKS = Q_TILE // K_TILE


def _sweep_keys(i, stages):
    depth = len(stages)
    blocks = (i + 1) * _DIAG_BLOCKS
    lead = [depth - 1 - s for s in range(depth)]
    peel = max(0, _DIAG_BLOCKS - depth + 1)
    for n in range(1 - depth, peel):
        for s, stage in enumerate(stages):
            b = n + lead[s]
            if b >= 0:
                stage(blocks - 1 - b, b % 2, b < _DIAG_BLOCKS)

    def body(t, carry):
        for u in range(2):
            for s, stage in enumerate(stages):
                b_off = peel + u + lead[s]
                stage(blocks - 1 - b_off - 2 * t, b_off % 2, False)
        return carry

    lax.fori_loop(0, i, body, 0)
    for e in range(depth - 1):
        for s, stage in enumerate(stages):
            j = depth - 2 - e - lead[s]
            if j >= 0:
                stage(j, (j + 1) % 2, False)


def _hidden(i, j, visible_when_equal):
    key = j * K_TILE + lax.broadcasted_iota(jnp.int32, (K_TILE, Q_TILE), 0)
    qry = i * Q_TILE + lax.broadcasted_iota(jnp.int32, (K_TILE, Q_TILE), 1)
    return key > qry if visible_when_equal else key >= qry


def _key_block(j):
    return pl.ds(pl.multiple_of(j * K_TILE, K_TILE), K_TILE)


def _merge_heads(o0, o1):
    half = LANES // 2
    return jnp.concatenate([o0[:half, :], o1[half:, :]], axis=0).T


_TILE_SLOTS = pltpu.VMEM((2, 2, K_TILE, Q_TILE), F32)


def _attn_call(body, name, q, q_spec, k, k_spec, v_t, batch, seq, width, state_shapes):
    n = q.shape[0]
    nq = seq // Q_TILE
    return pl.pallas_call(
        body,
        out_shape=jax.ShapeDtypeStruct((n, width), BF16),
        grid=(batch, HEAD_PAIRS, nq),
        in_specs=[q_spec, k_spec, pl.BlockSpec((LANES, seq), lambda b, p, i: (p, b))],
        out_specs=pl.BlockSpec((Q_TILE, LANES), lambda b, p, i: (b * nq + i, p)),
        scratch_shapes=[_TILE_SLOTS] + state_shapes,
        compiler_params=_params("parallel", "parallel", "parallel"),
        name=name,
    )(q, k, v_t)


def _sb_attn_kernel(q_ref, k_ref, vt_ref, o_ref, z_ref, e_ref, total_ref, acc_ref, later_ref):
    i = pl.program_id(2)
    q = q_ref[...]
    lane = lax.broadcasted_iota(jnp.int32, (Q_TILE, LANES), 1)
    qh = [jnp.where(lane < SB_HEAD_DIM, q, jnp.zeros_like(q)),
          jnp.where(lane >= SB_HEAD_DIM, q, jnp.zeros_like(q))]
    r = lax.broadcasted_iota(jnp.int32, (K_TILE, K_TILE), 0)
    c = lax.broadcasted_iota(jnp.int32, (K_TILE, K_TILE), 1)
    tri = jnp.where(c > r, 1.0, 0.0).astype(BF16)
    tri2 = jnp.concatenate([tri, tri], axis=1)
    acc_ref[...] = jnp.zeros_like(acc_ref)
    later_ref[...] = jnp.zeros_like(later_ref)

    def scores(j, slot, masked):
        k = k_ref[_key_block(j), :]
        for h in range(2):
            z = _dot_nt(k, qh[h])
            if masked:
                z = jnp.where(_hidden(i, j, visible_when_equal=False), NEG_BIG, z)
            z_ref[slot, h] = z

    def within_block(j, slot, masked):
        for h in range(2):
            z = z_ref[slot, h]
            neg_abs = lax.bitcast_convert_type(
                lax.bitcast_convert_type(z, jnp.uint32) | jnp.uint32(0x80000000), F32)
            log_beta = jnp.minimum(z, 0.0) - jnp.log(1.0 + jnp.exp(neg_abs))
            log_not = log_beta - z
            hi = log_not.astype(BF16)
            lo = (log_not - hi.astype(F32)).astype(BF16)
            later = _dot(tri2, jnp.concatenate([hi, lo], axis=0))
            e_ref[slot, h] = log_beta + later
            total_ref[slot, h] = later[0:1, :] + log_not[0:1, :]

    def accumulate(j, slot, masked):
        vt = vt_ref[:, _key_block(j)]
        for h in range(2):
            w = jnp.exp(e_ref[slot, h] + later_ref[h])
            acc_ref[h] += _dot(vt, w.astype(BF16))
            later_ref[h] += total_ref[slot, h]

    _sweep_keys(i, [scores, within_block, accumulate])
    o_ref[...] = _merge_heads(acc_ref[0], acc_ref[1]).astype(o_ref.dtype)


def _sb_attn(qk, v_t, batch, seq):
    nq = seq // Q_TILE
    return _attn_call(
        _sb_attn_kernel, "sb_attn",
        qk, pl.BlockSpec((Q_TILE, LANES), lambda b, p, i: (b * nq + i, p)),
        qk, pl.BlockSpec((seq, LANES), lambda b, p, i: (b, HEAD_PAIRS + p)),
        v_t, batch, seq, SB_WIDTH,
        [_TILE_SLOTS, pltpu.VMEM((2, 2, 1, Q_TILE), F32),
         pltpu.VMEM((2, LANES, Q_TILE), F32), pltpu.VMEM((2, 1, Q_TILE), F32)])


def _mla_attn_kernel(q_ref, k_ref, vt_ref, o_ref, s_ref, acc_ref, m_ref, l_ref):
    i = pl.program_id(2)
    c2 = (MLA_NOPE_DIM + MLA_ROPE_DIM) ** -0.5 * LOG2E
    qh = [q_ref[:, h * MLA_QK_PAD:(h + 1) * MLA_QK_PAD] for h in range(2)]
    acc_ref[...] = jnp.zeros_like(acc_ref)
    l_ref[...] = jnp.zeros_like(l_ref)
    m_ref[...] = jnp.full_like(m_ref, NEG_BIG)

    def scores(j, slot, masked):
        for h in range(2):
            s = _dot_nt(k_ref[_key_block(j), h * MLA_QK_PAD:(h + 1) * MLA_QK_PAD], qh[h])
            if masked:
                s = jnp.where(_hidden(i, j, visible_when_equal=True), NEG_BIG, s)
            s_ref[slot, h] = s

    def consume(j, slot, masked):
        vt = vt_ref[:, _key_block(j)]
        for h in range(2):
            s = s_ref[slot, h]
            m = m_ref[h]
            m_new = jnp.maximum(m, jnp.max(s, axis=0, keepdims=True))
            p = jnp.exp2((s - m_new) * c2)
            rescale = jnp.exp2((m - m_new) * c2)
            m_ref[h] = m_new
            l_ref[h] = rescale * l_ref[h] + jnp.sum(p, axis=0, keepdims=True)
            acc_ref[h] = rescale * acc_ref[h] + _dot(vt, p.astype(BF16))

    _sweep_keys(i, [scores, consume])
    o_ref[...] = _merge_heads(acc_ref[0] / l_ref[0], acc_ref[1] / l_ref[1]).astype(o_ref.dtype)


def _mla_attn(q, k, v_t, batch, seq):
    nq = seq // Q_TILE
    return _attn_call(
        _mla_attn_kernel, "mla_attn",
        q, pl.BlockSpec((Q_TILE, 2 * MLA_QK_PAD), lambda b, p, i: (b * nq + i, p)),
        k, pl.BlockSpec((seq, 2 * MLA_QK_PAD), lambda b, p, i: (b, p)),
        v_t, batch, seq, MLA_WIDTH,
        [pltpu.VMEM((2, LANES, Q_TILE), F32), pltpu.VMEM((2, 1, Q_TILE), F32),
         pltpu.VMEM((2, 1, Q_TILE), F32)])


def _post_even_kernel(x_ref, sb_ref, mla_ref, w_ref, g_ref, b_ref, o_ref):
    m = _dot(sb_ref[...], w_ref[:SB_WIDTH, :]) + _dot(mla_ref[...], w_ref[SB_WIDTH:, :])
    o_ref[...] = _layer_norm(ALPHA * x_ref[...] + m, g_ref[...], b_ref[...])


def _post_even(x, sb, mla, w, g, b):
    n = x.shape[0]
    row = lambda wd: pl.BlockSpec((ROW_TILE, wd), lambda i: (i, 0))
    return pl.pallas_call(
        _post_even_kernel,
        out_shape=jax.ShapeDtypeStruct((n, D_MODEL), F32),
        grid=(n // ROW_TILE,),
        in_specs=[row(D_MODEL), row(SB_WIDTH), row(MLA_WIDTH), _resident(w.shape),
                  _resident(g.shape), _resident(b.shape)],
        out_specs=row(D_MODEL),
        compiler_params=_params("parallel"),
        name="post_even",
    )(x, sb, mla, w, g, b)


HALO = 8


def _conv_mixer_kernel(tiles_per_seq, x_ref, win_ref, cw_ref, wout_ref, g_ref, b_ref, o_ref, tail_ref):
    i = pl.program_id(0)
    x = x_ref[...]
    xb = x.astype(BF16)
    gate_b = _dot(xb, win_ref[:, :D_MODEL])
    u = _dot(xb, win_ref[:, D_MODEL:2 * D_MODEL]) * _dot(xb, win_ref[:, 2 * D_MODEL:])

    @pl.when(i % tiles_per_seq == 0)
    def _():
        tail_ref[...] = jnp.zeros_like(tail_ref)

    prev = tail_ref[...]
    tail_ref[...] = u[ROW_TILE - HALO:, :]
    ext = jnp.concatenate([prev, u], axis=0)
    cw = cw_ref[...]
    conv = (cw[0:1, :] * ext[HALO - 2:HALO - 2 + ROW_TILE, :]
            + cw[1:2, :] * ext[HALO - 1:HALO - 1 + ROW_TILE, :]
            + cw[2:3, :] * u)
    m = _dot((gate_b * conv).astype(BF16), wout_ref[...])
    o_ref[...] = _layer_norm(ALPHA * x + m, g_ref[...], b_ref[...])


def _conv_mixer(x, w_in, conv_w, w_out, g, b, seq):
    n = x.shape[0]
    row = pl.BlockSpec((ROW_TILE, D_MODEL), lambda i: (i, 0))
    return pl.pallas_call(
        functools.partial(_conv_mixer_kernel, seq // ROW_TILE),
        out_shape=jax.ShapeDtypeStruct((n, D_MODEL), F32),
        grid=(n // ROW_TILE,),
        in_specs=[row, _resident(w_in.shape), _resident(conv_w.shape), _resident(w_out.shape),
                  _resident(g.shape), _resident(b.shape)],
        out_specs=row,
        scratch_shapes=[pltpu.VMEM((HALO, D_MODEL), F32)],
        compiler_params=_params("arbitrary"),
        name="conv_mixer",
    )(x, w_in, conv_w, w_out, g, b)


def _rope_tables(seq):
    half = MLA_ROPE_DIM // 2
    inv_freq = ROPE_THETA ** (-jnp.arange(0, MLA_ROPE_DIM, 2, dtype=F32) / MLA_ROPE_DIM)
    ang = jnp.arange(seq, dtype=F32)[:, None] * inv_freq[None, :]
    cos, sin = jnp.cos(ang), jnp.sin(ang)
    ones = jnp.ones((seq, MLA_NOPE_DIM), F32)
    pad = jnp.zeros((seq, MLA_QK_PAD - MLA_NOPE_DIM - 2 * half), F32)
    cos_t = jnp.concatenate([ones, cos, cos, pad], axis=1)
    sin_t = jnp.concatenate([0.0 * ones, -sin, sin, pad], axis=1)
    return cos_t, sin_t


def _even_weights(w_in, w_uq, w_ukv):
    half = MLA_ROPE_DIM // 2
    qk = MLA_NOPE_DIM + MLA_ROPE_DIM
    zeros = lambda r, c: jnp.zeros((r, c), F32)
    sb_v = slice(_X_QK, _X_QK + SB_WIDTH)
    lat = slice(_X_QK + SB_WIDTH, _X_QK + SB_WIDTH + MLA_Q_RANK + MLA_KV_RANK)
    w_kr = w_in[:, lat.stop:]
    k1, k2 = w_kr[:, :half], w_kr[:, half:]
    nope_pad = zeros(D_MODEL, MLA_NOPE_DIM)
    tail_pad = zeros(D_MODEL, MLA_QK_PAD - qk)
    wx = jnp.concatenate([w_in[:, :_X_QK], w_in[:, lat], nope_pad, k1, k2, tail_pad,
                          nope_pad, k2, k1, tail_pad], axis=1)
    wsv = w_in[:, sb_v].T
    wq = w_uq.reshape(MLA_Q_RANK, MLA_HEADS, qk)
    q_nope, q1, q2 = wq[..., :MLA_NOPE_DIM], wq[..., MLA_NOPE_DIM:MLA_NOPE_DIM + half], wq[..., MLA_NOPE_DIM + half:]
    zq = lambda c: jnp.zeros((MLA_Q_RANK, MLA_HEADS, c), F32)
    wqa = jnp.concatenate([q_nope, q1, q2, zq(MLA_QK_PAD - qk)], axis=-1).reshape(MLA_Q_RANK, _QK_WIDTH)
    wqb = jnp.concatenate([zq(MLA_NOPE_DIM), q2, q1, zq(MLA_QK_PAD - qk)], axis=-1).reshape(MLA_Q_RANK, _QK_WIDTH)
    wkv = w_ukv.reshape(MLA_KV_RANK, MLA_HEADS, MLA_NOPE_DIM + MLA_V_DIM)
    wk = jnp.concatenate([wkv[..., :MLA_NOPE_DIM],
                          jnp.zeros((MLA_KV_RANK, MLA_HEADS, MLA_QK_PAD - MLA_NOPE_DIM), F32)],
                         axis=-1).reshape(MLA_KV_RANK, _QK_WIDTH)
    wv = wkv[..., MLA_NOPE_DIM:].reshape(MLA_KV_RANK, MLA_WIDTH).T
    return [w.astype(BF16) for w in (wx, wsv, wqa, wqb, wk, wv)]


def kernel(x, ln_g, ln_b, ffn_w_gate, ffn_w_up, ffn_w_down, mix_w_in, mla_q_norm_g, mla_w_uq,
           mla_kv_norm_g, mla_w_ukv, mix_w_out, conv_w_in, conv_w, conv_w_out):
    batch, seq, _ = x.shape
    assert seq % ROW_TILE == 0 and seq % Q_TILE == 0 and _DIAG_BLOCKS == 2
    cos_t, sin_t = _rope_tables(seq)
    h = x.reshape(batch * seq, D_MODEL)
    vec = lambda a: a.reshape(1, -1)
    for layer in range(DEPTH):
        j = layer // 2
        ln = lambda s: (vec(ln_g[layer, s]), vec(ln_b[layer, s]))
        ffn_w = lambda s: (ffn_w_gate[layer, s].astype(BF16), ffn_w_up[layer, s].astype(BF16),
                           ffn_w_down[layer, s].astype(BF16))
        h = _ffn(h, *ffn_w(0), *ln(0))
        if layer % 2 == 0:
            wx, wsv, wqa, wqb, wk, wv = _even_weights(mix_w_in[j], mla_w_uq[j], mla_w_ukv[j])
            sb_qk, sb_vt, q, k, v_t = _proj_even(h, wx, wsv, vec(mla_q_norm_g[j]), wqa, wqb,
                                                 vec(mla_kv_norm_g[j]), wk, wv, cos_t, sin_t, seq)
            out_sb = _sb_attn(sb_qk, sb_vt, batch, seq)
            out_mla = _mla_attn(q, k, v_t, batch, seq)
            h = _post_even(h, out_sb, out_mla, mix_w_out[j].astype(BF16), *ln(1))
        else:
            h = _conv_mixer(h, conv_w_in[j].astype(BF16), conv_w[j], conv_w_out[j].astype(BF16), *ln(1), seq)
        h = _ffn(h, *ffn_w(1), *ln(2))
    return h.reshape(batch, seq, D_MODEL)
```

```python
import functools

import jax
import jax.numpy as jnp
from jax import lax
from jax.experimental import pallas as pl
from jax.experimental.pallas import tpu as pltpu

D_MODEL = 1024
DEPTH = 4
SB_HEADS = 8
SB_HEAD_DIM = 64
MLA_HEADS = 8
MLA_NOPE_DIM = 64
MLA_ROPE_DIM = 32
MLA_V_DIM = 64
MLA_Q_RANK = 768
MLA_KV_RANK = 256
ROPE_THETA = 10000.0
CONV_K = 3
D_FF = 2816
LN_EPS = 1e-5
RMS_EPS = 1e-6
ALPHA = (2 * DEPTH) ** 0.25
SB_WIDTH = SB_HEADS * SB_HEAD_DIM
MLA_WIDTH = MLA_HEADS * MLA_V_DIM

LANES = 128
HEAD_PAIRS = SB_HEADS // 2
MLA_QK_PAD = LANES
_VT_PAD = LANES
FF_CHUNK = 256
ROW_TILE = 512
Q_TILE = 512
K_TILE = 256
VMEM_LIMIT = 56 * 1024 * 1024
PAIRS_PER_TRIP = 2
SB_ISSUE_ORDER = (0, 1, 2)
MLA_ISSUE_ORDER = (0, 1)

BF16 = jnp.bfloat16
F32 = jnp.float32
NEG_BIG = -1e30
LOG2E = 1.4426950408889634


def _dot(a, b):
    return jnp.dot(a, b, preferred_element_type=F32)


def _dot_nt(a, b):
    return lax.dot_general(a, b, (((1,), (1,)), ((), ())), preferred_element_type=F32)


def _layer_norm(y, g, b):
    mu = jnp.mean(y, axis=-1, keepdims=True)
    d = y - mu
    var = jnp.mean(d * d, axis=-1, keepdims=True)
    return d * lax.rsqrt(var + LN_EPS) * g + b


def _rms_norm(c, g):
    return c * lax.rsqrt(jnp.mean(c * c, axis=-1, keepdims=True) + RMS_EPS) * g


def _params(*sem):
    return pltpu.CompilerParams(dimension_semantics=sem, vmem_limit_bytes=VMEM_LIMIT)


def _resident(shape):
    nd = len(shape)
    return pl.BlockSpec(shape, lambda *_: (0,) * nd, pipeline_mode=pl.Buffered(1))


def _ffn_kernel(x_ref, wg_ref, wu_ref, wd_ref, g_ref, b_ref, o_ref, h_ref):
    x = x_ref[...]
    xb = x.astype(BF16)
    for c in range(D_FF // FF_CHUNK):
        sl = slice(c * FF_CHUNK, (c + 1) * FF_CHUNK)
        gate = _dot(xb, wg_ref[:, sl])
        up = _dot(xb, wu_ref[:, sl])
        h_ref[:, sl] = (gate * (1.0 / (1.0 + jnp.exp(-gate))) * up).astype(BF16)
    f = _dot(h_ref[...], wd_ref[...])
    o_ref[...] = _layer_norm(ALPHA * x + 0.5 * f, g_ref[...], b_ref[...])


def _ffn(x, wg, wu, wd, g, b):
    n = x.shape[0]
    row = pl.BlockSpec((ROW_TILE, D_MODEL), lambda i: (i, 0))
    return pl.pallas_call(
        _ffn_kernel,
        out_shape=jax.ShapeDtypeStruct((n, D_MODEL), F32),
        grid=(n // ROW_TILE,),
        in_specs=[row, _resident(wg.shape), _resident(wu.shape), _resident(wd.shape),
                  _resident(g.shape), _resident(b.shape)],
        out_specs=row,
        scratch_shapes=[pltpu.VMEM((ROW_TILE, D_FF), BF16)],
        compiler_params=_params("parallel"),
        name="ffn",
    )(x, wg, wu, wd, g, b)


_X_QK = 2 * SB_WIDTH
_X_CQ = _X_QK + MLA_Q_RANK
_X_CKV = _X_CQ + MLA_KV_RANK
_X_KRA = _X_CKV + LANES
_X_KRB = _X_KRA + LANES
_QK_WIDTH = MLA_HEADS * MLA_QK_PAD


def _proj_even_kernel(x_ref, wx_ref, wsv_ref, gq_ref, wqa_ref, wqb_ref, gkv_ref, wk_ref, wv_ref,
                      cos_ref, sin_ref, sb_ref, sbv_ref, q_ref, k_ref, v_ref):
    xb = x_ref[...].astype(BF16)
    cos = cos_ref[...]
    sin = sin_ref[...]
    sb_ref[:, :SB_WIDTH] = (_dot(xb, wx_ref[:, :SB_WIDTH]) * (SB_HEAD_DIM ** -0.5)).astype(BF16)
    sb_ref[:, SB_WIDTH:] = _dot(xb, wx_ref[:, SB_WIDTH:_X_QK]).astype(BF16)
    sbv_ref[...] = _dot_nt(wsv_ref[...], xb).astype(BF16)
    cq = _rms_norm(_dot(xb, wx_ref[:, _X_QK:_X_CQ]), gq_ref[...]).astype(BF16)
    for h in range(MLA_HEADS):
        sl = slice(h * MLA_QK_PAD, (h + 1) * MLA_QK_PAD)
        q_ref[:, sl] = (_dot(cq, wqa_ref[:, sl]) * cos + _dot(cq, wqb_ref[:, sl]) * sin).astype(BF16)
    ckv = _rms_norm(_dot(xb, wx_ref[:, _X_CQ:_X_CKV]), gkv_ref[...]).astype(BF16)
    kr = _dot(xb, wx_ref[:, _X_CKV:_X_KRA]) * cos + _dot(xb, wx_ref[:, _X_KRA:_X_KRB]) * sin
    for h in range(MLA_HEADS):
        sl = slice(h * MLA_QK_PAD, (h + 1) * MLA_QK_PAD)
        k_ref[:, sl] = (_dot(ckv, wk_ref[:, sl]) + kr).astype(BF16)
    vt = _dot_nt(wv_ref[...], ckv)
    ones_row = lax.broadcasted_iota(jnp.int32, vt.shape, 0) % _VT_PAD == MLA_V_DIM
    v_ref[...] = jnp.where(ones_row, 1.0, vt).astype(BF16)


def _proj_even(x, wx, wsv, gq, wqa, wqb, gkv, wk, wv, cos_t, sin_t, seq):
    n = x.shape[0]
    tiles_per_seq = seq // ROW_TILE
    row = lambda w: pl.BlockSpec((ROW_TILE, w), lambda i: (i, 0))
    col = lambda h: pl.BlockSpec((h, ROW_TILE), lambda i: (0, i))
    pos = pl.BlockSpec((ROW_TILE, LANES), lambda i: (i % tiles_per_seq, 0))
    return pl.pallas_call(
        _proj_even_kernel,
        out_shape=(jax.ShapeDtypeStruct((n, _X_QK), BF16),
                   jax.ShapeDtypeStruct((SB_WIDTH, n), BF16),
                   jax.ShapeDtypeStruct((n, _QK_WIDTH), BF16),
                   jax.ShapeDtypeStruct((n, _QK_WIDTH), BF16),
                   jax.ShapeDtypeStruct((MLA_HEADS * _VT_PAD, n), BF16)),
        grid=(n // ROW_TILE,),
        in_specs=[row(D_MODEL), _resident(wx.shape), _resident(wsv.shape), _resident(gq.shape),
                  _resident(wqa.shape), _resident(wqb.shape), _resident(gkv.shape),
                  _resident(wk.shape), _resident(wv.shape), pos, pos],
        out_specs=(row(_X_QK), col(SB_WIDTH), row(_QK_WIDTH), row(_QK_WIDTH), col(MLA_HEADS * _VT_PAD)),
        compiler_params=_params("parallel"),
        name="proj_even",
    )(x, wx, wsv, gq, wqa, wqb, gkv, wk, wv, cos_t, sin_t)


_DIAG_BLOCKS = Q_TILE // K_TILE


def _sweep_keys(i, stages, issue_order):
    depth = len(stages)
    blocks = (i + 1) * _DIAG_BLOCKS
    lead = [depth - 1 - s for s in range(depth)]
    peel = max(0, _DIAG_BLOCKS - depth + 1)
    for n in range(1 - depth, peel):
        for s, stage in enumerate(stages):
            b = n + lead[s]
            if b >= 0:
                stage(blocks - 1 - b, b % 2, b < _DIAG_BLOCKS)

    def step_pair(t):
        for u in range(2):
            for s in issue_order:
                b_off = peel + u + lead[s]
                stages[s](blocks - 1 - b_off - 2 * t, b_off % 2, False)

    def body(t, carry):
        for r in range(PAIRS_PER_TRIP):
            step_pair(PAIRS_PER_TRIP * t + r)
        return carry

    lax.fori_loop(0, i // PAIRS_PER_TRIP, body, 0)
    for r in range(PAIRS_PER_TRIP - 1):
        pl.when(r < i % PAIRS_PER_TRIP)(functools.partial(step_pair, i - i % PAIRS_PER_TRIP + r))
    for e in range(depth - 1):
        for s, stage in enumerate(stages):
            j = depth - 2 - e - lead[s]
            if j >= 0:
                stage(j, (j + 1) % 2, False)


def _hidden(i, j, visible_when_equal):
    key = j * K_TILE + lax.broadcasted_iota(jnp.int32, (K_TILE, Q_TILE), 0)
    qry = i * Q_TILE + lax.broadcasted_iota(jnp.int32, (K_TILE, Q_TILE), 1)
    return key > qry if visible_when_equal else key >= qry


def _key_block(j):
    return pl.ds(pl.multiple_of(j * K_TILE, K_TILE), K_TILE)


def _merge_heads(o0, o1):
    return jnp.concatenate([o0, o1], axis=0).T


class _Slots:
    def __init__(self, first, second):
        self._refs = (first, second)

    def __getitem__(self, idx):
        return self._refs[idx[0]][idx[1:]]

    def __setitem__(self, idx, value):
        self._refs[idx[0]][idx[1:]] = value


def _slot_pair(shape, dtype):
    return [pltpu.VMEM(shape, dtype)] * 2


_TILE = (2, K_TILE, Q_TILE)


def _attn_call(body, name, q, q_spec, k, k_spec, v_t, vt_rows, batch, seq, width, state_shapes):
    n = q.shape[0]
    nq = seq // Q_TILE
    return pl.pallas_call(
        body,
        out_shape=jax.ShapeDtypeStruct((n, width), BF16),
        grid=(batch, HEAD_PAIRS, nq),
        in_specs=[q_spec, k_spec, pl.BlockSpec((vt_rows, seq), lambda b, p, i: (p, b))],
        out_specs=pl.BlockSpec((Q_TILE, LANES), lambda b, p, i: (b * nq + i, p)),
        scratch_shapes=_slot_pair(_TILE, F32) + state_shapes,
        compiler_params=_params("parallel", "parallel", "parallel"),
        name=name,
    )(q, k, v_t)


def _sb_attn_kernel(q_ref, k_ref, vt_ref, o_ref, z0, z1, e0, e1, total0, total1, acc_ref, later_ref):
    z_ref, e_ref, total_ref = _Slots(z0, z1), _Slots(e0, e1), _Slots(total0, total1)
    i = pl.program_id(2)
    q = q_ref[...]
    lane = lax.broadcasted_iota(jnp.int32, (Q_TILE, LANES), 1)
    qh = [jnp.where(lane < SB_HEAD_DIM, q, jnp.zeros_like(q)),
          jnp.where(lane >= SB_HEAD_DIM, q, jnp.zeros_like(q))]
    r = lax.broadcasted_iota(jnp.int32, (K_TILE, K_TILE), 0)
    c = lax.broadcasted_iota(jnp.int32, (K_TILE, K_TILE), 1)
    tri = jnp.where(c > r, 1.0, 0.0).astype(BF16)
    tri2 = jnp.concatenate([tri, tri], axis=1)
    acc_ref[...] = jnp.zeros_like(acc_ref)
    later_ref[...] = jnp.zeros_like(later_ref)

    def scores(j, slot, masked):
        k = k_ref[_key_block(j), :]
        for h in range(2):
            z = _dot_nt(k, qh[h])
            if masked:
                z = jnp.where(_hidden(i, j, visible_when_equal=False), NEG_BIG, z)
            z_ref[slot, h] = z

    def within_block(j, slot, masked):
        for h in range(2):
            z = z_ref[slot, h]
            neg_abs = lax.bitcast_convert_type(
                lax.bitcast_convert_type(z, jnp.uint32) | jnp.uint32(0x80000000), F32)
            log_beta = jnp.minimum(z, 0.0) - jnp.log(1.0 + jnp.exp(neg_abs))
            log_not = log_beta - z
            hi = log_not.astype(BF16)
            lo = (log_not - hi.astype(F32)).astype(BF16)
            later = _dot(tri2, jnp.concatenate([hi, lo], axis=0))
            e_ref[slot, h] = log_beta + later
            total_ref[slot, h] = later[0:1, :] + log_not[0:1, :]

    def accumulate(j, slot, masked):
        vt = vt_ref[:, _key_block(j)]
        for h in range(2):
            w = jnp.exp(e_ref[slot, h] + later_ref[h])
            acc_ref[h] += _dot(vt, w.astype(BF16))
            later_ref[h] += total_ref[slot, h]

    _sweep_keys(i, [scores, within_block, accumulate], SB_ISSUE_ORDER)
    o_ref[...] = _merge_heads(acc_ref[0, :SB_HEAD_DIM, :], acc_ref[1, SB_HEAD_DIM:, :]).astype(o_ref.dtype)


def _sb_attn(qk, v_t, batch, seq):
    nq = seq // Q_TILE
    return _attn_call(
        _sb_attn_kernel, "sb_attn",
        qk, pl.BlockSpec((Q_TILE, LANES), lambda b, p, i: (b * nq + i, p)),
        qk, pl.BlockSpec((seq, LANES), lambda b, p, i: (b, HEAD_PAIRS + p)),
        v_t, LANES, batch, seq, SB_WIDTH,
        _slot_pair(_TILE, F32) + _slot_pair((2, 1, Q_TILE), F32)
        + [pltpu.VMEM((2, LANES, Q_TILE), F32), pltpu.VMEM((2, 1, Q_TILE), F32)])


def _mla_attn_kernel(q_ref, k_ref, vt_ref, o_ref, s0, s1, acc_ref, m_ref):
    s_ref = _Slots(s0, s1)
    i = pl.program_id(2)
    c2 = (MLA_NOPE_DIM + MLA_ROPE_DIM) ** -0.5 * LOG2E
    qh = [q_ref[:, h * MLA_QK_PAD:(h + 1) * MLA_QK_PAD] for h in range(2)]
    acc_ref[...] = jnp.zeros_like(acc_ref)
    m_ref[...] = jnp.full_like(m_ref, NEG_BIG)

    def scores(j, slot, masked):
        for h in range(2):
            s = _dot_nt(k_ref[_key_block(j), h * MLA_QK_PAD:(h + 1) * MLA_QK_PAD], qh[h])
            if masked:
                s = jnp.where(_hidden(i, j, visible_when_equal=True), NEG_BIG, s)
            s_ref[slot, h] = s

    def accumulate(j, slot, masked):
        for h in range(2):
            s = s_ref[slot, h]
            m = m_ref[h]
            m_new = jnp.maximum(m, jnp.max(s, axis=0, keepdims=True))
            p = jnp.exp2((s - m_new) * c2)
            rescale = jnp.exp2((m - m_new) * c2)
            m_ref[h] = m_new
            vt = vt_ref[h * _VT_PAD:(h + 1) * _VT_PAD, _key_block(j)]
            acc_ref[h] = rescale * acc_ref[h] + _dot(vt, p.astype(BF16))

    _sweep_keys(i, [scores, accumulate], MLA_ISSUE_ORDER)
    out = [acc_ref[h, :MLA_V_DIM, :] / acc_ref[h, MLA_V_DIM:MLA_V_DIM + 1, :] for h in range(2)]
    o_ref[...] = _merge_heads(*out).astype(o_ref.dtype)


def _mla_attn(q, k, v_t, batch, seq):
    nq = seq // Q_TILE
    return _attn_call(
        _mla_attn_kernel, "mla_attn",
        q, pl.BlockSpec((Q_TILE, 2 * MLA_QK_PAD), lambda b, p, i: (b * nq + i, p)),
        k, pl.BlockSpec((seq, 2 * MLA_QK_PAD), lambda b, p, i: (b, p)),
        v_t, 2 * _VT_PAD, batch, seq, MLA_WIDTH,
        [pltpu.VMEM((2, _VT_PAD, Q_TILE), F32), pltpu.VMEM((2, 1, Q_TILE), F32)])


def _post_even_kernel(x_ref, sb_ref, mla_ref, w_ref, g_ref, b_ref, o_ref):
    m = _dot(sb_ref[...], w_ref[:SB_WIDTH, :]) + _dot(mla_ref[...], w_ref[SB_WIDTH:, :])
    o_ref[...] = _layer_norm(ALPHA * x_ref[...] + m, g_ref[...], b_ref[...])


def _post_even(x, sb, mla, w, g, b):
    n = x.shape[0]
    row = lambda wd: pl.BlockSpec((ROW_TILE, wd), lambda i: (i, 0))
    return pl.pallas_call(
        _post_even_kernel,
        out_shape=jax.ShapeDtypeStruct((n, D_MODEL), F32),
        grid=(n // ROW_TILE,),
        in_specs=[row(D_MODEL), row(SB_WIDTH), row(MLA_WIDTH), _resident(w.shape),
                  _resident(g.shape), _resident(b.shape)],
        out_specs=row(D_MODEL),
        compiler_params=_params("parallel"),
        name="post_even",
    )(x, sb, mla, w, g, b)


HALO = 8


def _conv_mixer_kernel(tiles_per_seq, x_ref, win_ref, cw_ref, wout_ref, g_ref, b_ref, o_ref, tail_ref):
    i = pl.program_id(0)
    x = x_ref[...]
    xb = x.astype(BF16)
    gate_b = _dot(xb, win_ref[:, :D_MODEL])
    u = _dot(xb, win_ref[:, D_MODEL:2 * D_MODEL]) * _dot(xb, win_ref[:, 2 * D_MODEL:])

    @pl.when(i % tiles_per_seq == 0)
    def _():
        tail_ref[...] = jnp.zeros_like(tail_ref)

    prev = tail_ref[...]
    tail_ref[...] = u[ROW_TILE - HALO:, :]
    ext = jnp.concatenate([prev, u], axis=0)
    cw = cw_ref[...]
    conv = (cw[0:1, :] * ext[HALO - 2:HALO - 2 + ROW_TILE, :]
            + cw[1:2, :] * ext[HALO - 1:HALO - 1 + ROW_TILE, :]
            + cw[2:3, :] * u)
    m = _dot((gate_b * conv).astype(BF16), wout_ref[...])
    o_ref[...] = _layer_norm(ALPHA * x + m, g_ref[...], b_ref[...])


def _conv_mixer(x, w_in, conv_w, w_out, g, b, seq):
    n = x.shape[0]
    row = pl.BlockSpec((ROW_TILE, D_MODEL), lambda i: (i, 0))
    return pl.pallas_call(
        functools.partial(_conv_mixer_kernel, seq // ROW_TILE),
        out_shape=jax.ShapeDtypeStruct((n, D_MODEL), F32),
        grid=(n // ROW_TILE,),
        in_specs=[row, _resident(w_in.shape), _resident(conv_w.shape), _resident(w_out.shape),
                  _resident(g.shape), _resident(b.shape)],
        out_specs=row,
        scratch_shapes=[pltpu.VMEM((HALO, D_MODEL), F32)],
        compiler_params=_params("arbitrary"),
        name="conv_mixer",
    )(x, w_in, conv_w, w_out, g, b)


def _rope_tables(seq):
    half = MLA_ROPE_DIM // 2
    inv_freq = ROPE_THETA ** (-jnp.arange(0, MLA_ROPE_DIM, 2, dtype=F32) / MLA_ROPE_DIM)
    ang = jnp.arange(seq, dtype=F32)[:, None] * inv_freq[None, :]
    cos, sin = jnp.cos(ang), jnp.sin(ang)
    ones = jnp.ones((seq, MLA_NOPE_DIM), F32)
    pad = jnp.zeros((seq, MLA_QK_PAD - MLA_NOPE_DIM - 2 * half), F32)
    cos_t = jnp.concatenate([ones, cos, cos, pad], axis=1)
    sin_t = jnp.concatenate([0.0 * ones, -sin, sin, pad], axis=1)
    return cos_t, sin_t


def _even_weights(w_in, w_uq, w_ukv):
    half = MLA_ROPE_DIM // 2
    qk = MLA_NOPE_DIM + MLA_ROPE_DIM
    zeros = lambda r, c: jnp.zeros((r, c), F32)
    sb_v = slice(_X_QK, _X_QK + SB_WIDTH)
    lat = slice(_X_QK + SB_WIDTH, _X_QK + SB_WIDTH + MLA_Q_RANK + MLA_KV_RANK)
    w_kr = w_in[:, lat.stop:]
    k1, k2 = w_kr[:, :half], w_kr[:, half:]
    nope_pad = zeros(D_MODEL, MLA_NOPE_DIM)
    tail_pad = zeros(D_MODEL, MLA_QK_PAD - qk)
    wx = jnp.concatenate([w_in[:, :_X_QK], w_in[:, lat], nope_pad, k1, k2, tail_pad,
                          nope_pad, k2, k1, tail_pad], axis=1)
    wsv = w_in[:, sb_v].T
    wq = w_uq.reshape(MLA_Q_RANK, MLA_HEADS, qk)
    q_nope, q1, q2 = wq[..., :MLA_NOPE_DIM], wq[..., MLA_NOPE_DIM:MLA_NOPE_DIM + half], wq[..., MLA_NOPE_DIM + half:]
    zq = lambda c: jnp.zeros((MLA_Q_RANK, MLA_HEADS, c), F32)
    wqa = jnp.concatenate([q_nope, q1, q2, zq(MLA_QK_PAD - qk)], axis=-1).reshape(MLA_Q_RANK, _QK_WIDTH)
    wqb = jnp.concatenate([zq(MLA_NOPE_DIM), q2, q1, zq(MLA_QK_PAD - qk)], axis=-1).reshape(MLA_Q_RANK, _QK_WIDTH)
    wkv = w_ukv.reshape(MLA_KV_RANK, MLA_HEADS, MLA_NOPE_DIM + MLA_V_DIM)
    wk = jnp.concatenate([wkv[..., :MLA_NOPE_DIM],
                          jnp.zeros((MLA_KV_RANK, MLA_HEADS, MLA_QK_PAD - MLA_NOPE_DIM), F32)],
                         axis=-1).reshape(MLA_KV_RANK, _QK_WIDTH)
    wv = jnp.concatenate([wkv[..., MLA_NOPE_DIM:],
                          jnp.zeros((MLA_KV_RANK, MLA_HEADS, _VT_PAD - MLA_V_DIM), F32)],
                         axis=-1).reshape(MLA_KV_RANK, MLA_HEADS * _VT_PAD).T
    return [w.astype(BF16) for w in (wx, wsv, wqa, wqb, wk, wv)]


def kernel(x, ln_g, ln_b, ffn_w_gate, ffn_w_up, ffn_w_down, mix_w_in, mla_q_norm_g, mla_w_uq,
           mla_kv_norm_g, mla_w_ukv, mix_w_out, conv_w_in, conv_w, conv_w_out):
    batch, seq, _ = x.shape
    assert seq % ROW_TILE == 0 and seq % Q_TILE == 0 and _DIAG_BLOCKS == 2
    cos_t, sin_t = _rope_tables(seq)
    h = x.reshape(batch * seq, D_MODEL)
    vec = lambda a: a.reshape(1, -1)
    for layer in range(DEPTH):
        j = layer // 2
        ln = lambda s: (vec(ln_g[layer, s]), vec(ln_b[layer, s]))
        ffn_w = lambda s: (ffn_w_gate[layer, s].astype(BF16), ffn_w_up[layer, s].astype(BF16),
                           ffn_w_down[layer, s].astype(BF16))
        h = _ffn(h, *ffn_w(0), *ln(0))
        if layer % 2 == 0:
            wx, wsv, wqa, wqb, wk, wv = _even_weights(mix_w_in[j], mla_w_uq[j], mla_w_ukv[j])
            sb_qk, sb_vt, q, k, v_t = _proj_even(h, wx, wsv, vec(mla_q_norm_g[j]), wqa, wqb,
                                                 vec(mla_kv_norm_g[j]), wk, wv, cos_t, sin_t, seq)
            out_sb = _sb_attn(sb_qk, sb_vt, batch, seq)
            out_mla = _mla_attn(q, k, v_t, batch, seq)
            h = _post_even(h, out_sb, out_mla, mix_w_out[j].astype(BF16), *ln(1))
        else:
            h = _conv_mixer(h, conv_w_in[j].astype(BF16), conv_w[j], conv_w_out[j].astype(BF16), *ln(1), seq)
        h = _ffn(h, *ffn_w(1), *ln(2))
    return h.reshape(batch, seq, D_MODEL)
```

```python
import functools

import jax
import jax.numpy as jnp
from jax import lax
from jax.experimental import pallas as pl
from jax.experimental.pallas import tpu as pltpu

D_MODEL = 1024
DEPTH = 4
SB_HEADS = 8
SB_HEAD_DIM = 64
MLA_HEADS = 8
MLA_NOPE_DIM = 64
MLA_ROPE_DIM = 32
MLA_V_DIM = 64
MLA_Q_RANK = 768
MLA_KV_RANK = 256
ROPE_THETA = 10000.0
CONV_K = 3
D_FF = 2816
LN_EPS = 1e-5
RMS_EPS = 1e-6
ALPHA = (2 * DEPTH) ** 0.25
SB_WIDTH = SB_HEADS * SB_HEAD_DIM
MLA_WIDTH = MLA_HEADS * MLA_V_DIM

LANES = 128
HEAD_PAIRS = SB_HEADS // 2
MLA_QK_PAD = LANES
_VT_PAD = LANES
FF_CHUNK = 256
ROW_TILE = 512
Q_TILE = 512
K_TILE = 256
VMEM_LIMIT = 56 * 1024 * 1024
PAIRS_PER_TRIP = 2
SB_ISSUE_ORDER = (0, 1, 2)
MLA_ISSUE_ORDER = (0, 1)

BF16 = jnp.bfloat16
F32 = jnp.float32
NEG_BIG = -1e30
LOG2E = 1.4426950408889634


def _dot(a, b):
    return jnp.dot(a, b, preferred_element_type=F32)


def _dot_nt(a, b):
    return lax.dot_general(a, b, (((1,), (1,)), ((), ())), preferred_element_type=F32)


def _layer_norm(y, g, b):
    mu = jnp.mean(y, axis=-1, keepdims=True)
    d = y - mu
    var = jnp.mean(d * d, axis=-1, keepdims=True)
    return d * lax.rsqrt(var + LN_EPS) * g + b


def _rms_norm(c, g):
    return c * lax.rsqrt(jnp.mean(c * c, axis=-1, keepdims=True) + RMS_EPS) * g


def _params(*sem):
    return pltpu.CompilerParams(dimension_semantics=sem, vmem_limit_bytes=VMEM_LIMIT)


def _resident(shape):
    nd = len(shape)
    return pl.BlockSpec(shape, lambda *_: (0,) * nd, pipeline_mode=pl.Buffered(1))


def _ffn_kernel(x_ref, wg_ref, wu_ref, wd_ref, g_ref, b_ref, o_ref, h_ref):
    x = x_ref[...]
    xb = x.astype(BF16)
    for c in range(D_FF // FF_CHUNK):
        sl = slice(c * FF_CHUNK, (c + 1) * FF_CHUNK)
        gate = _dot(xb, wg_ref[:, sl])
        up = _dot(xb, wu_ref[:, sl])
        h_ref[:, sl] = (gate * (1.0 / (1.0 + jnp.exp(-gate))) * up).astype(BF16)
    f = _dot(h_ref[...], wd_ref[...])
    o_ref[...] = _layer_norm(ALPHA * x + 0.5 * f, g_ref[...], b_ref[...])


def _ffn(x, wg, wu, wd, g, b):
    n = x.shape[0]
    row = pl.BlockSpec((ROW_TILE, D_MODEL), lambda i: (i, 0))
    return pl.pallas_call(
        _ffn_kernel,
        out_shape=jax.ShapeDtypeStruct((n, D_MODEL), F32),
        grid=(n // ROW_TILE,),
        in_specs=[row, _resident(wg.shape), _resident(wu.shape), _resident(wd.shape),
                  _resident(g.shape), _resident(b.shape)],
        out_specs=row,
        scratch_shapes=[pltpu.VMEM((ROW_TILE, D_FF), BF16)],
        compiler_params=_params("parallel"),
        name="ffn",
    )(x, wg, wu, wd, g, b)


_X_QK = 2 * SB_WIDTH
_X_CQ = _X_QK + MLA_Q_RANK
_X_CKV = _X_CQ + MLA_KV_RANK
_X_KRA = _X_CKV + LANES
_X_KRB = _X_KRA + LANES
_QK_WIDTH = MLA_HEADS * MLA_QK_PAD


def _proj_even_kernel(x_ref, wx_ref, wsv_ref, gq_ref, wqa_ref, wqb_ref, gkv_ref, wk_ref, wv_ref,
                      cos_ref, sin_ref, sb_ref, sbv_ref, q_ref, k_ref, v_ref):
    xb = x_ref[...].astype(BF16)
    cos = cos_ref[...]
    sin = sin_ref[...]
    sb_ref[:, :SB_WIDTH] = (_dot(xb, wx_ref[:, :SB_WIDTH]) * (SB_HEAD_DIM ** -0.5)).astype(BF16)
    sb_ref[:, SB_WIDTH:] = _dot(xb, wx_ref[:, SB_WIDTH:_X_QK]).astype(BF16)
    sbv_ref[...] = _dot_nt(wsv_ref[...], xb).astype(BF16)
    cq = _rms_norm(_dot(xb, wx_ref[:, _X_QK:_X_CQ]), gq_ref[...]).astype(BF16)
    for h in range(MLA_HEADS):
        sl = slice(h * MLA_QK_PAD, (h + 1) * MLA_QK_PAD)
        q_ref[:, sl] = (_dot(cq, wqa_ref[:, sl]) * cos + _dot(cq, wqb_ref[:, sl]) * sin).astype(BF16)
    ckv = _rms_norm(_dot(xb, wx_ref[:, _X_CQ:_X_CKV]), gkv_ref[...]).astype(BF16)
    kr = _dot(xb, wx_ref[:, _X_CKV:_X_KRA]) * cos + _dot(xb, wx_ref[:, _X_KRA:_X_KRB]) * sin
    for h in range(MLA_HEADS):
        sl = slice(h * MLA_QK_PAD, (h + 1) * MLA_QK_PAD)
        k_ref[:, sl] = (_dot(ckv, wk_ref[:, sl]) + kr).astype(BF16)
    vt = _dot_nt(wv_ref[...], ckv)
    ones_row = lax.broadcasted_iota(jnp.int32, vt.shape, 0) % _VT_PAD == MLA_V_DIM
    v_ref[...] = jnp.where(ones_row, 1.0, vt).astype(BF16)


def _proj_even(x, wx, wsv, gq, wqa, wqb, gkv, wk, wv, cos_t, sin_t, seq):
    n = x.shape[0]
    tiles_per_seq = seq // ROW_TILE
    row = lambda w: pl.BlockSpec((ROW_TILE, w), lambda i: (i, 0))
    col = lambda h: pl.BlockSpec((h, ROW_TILE), lambda i: (0, i))
    pos = pl.BlockSpec((ROW_TILE, LANES), lambda i: (i % tiles_per_seq, 0))
    return pl.pallas_call(
        _proj_even_kernel,
        out_shape=(jax.ShapeDtypeStruct((n, _X_QK), BF16),
                   jax.ShapeDtypeStruct((SB_WIDTH, n), BF16),
                   jax.ShapeDtypeStruct((n, _QK_WIDTH), BF16),
                   jax.ShapeDtypeStruct((n, _QK_WIDTH), BF16),
                   jax.ShapeDtypeStruct((MLA_HEADS * _VT_PAD, n), BF16)),
        grid=(n // ROW_TILE,),
        in_specs=[row(D_MODEL), _resident(wx.shape), _resident(wsv.shape), _resident(gq.shape),
                  _resident(wqa.shape), _resident(wqb.shape), _resident(gkv.shape),
                  _resident(wk.shape), _resident(wv.shape), pos, pos],
        out_specs=(row(_X_QK), col(SB_WIDTH), row(_QK_WIDTH), row(_QK_WIDTH), col(MLA_HEADS * _VT_PAD)),
        compiler_params=_params("parallel"),
        name="proj_even",
    )(x, wx, wsv, gq, wqa, wqb, gkv, wk, wv, cos_t, sin_t)


_DIAG_BLOCKS = Q_TILE // K_TILE


def _sweep_keys(i, stages, issue_order):
    depth = len(stages)
    blocks = (i + 1) * _DIAG_BLOCKS
    lead = [depth - 1 - s for s in range(depth)]
    peel = max(0, _DIAG_BLOCKS - depth + 1)
    for n in range(1 - depth, peel):
        for s, stage in enumerate(stages):
            b = n + lead[s]
            if b >= 0:
                stage(blocks - 1 - b, b % 2, b < _DIAG_BLOCKS)

    def step_pair(t):
        for u in range(2):
            for s in issue_order:
                b_off = peel + u + lead[s]
                stages[s](blocks - 1 - b_off - 2 * t, b_off % 2, False)

    def body(t, carry):
        for r in range(PAIRS_PER_TRIP):
            step_pair(PAIRS_PER_TRIP * t + r)
        return carry

    lax.fori_loop(0, i // PAIRS_PER_TRIP, body, 0)
    for r in range(PAIRS_PER_TRIP - 1):
        pl.when(r < i % PAIRS_PER_TRIP)(functools.partial(step_pair, i - i % PAIRS_PER_TRIP + r))
    for e in range(depth - 1):
        for s, stage in enumerate(stages):
            j = depth - 2 - e - lead[s]
            if j >= 0:
                stage(j, (j + 1) % 2, False)


def _hidden(i, j, visible_when_equal):
    key = j * K_TILE + lax.broadcasted_iota(jnp.int32, (K_TILE, Q_TILE), 0)
    qry = i * Q_TILE + lax.broadcasted_iota(jnp.int32, (K_TILE, Q_TILE), 1)
    return key > qry if visible_when_equal else key >= qry


def _key_block(j):
    return pl.ds(pl.multiple_of(j * K_TILE, K_TILE), K_TILE)


def _merge_heads(o0, o1):
    return jnp.concatenate([o0, o1], axis=0).T


class _Slots:
    def __init__(self, first, second):
        self._refs = (first, second)

    def __getitem__(self, idx):
        return self._refs[idx[0]][idx[1:]]

    def __setitem__(self, idx, value):
        self._refs[idx[0]][idx[1:]] = value


def _slot_pair(shape, dtype):
    return [pltpu.VMEM(shape, dtype)] * 2


_TILE = (2, K_TILE, Q_TILE)


def _attn_call(body, name, q, q_spec, k, k_spec, v_t, vt_rows, batch, seq, width, state_shapes):
    n = q.shape[0]
    nq = seq // Q_TILE
    return pl.pallas_call(
        body,
        out_shape=jax.ShapeDtypeStruct((n, width), BF16),
        grid=(batch, HEAD_PAIRS, nq),
        in_specs=[q_spec, k_spec, pl.BlockSpec((vt_rows, seq), lambda b, p, i: (p, b))],
        out_specs=pl.BlockSpec((Q_TILE, LANES), lambda b, p, i: (b * nq + i, p)),
        scratch_shapes=_slot_pair(_TILE, F32) + state_shapes,
        compiler_params=_params("parallel", "parallel", "parallel"),
        name=name,
    )(q, k, v_t)


def _sb_attn_kernel(q_ref, k_ref, vt_ref, o_ref, z0, z1, e0, e1, total0, total1, acc_ref, later_ref):
    z_ref, e_ref, total_ref = _Slots(z0, z1), _Slots(e0, e1), _Slots(total0, total1)
    i = pl.program_id(2)
    q = q_ref[...]
    lane = lax.broadcasted_iota(jnp.int32, (Q_TILE, LANES), 1)
    qh = [jnp.where(lane < SB_HEAD_DIM, q, jnp.zeros_like(q)),
          jnp.where(lane >= SB_HEAD_DIM, q, jnp.zeros_like(q))]
    r = lax.broadcasted_iota(jnp.int32, (K_TILE, K_TILE), 0)
    c = lax.broadcasted_iota(jnp.int32, (K_TILE, K_TILE), 1)
    tri = jnp.where(c > r, 1.0, 0.0).astype(BF16)
    acc_ref[...] = jnp.zeros_like(acc_ref)
    later_ref[...] = jnp.zeros_like(later_ref)

    def scores(j, slot, masked):
        k = k_ref[_key_block(j), :]
        for h in range(2):
            z = _dot_nt(k, qh[h])
            if masked:
                z = jnp.where(_hidden(i, j, visible_when_equal=False), NEG_BIG, z)
            z_ref[slot, h] = z

    def within_block(j, slot, masked):
        for h in range(2):
            z = z_ref[slot, h]
            neg_abs = lax.bitcast_convert_type(
                lax.bitcast_convert_type(z, jnp.uint32) | jnp.uint32(0x80000000), F32)
            softplus = jnp.maximum(z, 0.0) + jnp.log(1.0 + jnp.exp(neg_abs))
            later = _dot(tri, softplus.astype(BF16))
            e_ref[slot, h] = (z - softplus) - later
            total_ref[slot, h] = later[0:1, :] + softplus[0:1, :]

    def accumulate(j, slot, masked):
        vt = vt_ref[:, _key_block(j)]
        for h in range(2):
            w = jnp.exp(e_ref[slot, h])
            acc_ref[h] += _dot(vt, w.astype(BF16)) * jnp.exp(-later_ref[h])
            later_ref[h] += total_ref[slot, h]

    _sweep_keys(i, [scores, within_block, accumulate], SB_ISSUE_ORDER)
    o_ref[...] = _merge_heads(acc_ref[0, :SB_HEAD_DIM, :], acc_ref[1, SB_HEAD_DIM:, :]).astype(o_ref.dtype)


def _sb_attn(qk, v_t, batch, seq):
    nq = seq // Q_TILE
    return _attn_call(
        _sb_attn_kernel, "sb_attn",
        qk, pl.BlockSpec((Q_TILE, LANES), lambda b, p, i: (b * nq + i, p)),
        qk, pl.BlockSpec((seq, LANES), lambda b, p, i: (b, HEAD_PAIRS + p)),
        v_t, LANES, batch, seq, SB_WIDTH,
        _slot_pair(_TILE, F32) + _slot_pair((2, 1, Q_TILE), F32)
        + [pltpu.VMEM((2, LANES, Q_TILE), F32), pltpu.VMEM((2, 1, Q_TILE), F32)])


def _mla_attn_kernel(q_ref, k_ref, vt_ref, o_ref, s0, s1, acc_ref, m_ref):
    s_ref = _Slots(s0, s1)
    i = pl.program_id(2)
    c2 = (MLA_NOPE_DIM + MLA_ROPE_DIM) ** -0.5 * LOG2E
    qh = [q_ref[:, h * MLA_QK_PAD:(h + 1) * MLA_QK_PAD] for h in range(2)]
    acc_ref[...] = jnp.zeros_like(acc_ref)
    m_ref[...] = jnp.full_like(m_ref, NEG_BIG)

    def scores(j, slot, masked):
        for h in range(2):
            s = _dot_nt(k_ref[_key_block(j), h * MLA_QK_PAD:(h + 1) * MLA_QK_PAD], qh[h])
            if masked:
                s = jnp.where(_hidden(i, j, visible_when_equal=True), NEG_BIG, s)
            s_ref[slot, h] = s

    def accumulate(j, slot, masked):
        for h in range(2):
            s = s_ref[slot, h]
            m = m_ref[h]
            m_new = jnp.maximum(m, jnp.max(s, axis=0, keepdims=True))
            p = jnp.exp2((s - m_new) * c2)
            rescale = jnp.exp2((m - m_new) * c2)
            m_ref[h] = m_new
            vt = vt_ref[h * _VT_PAD:(h + 1) * _VT_PAD, _key_block(j)]
            acc_ref[h] = rescale * acc_ref[h] + _dot(vt, p.astype(BF16))

    _sweep_keys(i, [scores, accumulate], MLA_ISSUE_ORDER)
    out = [acc_ref[h, :MLA_V_DIM, :] / acc_ref[h, MLA_V_DIM:MLA_V_DIM + 1, :] for h in range(2)]
    o_ref[...] = _merge_heads(*out).astype(o_ref.dtype)


def _mla_attn(q, k, v_t, batch, seq):
    nq = seq // Q_TILE
    return _attn_call(
        _mla_attn_kernel, "mla_attn",
        q, pl.BlockSpec((Q_TILE, 2 * MLA_QK_PAD), lambda b, p, i: (b * nq + i, p)),
        k, pl.BlockSpec((seq, 2 * MLA_QK_PAD), lambda b, p, i: (b, p)),
        v_t, 2 * _VT_PAD, batch, seq, MLA_WIDTH,
        [pltpu.VMEM((2, _VT_PAD, Q_TILE), F32), pltpu.VMEM((2, 1, Q_TILE), F32)])


def _post_even_kernel(x_ref, sb_ref, mla_ref, w_ref, g_ref, b_ref, o_ref):
    m = _dot(sb_ref[...], w_ref[:SB_WIDTH, :]) + _dot(mla_ref[...], w_ref[SB_WIDTH:, :])
    o_ref[...] = _layer_norm(ALPHA * x_ref[...] + m, g_ref[...], b_ref[...])


def _post_even(x, sb, mla, w, g, b):
    n = x.shape[0]
    row = lambda wd: pl.BlockSpec((ROW_TILE, wd), lambda i: (i, 0))
    return pl.pallas_call(
        _post_even_kernel,
        out_shape=jax.ShapeDtypeStruct((n, D_MODEL), F32),
        grid=(n // ROW_TILE,),
        in_specs=[row(D_MODEL), row(SB_WIDTH), row(MLA_WIDTH), _resident(w.shape),
                  _resident(g.shape), _resident(b.shape)],
        out_specs=row(D_MODEL),
        compiler_params=_params("parallel"),
        name="post_even",
    )(x, sb, mla, w, g, b)


HALO = 8


def _conv_mixer_kernel(tiles_per_seq, x_ref, win_ref, cw_ref, wout_ref, g_ref, b_ref, o_ref, tail_ref):
    i = pl.program_id(0)
    x = x_ref[...]
    xb = x.astype(BF16)
    gate_b = _dot(xb, win_ref[:, :D_MODEL])
    u = _dot(xb, win_ref[:, D_MODEL:2 * D_MODEL]) * _dot(xb, win_ref[:, 2 * D_MODEL:])

    @pl.when(i % tiles_per_seq == 0)
    def _():
        tail_ref[...] = jnp.zeros_like(tail_ref)

    prev = tail_ref[...]
    tail_ref[...] = u[ROW_TILE - HALO:, :]
    ext = jnp.concatenate([prev, u], axis=0)
    cw = cw_ref[...]
    conv = (cw[0:1, :] * ext[HALO - 2:HALO - 2 + ROW_TILE, :]
            + cw[1:2, :] * ext[HALO - 1:HALO - 1 + ROW_TILE, :]
            + cw[2:3, :] * u)
    m = _dot((gate_b * conv).astype(BF16), wout_ref[...])
    o_ref[...] = _layer_norm(ALPHA * x + m, g_ref[...], b_ref[...])


def _conv_mixer(x, w_in, conv_w, w_out, g, b, seq):
    n = x.shape[0]
    row = pl.BlockSpec((ROW_TILE, D_MODEL), lambda i: (i, 0))
    return pl.pallas_call(
        functools.partial(_conv_mixer_kernel, seq // ROW_TILE),
        out_shape=jax.ShapeDtypeStruct((n, D_MODEL), F32),
        grid=(n // ROW_TILE,),
        in_specs=[row, _resident(w_in.shape), _resident(conv_w.shape), _resident(w_out.shape),
                  _resident(g.shape), _resident(b.shape)],
        out_specs=row,
        scratch_shapes=[pltpu.VMEM((HALO, D_MODEL), F32)],
        compiler_params=_params("arbitrary"),
        name="conv_mixer",
    )(x, w_in, conv_w, w_out, g, b)


def _rope_tables(seq):
    half = MLA_ROPE_DIM // 2
    inv_freq = ROPE_THETA ** (-jnp.arange(0, MLA_ROPE_DIM, 2, dtype=F32) / MLA_ROPE_DIM)
    ang = jnp.arange(seq, dtype=F32)[:, None] * inv_freq[None, :]
    cos, sin = jnp.cos(ang), jnp.sin(ang)
    ones = jnp.ones((seq, MLA_NOPE_DIM), F32)
    pad = jnp.zeros((seq, MLA_QK_PAD - MLA_NOPE_DIM - 2 * half), F32)
    cos_t = jnp.concatenate([ones, cos, cos, pad], axis=1)
    sin_t = jnp.concatenate([0.0 * ones, -sin, sin, pad], axis=1)
    return cos_t, sin_t


def _even_weights(w_in, w_uq, w_ukv):
    half = MLA_ROPE_DIM // 2
    qk = MLA_NOPE_DIM + MLA_ROPE_DIM
    zeros = lambda r, c: jnp.zeros((r, c), F32)
    sb_v = slice(_X_QK, _X_QK + SB_WIDTH)
    lat = slice(_X_QK + SB_WIDTH, _X_QK + SB_WIDTH + MLA_Q_RANK + MLA_KV_RANK)
    w_kr = w_in[:, lat.stop:]
    k1, k2 = w_kr[:, :half], w_kr[:, half:]
    nope_pad = zeros(D_MODEL, MLA_NOPE_DIM)
    tail_pad = zeros(D_MODEL, MLA_QK_PAD - qk)
    wx = jnp.concatenate([w_in[:, :_X_QK], w_in[:, lat], nope_pad, k1, k2, tail_pad,
                          nope_pad, k2, k1, tail_pad], axis=1)
    wsv = w_in[:, sb_v].T
    wq = w_uq.reshape(MLA_Q_RANK, MLA_HEADS, qk)
    q_nope, q1, q2 = wq[..., :MLA_NOPE_DIM], wq[..., MLA_NOPE_DIM:MLA_NOPE_DIM + half], wq[..., MLA_NOPE_DIM + half:]
    zq = lambda c: jnp.zeros((MLA_Q_RANK, MLA_HEADS, c), F32)
    wqa = jnp.concatenate([q_nope, q1, q2, zq(MLA_QK_PAD - qk)], axis=-1).reshape(MLA_Q_RANK, _QK_WIDTH)
    wqb = jnp.concatenate([zq(MLA_NOPE_DIM), q2, q1, zq(MLA_QK_PAD - qk)], axis=-1).reshape(MLA_Q_RANK, _QK_WIDTH)
    wkv = w_ukv.reshape(MLA_KV_RANK, MLA_HEADS, MLA_NOPE_DIM + MLA_V_DIM)
    wk = jnp.concatenate([wkv[..., :MLA_NOPE_DIM],
                          jnp.zeros((MLA_KV_RANK, MLA_HEADS, MLA_QK_PAD - MLA_NOPE_DIM), F32)],
                         axis=-1).reshape(MLA_KV_RANK, _QK_WIDTH)
    wv = jnp.concatenate([wkv[..., MLA_NOPE_DIM:],
                          jnp.zeros((MLA_KV_RANK, MLA_HEADS, _VT_PAD - MLA_V_DIM), F32)],
                         axis=-1).reshape(MLA_KV_RANK, MLA_HEADS * _VT_PAD).T
    return [w.astype(BF16) for w in (wx, wsv, wqa, wqb, wk, wv)]


def kernel(x, ln_g, ln_b, ffn_w_gate, ffn_w_up, ffn_w_down, mix_w_in, mla_q_norm_g, mla_w_uq,
           mla_kv_norm_g, mla_w_ukv, mix_w_out, conv_w_in, conv_w, conv_w_out):
    batch, seq, _ = x.shape
    assert seq % ROW_TILE == 0 and seq % Q_TILE == 0 and _DIAG_BLOCKS == 2
    cos_t, sin_t = _rope_tables(seq)
    h = x.reshape(batch * seq, D_MODEL)
    vec = lambda a: a.reshape(1, -1)
    for layer in range(DEPTH):
        j = layer // 2
        ln = lambda s: (vec(ln_g[layer, s]), vec(ln_b[layer, s]))
        ffn_w = lambda s: (ffn_w_gate[layer, s].astype(BF16), ffn_w_up[layer, s].astype(BF16),
                           ffn_w_down[layer, s].astype(BF16))
        h = _ffn(h, *ffn_w(0), *ln(0))
        if layer % 2 == 0:
            wx, wsv, wqa, wqb, wk, wv = _even_weights(mix_w_in[j], mla_w_uq[j], mla_w_ukv[j])
            sb_qk, sb_vt, q, k, v_t = _proj_even(h, wx, wsv, vec(mla_q_norm_g[j]), wqa, wqb,
                                                 vec(mla_kv_norm_g[j]), wk, wv, cos_t, sin_t, seq)
            out_sb = _sb_attn(sb_qk, sb_vt, batch, seq)
            out_mla = _mla_attn(q, k, v_t, batch, seq)
            h = _post_even(h, out_sb, out_mla, mix_w_out[j].astype(BF16), *ln(1))
        else:
            h = _conv_mixer(h, conv_w_in[j].astype(BF16), conv_w[j], conv_w_out[j].astype(BF16), *ln(1), seq)
        h = _ffn(h, *ffn_w(1), *ln(2))
    return h.reshape(batch, seq, D_MODEL)
```

```python
import functools
from typing import NamedTuple

import jax
import jax.numpy as jnp
from jax import lax
from jax.experimental import pallas as pl
from jax.experimental.pallas import tpu as pltpu

D_MODEL = 1024
DEPTH = 4
SB_HEADS = 8
SB_HEAD_DIM = 64
MLA_HEADS = 8
MLA_NOPE_DIM = 64
MLA_ROPE_DIM = 32
MLA_V_DIM = 64
MLA_Q_RANK = 768
MLA_KV_RANK = 256
ROPE_THETA = 10000.0
CONV_K = 3
D_FF = 2816
LN_EPS = 1e-5
RMS_EPS = 1e-6
ALPHA = (2 * DEPTH) ** 0.25
SB_WIDTH = SB_HEADS * SB_HEAD_DIM
MLA_WIDTH = MLA_HEADS * MLA_V_DIM

LANES = 128
HEAD_PAIRS = SB_HEADS // 2
MLA_QK_PAD = LANES
_VT_PAD = LANES
FF_CHUNK = 256
ROW_TILE = 512
FFN_ROW_TILE = 1024


class _Tiling(NamedTuple):
    q: int
    k: int

    @property
    def diag(self):
        return self.q // self.k


SB_TILES = _Tiling(q=512, k=256)
MLA_TILES = _Tiling(q=512, k=256)
VMEM_LIMIT = 56 * 1024 * 1024
PAIRS_PER_TRIP = 2
SB_ISSUE_ORDER = (0, 1, 2)
MLA_ISSUE_ORDER = (0, 1)

BF16 = jnp.bfloat16
F32 = jnp.float32
NEG_BIG = -1e30
LOG2E = 1.4426950408889634


def _dot(a, b):
    return jnp.dot(a, b, preferred_element_type=F32)


def _dot_nt(a, b):
    return lax.dot_general(a, b, (((1,), (1,)), ((), ())), preferred_element_type=F32)


def _layer_norm(y, g, b):
    mu = jnp.mean(y, axis=-1, keepdims=True)
    d = y - mu
    var = jnp.mean(d * d, axis=-1, keepdims=True)
    return d * lax.rsqrt(var + LN_EPS) * g + b


def _rms_norm(c, g):
    return c * lax.rsqrt(jnp.mean(c * c, axis=-1, keepdims=True) + RMS_EPS) * g


def _params(*sem):
    return pltpu.CompilerParams(dimension_semantics=sem, vmem_limit_bytes=VMEM_LIMIT)


def _resident(shape):
    nd = len(shape)
    return pl.BlockSpec(shape, lambda *_: (0,) * nd, pipeline_mode=pl.Buffered(1))


def _ffn_kernel(x_ref, wg_ref, wu_ref, wd_ref, g_ref, b_ref, o_ref, h_ref):
    x = x_ref[...]
    xb = x.astype(BF16)
    for c in range(D_FF // FF_CHUNK):
        sl = slice(c * FF_CHUNK, (c + 1) * FF_CHUNK)
        gate = _dot(xb, wg_ref[:, sl])
        up = _dot(xb, wu_ref[:, sl])
        h_ref[:, sl] = (gate * (1.0 / (1.0 + jnp.exp(-gate))) * up).astype(BF16)
    f = _dot(h_ref[...], wd_ref[...])
    o_ref[...] = _layer_norm(ALPHA * x + 0.5 * f, g_ref[...], b_ref[...])


def _ffn(x, wg, wu, wd, g, b):
    n = x.shape[0]
    row = pl.BlockSpec((FFN_ROW_TILE, D_MODEL), lambda i: (i, 0))
    return pl.pallas_call(
        _ffn_kernel,
        out_shape=jax.ShapeDtypeStruct((n, D_MODEL), F32),
        grid=(n // FFN_ROW_TILE,),
        in_specs=[row, _resident(wg.shape), _resident(wu.shape), _resident(wd.shape),
                  _resident(g.shape), _resident(b.shape)],
        out_specs=row,
        scratch_shapes=[pltpu.VMEM((FFN_ROW_TILE, D_FF), BF16)],
        compiler_params=_params("parallel"),
        name="ffn",
    )(x, wg, wu, wd, g, b)


_X_QK = 2 * SB_WIDTH
_X_CQ = _X_QK + MLA_Q_RANK
_X_CKV = _X_CQ + MLA_KV_RANK
_X_KRA = _X_CKV + LANES
_X_KRB = _X_KRA + LANES
_QK_WIDTH = MLA_HEADS * MLA_QK_PAD


def _proj_even_kernel(x_ref, wx_ref, wsv_ref, gq_ref, wqa_ref, wqb_ref, gkv_ref, wk_ref, wv_ref,
                      cos_ref, sin_ref, sb_ref, sbv_ref, q_ref, k_ref, v_ref):
    xb = x_ref[...].astype(BF16)
    cos = cos_ref[...]
    sin = sin_ref[...]
    sb_ref[:, :SB_WIDTH] = (_dot(xb, wx_ref[:, :SB_WIDTH]) * (SB_HEAD_DIM ** -0.5)).astype(BF16)
    sb_ref[:, SB_WIDTH:] = _dot(xb, wx_ref[:, SB_WIDTH:_X_QK]).astype(BF16)
    sbv_ref[...] = _dot_nt(wsv_ref[...], xb).astype(BF16)
    cq = _rms_norm(_dot(xb, wx_ref[:, _X_QK:_X_CQ]), gq_ref[...]).astype(BF16)
    for h in range(MLA_HEADS):
        sl = slice(h * MLA_QK_PAD, (h + 1) * MLA_QK_PAD)
        q_ref[:, sl] = (_dot(cq, wqa_ref[:, sl]) * cos + _dot(cq, wqb_ref[:, sl]) * sin).astype(BF16)
    ckv = _rms_norm(_dot(xb, wx_ref[:, _X_CQ:_X_CKV]), gkv_ref[...]).astype(BF16)
    kr = _dot(xb, wx_ref[:, _X_CKV:_X_KRA]) * cos + _dot(xb, wx_ref[:, _X_KRA:_X_KRB]) * sin
    for h in range(MLA_HEADS):
        sl = slice(h * MLA_QK_PAD, (h + 1) * MLA_QK_PAD)
        k_ref[:, sl] = (_dot(ckv, wk_ref[:, sl]) + kr).astype(BF16)
    vt = _dot_nt(wv_ref[...], ckv)
    ones_row = lax.broadcasted_iota(jnp.int32, vt.shape, 0) % _VT_PAD == MLA_V_DIM
    v_ref[...] = jnp.where(ones_row, 1.0, vt).astype(BF16)


def _proj_even(x, wx, wsv, gq, wqa, wqb, gkv, wk, wv, cos_t, sin_t, seq):
    n = x.shape[0]
    tiles_per_seq = seq // ROW_TILE
    row = lambda w: pl.BlockSpec((ROW_TILE, w), lambda i: (i, 0))
    col = lambda h: pl.BlockSpec((h, ROW_TILE), lambda i: (0, i))
    pos = pl.BlockSpec((ROW_TILE, LANES), lambda i: (i % tiles_per_seq, 0))
    return pl.pallas_call(
        _proj_even_kernel,
        out_shape=(jax.ShapeDtypeStruct((n, _X_QK), BF16),
                   jax.ShapeDtypeStruct((SB_WIDTH, n), BF16),
                   jax.ShapeDtypeStruct((n, _QK_WIDTH), BF16),
                   jax.ShapeDtypeStruct((n, _QK_WIDTH), BF16),
                   jax.ShapeDtypeStruct((MLA_HEADS * _VT_PAD, n), BF16)),
        grid=(n // ROW_TILE,),
        in_specs=[row(D_MODEL), _resident(wx.shape), _resident(wsv.shape), _resident(gq.shape),
                  _resident(wqa.shape), _resident(wqb.shape), _resident(gkv.shape),
                  _resident(wk.shape), _resident(wv.shape), pos, pos],
        out_specs=(row(_X_QK), col(SB_WIDTH), row(_QK_WIDTH), row(_QK_WIDTH), col(MLA_HEADS * _VT_PAD)),
        compiler_params=_params("parallel"),
        name="proj_even",
    )(x, wx, wsv, gq, wqa, wqb, gkv, wk, wv, cos_t, sin_t)


def _sweep_keys(i, stages, issue_order, tiles):
    depth = len(stages)
    blocks = (i + 1) * tiles.diag
    lead = [depth - 1 - s for s in range(depth)]
    peel = max(0, tiles.diag - depth + 1)
    for n in range(1 - depth, peel):
        for s, stage in enumerate(stages):
            b = n + lead[s]
            if b >= 0:
                stage(blocks - 1 - b, b % 2, (tiles.diag - 1 - b) * tiles.k if b < tiles.diag else None)

    def step_pair(t):
        for u in range(2):
            for s in issue_order:
                b_off = peel + u + lead[s]
                stages[s](blocks - 1 - b_off - 2 * t, b_off % 2, None)

    def body(t, carry):
        for r in range(PAIRS_PER_TRIP):
            step_pair(PAIRS_PER_TRIP * t + r)
        return carry

    pairs = i * (tiles.diag // 2)
    lax.fori_loop(0, pairs // PAIRS_PER_TRIP, body, 0)
    if (tiles.diag // 2) % PAIRS_PER_TRIP:
        for r in range(PAIRS_PER_TRIP - 1):
            pl.when(r < pairs % PAIRS_PER_TRIP)(
                functools.partial(step_pair, pairs - pairs % PAIRS_PER_TRIP + r))
    for e in range(depth - 1):
        for s, stage in enumerate(stages):
            j = depth - 2 - e - lead[s]
            if j >= 0:
                stage(j, (j + 1) % 2, None)


def _hidden(tiles, i, j, q_lo, visible_when_equal):
    shape = (tiles.k, tiles.q - q_lo)
    key = j * tiles.k + lax.broadcasted_iota(jnp.int32, shape, 0)
    qry = i * tiles.q + q_lo + lax.broadcasted_iota(jnp.int32, shape, 1)
    return key > qry if visible_when_equal else key >= qry


def _key_block(tiles, j):
    return pl.ds(pl.multiple_of(j * tiles.k, tiles.k), tiles.k)


def _merge_heads(o0, o1):
    return jnp.concatenate([o0, o1], axis=0).T


class _Slots:
    def __init__(self, first, second):
        self._refs = (first, second)

    def __getitem__(self, idx):
        return self._refs[idx[0]][idx[1:]]

    def __setitem__(self, idx, value):
        self._refs[idx[0]][idx[1:]] = value


def _slot_pair(shape, dtype):
    return [pltpu.VMEM(shape, dtype)] * 2


def _tile_pair(tiles, dtype):
    return _slot_pair((2, tiles.k, tiles.q), dtype)


def _attn_call(body, name, tiles, q, q_spec, k, k_spec, v_t, vt_rows, batch, seq, width, state_shapes):
    n = q.shape[0]
    nq = seq // tiles.q
    return pl.pallas_call(
        body,
        out_shape=jax.ShapeDtypeStruct((n, width), BF16),
        grid=(batch, HEAD_PAIRS, nq),
        in_specs=[q_spec, k_spec, pl.BlockSpec((vt_rows, seq), lambda b, p, i: (p, b))],
        out_specs=pl.BlockSpec((tiles.q, LANES), lambda b, p, i: (b * nq + i, p)),
        scratch_shapes=_tile_pair(tiles, F32) + state_shapes,
        compiler_params=_params("parallel", "parallel", "parallel"),
        name=name,
    )(q, k, v_t)


def _sb_attn_kernel(tiles, q_ref, k_ref, vt_ref, o_ref, z0, z1, e0, e1, total0, total1, acc_ref, later_ref):
    z_ref, e_ref, total_ref = _Slots(z0, z1), _Slots(e0, e1), _Slots(total0, total1)
    i = pl.program_id(2)
    q = q_ref[...]
    lane = lax.broadcasted_iota(jnp.int32, (tiles.q, LANES), 1)
    qh = [jnp.where(lane < SB_HEAD_DIM, q, jnp.zeros_like(q)),
          jnp.where(lane >= SB_HEAD_DIM, q, jnp.zeros_like(q))]
    r = lax.broadcasted_iota(jnp.int32, (tiles.k, tiles.k), 0)
    c = lax.broadcasted_iota(jnp.int32, (tiles.k, tiles.k), 1)
    tri = jnp.where(c > r, 1.0, 0.0).astype(BF16)
    acc_ref[...] = jnp.zeros_like(acc_ref)
    later_ref[...] = jnp.zeros_like(later_ref)

    def scores(j, slot, q_lo):
        k = k_ref[_key_block(tiles, j), :]
        lo = q_lo or 0
        for h in range(2):
            z = _dot_nt(k, qh[h][lo:, :])
            if q_lo is not None:
                z = jnp.where(_hidden(tiles, i, j, lo, visible_when_equal=False), NEG_BIG, z)
                if lo:
                    z_ref[slot, h, :, :lo] = jnp.full((tiles.k, lo), NEG_BIG, F32)
            z_ref[slot, h, :, lo:] = z

    def within_block(j, slot, q_lo):
        lo = q_lo or 0
        for h in range(2):
            z = z_ref[slot, h, :, lo:]
            neg_abs = lax.bitcast_convert_type(
                lax.bitcast_convert_type(z, jnp.uint32) | jnp.uint32(0x80000000), F32)
            softplus = jnp.maximum(z, 0.0) + jnp.log(1.0 + jnp.exp(neg_abs))
            later = _dot(tri, softplus.astype(BF16))
            if lo:
                e_ref[slot, h, :, :lo] = jnp.full((tiles.k, lo), NEG_BIG, F32)
                total_ref[slot, h, :, :lo] = jnp.zeros((1, lo), F32)
            e_ref[slot, h, :, lo:] = (z - softplus) - later
            total_ref[slot, h, :, lo:] = later[0:1, :] + softplus[0:1, :]

    def accumulate(j, slot, q_lo):
        vt = vt_ref[:, _key_block(tiles, j)]
        lo = q_lo or 0
        for h in range(2):
            w = jnp.exp(e_ref[slot, h, :, lo:])
            acc_ref[h, :, lo:] += _dot(vt, w.astype(BF16)) * jnp.exp(-later_ref[h, :, lo:])
            later_ref[h, :, lo:] += total_ref[slot, h, :, lo:]

    _sweep_keys(i, [scores, within_block, accumulate], SB_ISSUE_ORDER, tiles)
    o_ref[...] = _merge_heads(acc_ref[0, :SB_HEAD_DIM, :], acc_ref[1, SB_HEAD_DIM:, :]).astype(o_ref.dtype)


def _sb_attn(qk, v_t, batch, seq):
    tiles = SB_TILES
    nq = seq // tiles.q
    return _attn_call(
        functools.partial(_sb_attn_kernel, tiles), "sb_attn", tiles,
        qk, pl.BlockSpec((tiles.q, LANES), lambda b, p, i: (b * nq + i, p)),
        qk, pl.BlockSpec((seq, LANES), lambda b, p, i: (b, HEAD_PAIRS + p)),
        v_t, LANES, batch, seq, SB_WIDTH,
        _tile_pair(tiles, F32) + _slot_pair((2, 1, tiles.q), F32)
        + [pltpu.VMEM((2, LANES, tiles.q), F32), pltpu.VMEM((2, 1, tiles.q), F32)])


def _mla_attn_kernel(tiles, q_ref, k_ref, vt_ref, o_ref, s0, s1, acc_ref, m_ref):
    s_ref = _Slots(s0, s1)
    i = pl.program_id(2)
    c2 = (MLA_NOPE_DIM + MLA_ROPE_DIM) ** -0.5 * LOG2E
    qh = [q_ref[:, h * MLA_QK_PAD:(h + 1) * MLA_QK_PAD] for h in range(2)]
    acc_ref[...] = jnp.zeros_like(acc_ref)
    m_ref[...] = jnp.full_like(m_ref, NEG_BIG)

    def scores(j, slot, q_lo):
        lo = q_lo or 0
        for h in range(2):
            s = _dot_nt(k_ref[_key_block(tiles, j), h * MLA_QK_PAD:(h + 1) * MLA_QK_PAD], qh[h][lo:, :])
            if q_lo is not None:
                s = jnp.where(_hidden(tiles, i, j, lo, visible_when_equal=True), NEG_BIG, s)
                if lo:
                    s_ref[slot, h, :, :lo] = jnp.full((tiles.k, lo), NEG_BIG, F32)
            s_ref[slot, h, :, lo:] = s

    def accumulate(j, slot, q_lo):
        lo = q_lo or 0
        for h in range(2):
            s = s_ref[slot, h, :, lo:]
            m = m_ref[h, :, lo:]
            m_new = jnp.maximum(m, jnp.max(s, axis=0, keepdims=True))
            p = jnp.exp2((s - m_new) * c2)
            rescale = jnp.exp2((m - m_new) * c2)
            m_ref[h, :, lo:] = m_new
            vt = vt_ref[h * _VT_PAD:(h + 1) * _VT_PAD, _key_block(tiles, j)]
            acc_ref[h, :, lo:] = rescale * acc_ref[h, :, lo:] + _dot(vt, p.astype(BF16))

    _sweep_keys(i, [scores, accumulate], MLA_ISSUE_ORDER, tiles)
    out = [acc_ref[h, :MLA_V_DIM, :] / acc_ref[h, MLA_V_DIM:MLA_V_DIM + 1, :] for h in range(2)]
    o_ref[...] = _merge_heads(*out).astype(o_ref.dtype)


def _mla_attn(q, k, v_t, batch, seq):
    tiles = MLA_TILES
    nq = seq // tiles.q
    return _attn_call(
        functools.partial(_mla_attn_kernel, tiles), "mla_attn", tiles,
        q, pl.BlockSpec((tiles.q, 2 * MLA_QK_PAD), lambda b, p, i: (b * nq + i, p)),
        k, pl.BlockSpec((seq, 2 * MLA_QK_PAD), lambda b, p, i: (b, p)),
        v_t, 2 * _VT_PAD, batch, seq, MLA_WIDTH,
        [pltpu.VMEM((2, _VT_PAD, tiles.q), F32), pltpu.VMEM((2, 1, tiles.q), F32)])


def _post_even_kernel(x_ref, sb_ref, mla_ref, w_ref, g_ref, b_ref, o_ref):
    m = _dot(sb_ref[...], w_ref[:SB_WIDTH, :]) + _dot(mla_ref[...], w_ref[SB_WIDTH:, :])
    o_ref[...] = _layer_norm(ALPHA * x_ref[...] + m, g_ref[...], b_ref[...])


def _post_even(x, sb, mla, w, g, b):
    n = x.shape[0]
    row = lambda wd: pl.BlockSpec((ROW_TILE, wd), lambda i: (i, 0))
    return pl.pallas_call(
        _post_even_kernel,
        out_shape=jax.ShapeDtypeStruct((n, D_MODEL), F32),
        grid=(n // ROW_TILE,),
        in_specs=[row(D_MODEL), row(SB_WIDTH), row(MLA_WIDTH), _resident(w.shape),
                  _resident(g.shape), _resident(b.shape)],
        out_specs=row(D_MODEL),
        compiler_params=_params("parallel"),
        name="post_even",
    )(x, sb, mla, w, g, b)


HALO = 8


def _conv_mixer_kernel(tiles_per_seq, x_ref, win_ref, cw_ref, wout_ref, g_ref, b_ref, o_ref, tail_ref):
    i = pl.program_id(0)
    x = x_ref[...]
    xb = x.astype(BF16)
    gate_b = _dot(xb, win_ref[:, :D_MODEL])
    u = _dot(xb, win_ref[:, D_MODEL:2 * D_MODEL]) * _dot(xb, win_ref[:, 2 * D_MODEL:])

    @pl.when(i % tiles_per_seq == 0)
    def _():
        tail_ref[...] = jnp.zeros_like(tail_ref)

    prev = tail_ref[...]
    tail_ref[...] = u[ROW_TILE - HALO:, :]
    ext = jnp.concatenate([prev, u], axis=0)
    cw = cw_ref[...]
    conv = (cw[0:1, :] * ext[HALO - 2:HALO - 2 + ROW_TILE, :]
            + cw[1:2, :] * ext[HALO - 1:HALO - 1 + ROW_TILE, :]
            + cw[2:3, :] * u)
    m = _dot((gate_b * conv).astype(BF16), wout_ref[...])
    o_ref[...] = _layer_norm(ALPHA * x + m, g_ref[...], b_ref[...])


def _conv_mixer(x, w_in, conv_w, w_out, g, b, seq):
    n = x.shape[0]
    row = pl.BlockSpec((ROW_TILE, D_MODEL), lambda i: (i, 0))
    return pl.pallas_call(
        functools.partial(_conv_mixer_kernel, seq // ROW_TILE),
        out_shape=jax.ShapeDtypeStruct((n, D_MODEL), F32),
        grid=(n // ROW_TILE,),
        in_specs=[row, _resident(w_in.shape), _resident(conv_w.shape), _resident(w_out.shape),
                  _resident(g.shape), _resident(b.shape)],
        out_specs=row,
        scratch_shapes=[pltpu.VMEM((HALO, D_MODEL), F32)],
        compiler_params=_params("arbitrary"),
        name="conv_mixer",
    )(x, w_in, conv_w, w_out, g, b)


def _rope_tables(seq):
    half = MLA_ROPE_DIM // 2
    inv_freq = ROPE_THETA ** (-jnp.arange(0, MLA_ROPE_DIM, 2, dtype=F32) / MLA_ROPE_DIM)
    ang = jnp.arange(seq, dtype=F32)[:, None] * inv_freq[None, :]
    cos, sin = jnp.cos(ang), jnp.sin(ang)
    ones = jnp.ones((seq, MLA_NOPE_DIM), F32)
    pad = jnp.zeros((seq, MLA_QK_PAD - MLA_NOPE_DIM - 2 * half), F32)
    cos_t = jnp.concatenate([ones, cos, cos, pad], axis=1)
    sin_t = jnp.concatenate([0.0 * ones, -sin, sin, pad], axis=1)
    return cos_t, sin_t


def _even_weights(w_in, w_uq, w_ukv):
    half = MLA_ROPE_DIM // 2
    qk = MLA_NOPE_DIM + MLA_ROPE_DIM
    zeros = lambda r, c: jnp.zeros((r, c), F32)
    sb_v = slice(_X_QK, _X_QK + SB_WIDTH)
    lat = slice(_X_QK + SB_WIDTH, _X_QK + SB_WIDTH + MLA_Q_RANK + MLA_KV_RANK)
    w_kr = w_in[:, lat.stop:]
    k1, k2 = w_kr[:, :half], w_kr[:, half:]
    nope_pad = zeros(D_MODEL, MLA_NOPE_DIM)
    tail_pad = zeros(D_MODEL, MLA_QK_PAD - qk)
    wx = jnp.concatenate([w_in[:, :_X_QK], w_in[:, lat], nope_pad, k1, k2, tail_pad,
                          nope_pad, k2, k1, tail_pad], axis=1)
    wsv = w_in[:, sb_v].T
    wq = w_uq.reshape(MLA_Q_RANK, MLA_HEADS, qk)
    q_nope, q1, q2 = wq[..., :MLA_NOPE_DIM], wq[..., MLA_NOPE_DIM:MLA_NOPE_DIM + half], wq[..., MLA_NOPE_DIM + half:]
    zq = lambda c: jnp.zeros((MLA_Q_RANK, MLA_HEADS, c), F32)
    wqa = jnp.concatenate([q_nope, q1, q2, zq(MLA_QK_PAD - qk)], axis=-1).reshape(MLA_Q_RANK, _QK_WIDTH)
    wqb = jnp.concatenate([zq(MLA_NOPE_DIM), q2, q1, zq(MLA_QK_PAD - qk)], axis=-1).reshape(MLA_Q_RANK, _QK_WIDTH)
    wkv = w_ukv.reshape(MLA_KV_RANK, MLA_HEADS, MLA_NOPE_DIM + MLA_V_DIM)
    wk = jnp.concatenate([wkv[..., :MLA_NOPE_DIM],
                          jnp.zeros((MLA_KV_RANK, MLA_HEADS, MLA_QK_PAD - MLA_NOPE_DIM), F32)],
                         axis=-1).reshape(MLA_KV_RANK, _QK_WIDTH)
    wv = jnp.concatenate([wkv[..., MLA_NOPE_DIM:],
                          jnp.zeros((MLA_KV_RANK, MLA_HEADS, _VT_PAD - MLA_V_DIM), F32)],
                         axis=-1).reshape(MLA_KV_RANK, MLA_HEADS * _VT_PAD).T
    return [w.astype(BF16) for w in (wx, wsv, wqa, wqb, wk, wv)]


def kernel(x, ln_g, ln_b, ffn_w_gate, ffn_w_up, ffn_w_down, mix_w_in, mla_q_norm_g, mla_w_uq,
           mla_kv_norm_g, mla_w_ukv, mix_w_out, conv_w_in, conv_w, conv_w_out):
    batch, seq, _ = x.shape
    assert seq % FFN_ROW_TILE == 0 and seq % ROW_TILE == 0
    for tiles in (SB_TILES, MLA_TILES):
        assert seq % tiles.q == 0 and tiles.q % (2 * tiles.k) == 0
    cos_t, sin_t = _rope_tables(seq)
    h = x.reshape(batch * seq, D_MODEL)
    vec = lambda a: a.reshape(1, -1)
    for layer in range(DEPTH):
        j = layer // 2
        ln = lambda s: (vec(ln_g[layer, s]), vec(ln_b[layer, s]))
        ffn_w = lambda s: (ffn_w_gate[layer, s].astype(BF16), ffn_w_up[layer, s].astype(BF16),
                           ffn_w_down[layer, s].astype(BF16))
        h = _ffn(h, *ffn_w(0), *ln(0))
        if layer % 2 == 0:
            wx, wsv, wqa, wqb, wk, wv = _even_weights(mix_w_in[j], mla_w_uq[j], mla_w_ukv[j])
            sb_qk, sb_vt, q, k, v_t = _proj_even(h, wx, wsv, vec(mla_q_norm_g[j]), wqa, wqb,
                                                 vec(mla_kv_norm_g[j]), wk, wv, cos_t, sin_t, seq)
            out_sb = _sb_attn(sb_qk, sb_vt, batch, seq)
            out_mla = _mla_attn(q, k, v_t, batch, seq)
            h = _post_even(h, out_sb, out_mla, mix_w_out[j].astype(BF16), *ln(1))
        else:
            h = _conv_mixer(h, conv_w_in[j].astype(BF16), conv_w[j], conv_w_out[j].astype(BF16), *ln(1), seq)
        h = _ffn(h, *ffn_w(1), *ln(2))
    return h.reshape(batch, seq, D_MODEL)
```

```python
import functools
from typing import NamedTuple

import jax
import jax.numpy as jnp
from jax import lax
from jax.experimental import pallas as pl
from jax.experimental.pallas import tpu as pltpu

D_MODEL = 1024
DEPTH = 4
SB_HEADS = 8
SB_HEAD_DIM = 64
MLA_HEADS = 8
MLA_NOPE_DIM = 64
MLA_ROPE_DIM = 32
MLA_V_DIM = 64
MLA_Q_RANK = 768
MLA_KV_RANK = 256
ROPE_THETA = 10000.0
CONV_K = 3
D_FF = 2816
LN_EPS = 1e-5
RMS_EPS = 1e-6
ALPHA = (2 * DEPTH) ** 0.25
SB_WIDTH = SB_HEADS * SB_HEAD_DIM
MLA_WIDTH = MLA_HEADS * MLA_V_DIM

LANES = 128
HEAD_PAIRS = SB_HEADS // 2
MLA_QK_PAD = LANES
_VT_PAD = LANES
FF_CHUNK = 256
ROW_TILE = 512
FFN_ROW_TILE = 1024


class _Tiling(NamedTuple):
    q: int
    k: int

    @property
    def diag(self):
        return self.q // self.k


ATTN_TILES = _Tiling(q=512, k=256)
VMEM_LIMIT = 56 * 1024 * 1024
PAIRS_PER_TRIP = 2

BF16 = jnp.bfloat16
F32 = jnp.float32
NEG_BIG = -1e30
LOG2E = 1.4426950408889634


def _dot(a, b):
    return jnp.dot(a, b, preferred_element_type=F32)


def _dot_nt(a, b):
    return lax.dot_general(a, b, (((1,), (1,)), ((), ())), preferred_element_type=F32)


def _layer_norm(y, g, b):
    mu = jnp.mean(y, axis=-1, keepdims=True)
    d = y - mu
    var = jnp.mean(d * d, axis=-1, keepdims=True)
    return d * lax.rsqrt(var + LN_EPS) * g + b


def _rms_norm(c, g):
    return c * lax.rsqrt(jnp.mean(c * c, axis=-1, keepdims=True) + RMS_EPS) * g


def _params(*sem):
    return pltpu.CompilerParams(dimension_semantics=sem, vmem_limit_bytes=VMEM_LIMIT)


def _resident(shape):
    nd = len(shape)
    return pl.BlockSpec(shape, lambda *_: (0,) * nd, pipeline_mode=pl.Buffered(1))


def _half_step_ffn(x, wg_ref, wu_ref, wd_ref, g_ref, b_ref, h_ref):
    xb = x.astype(BF16)
    for c in range(D_FF // FF_CHUNK):
        sl = slice(c * FF_CHUNK, (c + 1) * FF_CHUNK)
        gate = _dot(xb, wg_ref[:, sl])
        up = _dot(xb, wu_ref[:, sl])
        h_ref[:, sl] = (gate * (1.0 / (1.0 + jnp.exp(-gate))) * up).astype(BF16)
    f = _dot(h_ref[...], wd_ref[...])
    return _layer_norm(ALPHA * x + 0.5 * f, g_ref[...], b_ref[...])


def _ffn_kernel(x_ref, wg_ref, wu_ref, wd_ref, g_ref, b_ref, o_ref, h_ref):
    o_ref[...] = _half_step_ffn(x_ref[...], wg_ref, wu_ref, wd_ref, g_ref, b_ref, h_ref)


def _attn_out_ffn_kernel(x_ref, sb_ref, mla_ref, wo_ref, go_ref, bo_ref,
                         wg_ref, wu_ref, wd_ref, g_ref, b_ref, o_ref, h_ref):
    m = _dot(sb_ref[...], wo_ref[:SB_WIDTH, :]) + _dot(mla_ref[...], wo_ref[SB_WIDTH:, :])
    x = _layer_norm(ALPHA * x_ref[...] + m, go_ref[...], bo_ref[...])
    o_ref[...] = _half_step_ffn(x, wg_ref, wu_ref, wd_ref, g_ref, b_ref, h_ref)


def _ffn_call(body, name, rows, residents):
    n = rows[0][0].shape[0]
    row = lambda w: pl.BlockSpec((FFN_ROW_TILE, w), lambda i: (i, 0))
    return pl.pallas_call(
        body,
        out_shape=jax.ShapeDtypeStruct((n, D_MODEL), F32),
        grid=(n // FFN_ROW_TILE,),
        in_specs=[row(w) for _, w in rows] + [_resident(a.shape) for a in residents],
        out_specs=row(D_MODEL),
        scratch_shapes=[pltpu.VMEM((FFN_ROW_TILE, D_FF), BF16)],
        compiler_params=_params("parallel"),
        name=name,
    )(*[a for a, _ in rows], *residents)


def _ffn(x, wg, wu, wd, g, b):
    return _ffn_call(_ffn_kernel, "ffn", [(x, D_MODEL)], [wg, wu, wd, g, b])


def _attn_out_ffn(x, sb, mla, wo, go, bo, wg, wu, wd, g, b):
    return _ffn_call(_attn_out_ffn_kernel, "attn_out_ffn",
                     [(x, D_MODEL), (sb, SB_WIDTH), (mla, MLA_WIDTH)], [wo, go, bo, wg, wu, wd, g, b])


_X_QK = 2 * SB_WIDTH
_X_CQ = _X_QK + MLA_Q_RANK
_X_CKV = _X_CQ + MLA_KV_RANK
_X_KRA = _X_CKV + LANES
_X_KRB = _X_KRA + LANES
_QK_WIDTH = MLA_HEADS * MLA_QK_PAD


def _proj_even_kernel(x_ref, wx_ref, wsv_ref, gq_ref, wqa_ref, wqb_ref, gkv_ref, wk_ref, wv_ref,
                      cos_ref, sin_ref, sb_ref, sbv_ref, q_ref, k_ref, v_ref):
    xb = x_ref[...].astype(BF16)
    cos = cos_ref[...]
    sin = sin_ref[...]
    sb_ref[:, :SB_WIDTH] = (_dot(xb, wx_ref[:, :SB_WIDTH]) * (SB_HEAD_DIM ** -0.5)).astype(BF16)
    sb_ref[:, SB_WIDTH:] = _dot(xb, wx_ref[:, SB_WIDTH:_X_QK]).astype(BF16)
    sbv_ref[...] = _dot_nt(wsv_ref[...], xb).astype(BF16)
    cq = _rms_norm(_dot(xb, wx_ref[:, _X_QK:_X_CQ]), gq_ref[...]).astype(BF16)
    for h in range(MLA_HEADS):
        sl = slice(h * MLA_QK_PAD, (h + 1) * MLA_QK_PAD)
        q_ref[:, sl] = (_dot(cq, wqa_ref[:, sl]) * cos + _dot(cq, wqb_ref[:, sl]) * sin).astype(BF16)
    ckv = _rms_norm(_dot(xb, wx_ref[:, _X_CQ:_X_CKV]), gkv_ref[...]).astype(BF16)
    kr = _dot(xb, wx_ref[:, _X_CKV:_X_KRA]) * cos + _dot(xb, wx_ref[:, _X_KRA:_X_KRB]) * sin
    for h in range(MLA_HEADS):
        sl = slice(h * MLA_QK_PAD, (h + 1) * MLA_QK_PAD)
        k_ref[:, sl] = (_dot(ckv, wk_ref[:, sl]) + kr).astype(BF16)
    vt = _dot_nt(wv_ref[...], ckv)
    ones_row = lax.broadcasted_iota(jnp.int32, vt.shape, 0) % _VT_PAD == MLA_V_DIM
    v_ref[...] = jnp.where(ones_row, 1.0, vt).astype(BF16)


def _proj_even(x, wx, wsv, gq, wqa, wqb, gkv, wk, wv, cos_t, sin_t, seq):
    n = x.shape[0]
    tiles_per_seq = seq // ROW_TILE
    row = lambda w: pl.BlockSpec((ROW_TILE, w), lambda i: (i, 0))
    col = lambda h: pl.BlockSpec((h, ROW_TILE), lambda i: (0, i))
    pos = pl.BlockSpec((ROW_TILE, LANES), lambda i: (i % tiles_per_seq, 0))
    return pl.pallas_call(
        _proj_even_kernel,
        out_shape=(jax.ShapeDtypeStruct((n, _X_QK), BF16),
                   jax.ShapeDtypeStruct((SB_WIDTH, n), BF16),
                   jax.ShapeDtypeStruct((n, _QK_WIDTH), BF16),
                   jax.ShapeDtypeStruct((n, _QK_WIDTH), BF16),
                   jax.ShapeDtypeStruct((MLA_HEADS * _VT_PAD, n), BF16)),
        grid=(n // ROW_TILE,),
        in_specs=[row(D_MODEL), _resident(wx.shape), _resident(wsv.shape), _resident(gq.shape),
                  _resident(wqa.shape), _resident(wqb.shape), _resident(gkv.shape),
                  _resident(wk.shape), _resident(wv.shape), pos, pos],
        out_specs=(row(_X_QK), col(SB_WIDTH), row(_QK_WIDTH), row(_QK_WIDTH), col(MLA_HEADS * _VT_PAD)),
        compiler_params=_params("parallel"),
        name="proj_even",
    )(x, wx, wsv, gq, wqa, wqb, gkv, wk, wv, cos_t, sin_t)


def _sweep_keys(i, stages, tiles):
    depth = len(stages)
    blocks = (i + 1) * tiles.diag
    lead = [depth - 1 - s for s in range(depth)]
    peel = max(0, tiles.diag - depth + 1)
    for n in range(1 - depth, peel):
        for s, stage in enumerate(stages):
            b = n + lead[s]
            if b >= 0:
                stage(blocks - 1 - b, b % 2, (tiles.diag - 1 - b) * tiles.k if b < tiles.diag else None)

    def step_pair(t):
        for u in range(2):
            for s in range(depth):
                b_off = peel + u + lead[s]
                stages[s](blocks - 1 - b_off - 2 * t, b_off % 2, None)

    def body(t, carry):
        for r in range(PAIRS_PER_TRIP):
            step_pair(PAIRS_PER_TRIP * t + r)
        return carry

    pairs = i * (tiles.diag // 2)
    lax.fori_loop(0, pairs // PAIRS_PER_TRIP, body, 0)
    if (tiles.diag // 2) % PAIRS_PER_TRIP:
        for r in range(PAIRS_PER_TRIP - 1):
            pl.when(r < pairs % PAIRS_PER_TRIP)(
                functools.partial(step_pair, pairs - pairs % PAIRS_PER_TRIP + r))
    for e in range(depth - 1):
        for s, stage in enumerate(stages):
            j = depth - 2 - e - lead[s]
            if j >= 0:
                stage(j, (j + 1) % 2, None)


def _hidden(tiles, i, j, q_lo, visible_when_equal):
    shape = (tiles.k, tiles.q - q_lo)
    key = j * tiles.k + lax.broadcasted_iota(jnp.int32, shape, 0)
    qry = i * tiles.q + q_lo + lax.broadcasted_iota(jnp.int32, shape, 1)
    return key > qry if visible_when_equal else key >= qry


def _key_block(tiles, j):
    return pl.ds(pl.multiple_of(j * tiles.k, tiles.k), tiles.k)


def _merge_heads(o0, o1):
    return jnp.concatenate([o0, o1], axis=0).T


class _Slots:
    def __init__(self, first, second):
        self._refs = (first, second)

    def __getitem__(self, idx):
        return self._refs[idx[0]][idx[1:]]

    def __setitem__(self, idx, value):
        self._refs[idx[0]][idx[1:]] = value


def _slot_pair(shape, dtype):
    return [pltpu.VMEM(shape, dtype)] * 2


def _tile_pair(tiles, dtype):
    return _slot_pair((2, tiles.k, tiles.q), dtype)


def _sb_stages(tiles, i, q_ref, k_ref, vt_ref, o_ref, z0, z1, e0, e1, total0, total1, acc_ref, later_ref):
    z_ref, e_ref, total_ref = _Slots(z0, z1), _Slots(e0, e1), _Slots(total0, total1)
    q = q_ref[...]
    lane = lax.broadcasted_iota(jnp.int32, (tiles.q, LANES), 1)
    qh = [jnp.where(lane < SB_HEAD_DIM, q, jnp.zeros_like(q)),
          jnp.where(lane >= SB_HEAD_DIM, q, jnp.zeros_like(q))]
    r = lax.broadcasted_iota(jnp.int32, (tiles.k, tiles.k), 0)
    c = lax.broadcasted_iota(jnp.int32, (tiles.k, tiles.k), 1)
    tri = jnp.where(c > r, 1.0, 0.0).astype(BF16)
    acc_ref[...] = jnp.zeros_like(acc_ref)
    later_ref[...] = jnp.zeros_like(later_ref)

    def scores(j, slot, q_lo):
        k = k_ref[_key_block(tiles, j), :]
        lo = q_lo or 0
        for h in range(2):
            z = _dot_nt(k, qh[h][lo:, :])
            if q_lo is not None:
                z = jnp.where(_hidden(tiles, i, j, lo, visible_when_equal=False), NEG_BIG, z)
                if lo:
                    z_ref[slot, h, :, :lo] = jnp.full((tiles.k, lo), NEG_BIG, F32)
            z_ref[slot, h, :, lo:] = z

    def within_block(j, slot, q_lo):
        lo = q_lo or 0
        for h in range(2):
            z = z_ref[slot, h, :, lo:]
            neg_abs = lax.bitcast_convert_type(
                lax.bitcast_convert_type(z, jnp.uint32) | jnp.uint32(0x80000000), F32)
            softplus = jnp.maximum(z, 0.0) + jnp.log(1.0 + jnp.exp(neg_abs))
            later = _dot(tri, softplus.astype(BF16))
            if lo:
                e_ref[slot, h, :, :lo] = jnp.full((tiles.k, lo), NEG_BIG, F32)
                total_ref[slot, h, :, :lo] = jnp.zeros((1, lo), F32)
            e_ref[slot, h, :, lo:] = (z - softplus) - later
            total_ref[slot, h, :, lo:] = later[0:1, :] + softplus[0:1, :]

    def accumulate(j, slot, q_lo):
        vt = vt_ref[:, _key_block(tiles, j)]
        lo = q_lo or 0
        for h in range(2):
            w = jnp.exp(e_ref[slot, h, :, lo:])
            acc_ref[h, :, lo:] += _dot(vt, w.astype(BF16)) * jnp.exp(-later_ref[h, :, lo:])
            later_ref[h, :, lo:] += total_ref[slot, h, :, lo:]

    def finish():
        o_ref[...] = _merge_heads(acc_ref[0, :SB_HEAD_DIM, :], acc_ref[1, SB_HEAD_DIM:, :]).astype(o_ref.dtype)

    return [scores, within_block, accumulate], finish


def _sb_scratch(tiles):
    return (_tile_pair(tiles, F32) + _tile_pair(tiles, F32) + _slot_pair((2, 1, tiles.q), F32)
            + [pltpu.VMEM((2, LANES, tiles.q), F32), pltpu.VMEM((2, 1, tiles.q), F32)])


def _mla_stages(tiles, i, q_ref, k_ref, vt_ref, o_ref, s0, s1, acc_ref, m_ref):
    s_ref = _Slots(s0, s1)
    c2 = (MLA_NOPE_DIM + MLA_ROPE_DIM) ** -0.5 * LOG2E
    qh = [q_ref[:, h * MLA_QK_PAD:(h + 1) * MLA_QK_PAD] for h in range(2)]
    acc_ref[...] = jnp.zeros_like(acc_ref)
    m_ref[...] = jnp.full_like(m_ref, NEG_BIG)

    def scores(j, slot, q_lo):
        lo = q_lo or 0
        for h in range(2):
            s = _dot_nt(k_ref[_key_block(tiles, j), h * MLA_QK_PAD:(h + 1) * MLA_QK_PAD], qh[h][lo:, :])
            if q_lo is not None:
                s = jnp.where(_hidden(tiles, i, j, lo, visible_when_equal=True), NEG_BIG, s)
                if lo:
                    s_ref[slot, h, :, :lo] = jnp.full((tiles.k, lo), NEG_BIG, F32)
            s_ref[slot, h, :, lo:] = s

    def accumulate(j, slot, q_lo):
        lo = q_lo or 0
        for h in range(2):
            s = s_ref[slot, h, :, lo:]
            m = m_ref[h, :, lo:]
            m_new = jnp.maximum(m, jnp.max(s, axis=0, keepdims=True))
            p = jnp.exp2((s - m_new) * c2)
            rescale = jnp.exp2((m - m_new) * c2)
            m_ref[h, :, lo:] = m_new
            vt = vt_ref[h * _VT_PAD:(h + 1) * _VT_PAD, _key_block(tiles, j)]
            acc_ref[h, :, lo:] = rescale * acc_ref[h, :, lo:] + _dot(vt, p.astype(BF16))

    def finish():
        out = [acc_ref[h, :MLA_V_DIM, :] / acc_ref[h, MLA_V_DIM:MLA_V_DIM + 1, :] for h in range(2)]
        o_ref[...] = _merge_heads(*out).astype(o_ref.dtype)

    return [scores, accumulate], finish


def _mla_scratch(tiles):
    return _tile_pair(tiles, F32) + [pltpu.VMEM((2, _VT_PAD, tiles.q), F32), pltpu.VMEM((2, 1, tiles.q), F32)]


def _attn_kernel(build_stages, tiles, q_ref, k_ref, vt_ref, o_ref, *scratch):
    i = pl.program_id(2)
    stages, finish = build_stages(tiles, i, q_ref, k_ref, vt_ref, o_ref, *scratch)
    _sweep_keys(i, stages, tiles)
    finish()


def _attn_call(build_stages, scratch, name, q, qk_width, k, k_offset, v_t, vt_rows, batch, seq, width):
    tiles = ATTN_TILES
    n = q.shape[0]
    nq = seq // tiles.q
    return pl.pallas_call(
        functools.partial(_attn_kernel, build_stages, tiles),
        out_shape=jax.ShapeDtypeStruct((n, width), BF16),
        grid=(batch, HEAD_PAIRS, nq),
        in_specs=[pl.BlockSpec((tiles.q, qk_width), lambda b, p, i: (b * nq + i, p)),
                  pl.BlockSpec((seq, qk_width), lambda b, p, i: (b, k_offset + p)),
                  pl.BlockSpec((vt_rows, seq), lambda b, p, i: (p, b))],
        out_specs=pl.BlockSpec((tiles.q, LANES), lambda b, p, i: (b * nq + i, p)),
        scratch_shapes=scratch(tiles),
        compiler_params=_params("parallel", "parallel", "parallel"),
        name=name,
    )(q, k, v_t)


def _sb_attn(qk, v_t, batch, seq):
    return _attn_call(_sb_stages, _sb_scratch, "sb_attn", qk, LANES, qk, HEAD_PAIRS, v_t, LANES,
                      batch, seq, SB_WIDTH)


def _mla_attn(q, k, v_t, batch, seq):
    return _attn_call(_mla_stages, _mla_scratch, "mla_attn", q, 2 * MLA_QK_PAD, k, 0, v_t, 2 * _VT_PAD,
                      batch, seq, MLA_WIDTH)


HALO = 8


def _conv_mixer_kernel(tiles_per_seq, x_ref, win_ref, cw_ref, wout_ref, g_ref, b_ref, o_ref, tail_ref):
    i = pl.program_id(0)
    x = x_ref[...]
    xb = x.astype(BF16)
    gate_b = _dot(xb, win_ref[:, :D_MODEL])
    u = _dot(xb, win_ref[:, D_MODEL:2 * D_MODEL]) * _dot(xb, win_ref[:, 2 * D_MODEL:])

    @pl.when(i % tiles_per_seq == 0)
    def _():
        tail_ref[...] = jnp.zeros_like(tail_ref)

    prev = tail_ref[...]
    tail_ref[...] = u[ROW_TILE - HALO:, :]
    ext = jnp.concatenate([prev, u], axis=0)
    cw = cw_ref[...]
    conv = (cw[0:1, :] * ext[HALO - 2:HALO - 2 + ROW_TILE, :]
            + cw[1:2, :] * ext[HALO - 1:HALO - 1 + ROW_TILE, :]
            + cw[2:3, :] * u)
    m = _dot((gate_b * conv).astype(BF16), wout_ref[...])
    o_ref[...] = _layer_norm(ALPHA * x + m, g_ref[...], b_ref[...])


def _conv_mixer(x, w_in, conv_w, w_out, g, b, seq):
    n = x.shape[0]
    row = pl.BlockSpec((ROW_TILE, D_MODEL), lambda i: (i, 0))
    return pl.pallas_call(
        functools.partial(_conv_mixer_kernel, seq // ROW_TILE),
        out_shape=jax.ShapeDtypeStruct((n, D_MODEL), F32),
        grid=(n // ROW_TILE,),
        in_specs=[row, _resident(w_in.shape), _resident(conv_w.shape), _resident(w_out.shape),
                  _resident(g.shape), _resident(b.shape)],
        out_specs=row,
        scratch_shapes=[pltpu.VMEM((HALO, D_MODEL), F32)],
        compiler_params=_params("arbitrary"),
        name="conv_mixer",
    )(x, w_in, conv_w, w_out, g, b)


def _rope_tables(seq):
    half = MLA_ROPE_DIM // 2
    inv_freq = ROPE_THETA ** (-jnp.arange(0, MLA_ROPE_DIM, 2, dtype=F32) / MLA_ROPE_DIM)
    ang = jnp.arange(seq, dtype=F32)[:, None] * inv_freq[None, :]
    cos, sin = jnp.cos(ang), jnp.sin(ang)
    ones = jnp.ones((seq, MLA_NOPE_DIM), F32)
    pad = jnp.zeros((seq, MLA_QK_PAD - MLA_NOPE_DIM - 2 * half), F32)
    cos_t = jnp.concatenate([ones, cos, cos, pad], axis=1)
    sin_t = jnp.concatenate([0.0 * ones, -sin, sin, pad], axis=1)
    return cos_t, sin_t


def _even_weights(w_in, w_uq, w_ukv):
    half = MLA_ROPE_DIM // 2
    qk = MLA_NOPE_DIM + MLA_ROPE_DIM
    zeros = lambda r, c: jnp.zeros((r, c), F32)
    sb_v = slice(_X_QK, _X_QK + SB_WIDTH)
    lat = slice(_X_QK + SB_WIDTH, _X_QK + SB_WIDTH + MLA_Q_RANK + MLA_KV_RANK)
    w_kr = w_in[:, lat.stop:]
    k1, k2 = w_kr[:, :half], w_kr[:, half:]
    nope_pad = zeros(D_MODEL, MLA_NOPE_DIM)
    tail_pad = zeros(D_MODEL, MLA_QK_PAD - qk)
    wx = jnp.concatenate([w_in[:, :_X_QK], w_in[:, lat], nope_pad, k1, k2, tail_pad,
                          nope_pad, k2, k1, tail_pad], axis=1)
    wsv = w_in[:, sb_v].T
    wq = w_uq.reshape(MLA_Q_RANK, MLA_HEADS, qk)
    q_nope, q1, q2 = wq[..., :MLA_NOPE_DIM], wq[..., MLA_NOPE_DIM:MLA_NOPE_DIM + half], wq[..., MLA_NOPE_DIM + half:]
    zq = lambda c: jnp.zeros((MLA_Q_RANK, MLA_HEADS, c), F32)
    wqa = jnp.concatenate([q_nope, q1, q2, zq(MLA_QK_PAD - qk)], axis=-1).reshape(MLA_Q_RANK, _QK_WIDTH)
    wqb = jnp.concatenate([zq(MLA_NOPE_DIM), q2, q1, zq(MLA_QK_PAD - qk)], axis=-1).reshape(MLA_Q_RANK, _QK_WIDTH)
    wkv = w_ukv.reshape(MLA_KV_RANK, MLA_HEADS, MLA_NOPE_DIM + MLA_V_DIM)
    wk = jnp.concatenate([wkv[..., :MLA_NOPE_DIM],
                          jnp.zeros((MLA_KV_RANK, MLA_HEADS, MLA_QK_PAD - MLA_NOPE_DIM), F32)],
                         axis=-1).reshape(MLA_KV_RANK, _QK_WIDTH)
    wv = jnp.concatenate([wkv[..., MLA_NOPE_DIM:],
                          jnp.zeros((MLA_KV_RANK, MLA_HEADS, _VT_PAD - MLA_V_DIM), F32)],
                         axis=-1).reshape(MLA_KV_RANK, MLA_HEADS * _VT_PAD).T
    return [w.astype(BF16) for w in (wx, wsv, wqa, wqb, wk, wv)]


def kernel(x, ln_g, ln_b, ffn_w_gate, ffn_w_up, ffn_w_down, mix_w_in, mla_q_norm_g, mla_w_uq,
           mla_kv_norm_g, mla_w_ukv, mix_w_out, conv_w_in, conv_w, conv_w_out):
    batch, seq, _ = x.shape
    assert seq % FFN_ROW_TILE == 0 and seq % ROW_TILE == 0
    for tiles in (ATTN_TILES,):
        assert seq % tiles.q == 0 and tiles.q % (2 * tiles.k) == 0
    cos_t, sin_t = _rope_tables(seq)
    h = x.reshape(batch * seq, D_MODEL)
    vec = lambda a: a.reshape(1, -1)
    for layer in range(DEPTH):
        j = layer // 2
        ln = lambda s: (vec(ln_g[layer, s]), vec(ln_b[layer, s]))
        ffn_w = lambda s: (ffn_w_gate[layer, s].astype(BF16), ffn_w_up[layer, s].astype(BF16),
                           ffn_w_down[layer, s].astype(BF16))
        h = _ffn(h, *ffn_w(0), *ln(0))
        if layer % 2 == 0:
            wx, wsv, wqa, wqb, wk, wv = _even_weights(mix_w_in[j], mla_w_uq[j], mla_w_ukv[j])
            sb_qk, sb_vt, q, k, v_t = _proj_even(h, wx, wsv, vec(mla_q_norm_g[j]), wqa, wqb,
                                                 vec(mla_kv_norm_g[j]), wk, wv, cos_t, sin_t, seq)
            out_sb = _sb_attn(sb_qk, sb_vt, batch, seq)
            out_mla = _mla_attn(q, k, v_t, batch, seq)
            h = _attn_out_ffn(h, out_sb, out_mla, mix_w_out[j].astype(BF16), *ln(1), *ffn_w(1), *ln(2))
        else:
            h = _conv_mixer(h, conv_w_in[j].astype(BF16), conv_w[j], conv_w_out[j].astype(BF16), *ln(1), seq)
            h = _ffn(h, *ffn_w(1), *ln(2))
    return h.reshape(batch, seq, D_MODEL)
```

```python
import functools
from typing import NamedTuple

import jax
import jax.numpy as jnp
from jax import lax
from jax.experimental import pallas as pl
from jax.experimental.pallas import tpu as pltpu

D_MODEL = 1024
DEPTH = 4
SB_HEADS = 8
SB_HEAD_DIM = 64
MLA_HEADS = 8
MLA_NOPE_DIM = 64
MLA_ROPE_DIM = 32
MLA_V_DIM = 64
MLA_Q_RANK = 768
MLA_KV_RANK = 256
ROPE_THETA = 10000.0
CONV_K = 3
D_FF = 2816
LN_EPS = 1e-5
RMS_EPS = 1e-6
ALPHA = (2 * DEPTH) ** 0.25
SB_WIDTH = SB_HEADS * SB_HEAD_DIM
MLA_WIDTH = MLA_HEADS * MLA_V_DIM

LANES = 128
HEAD_PAIRS = SB_HEADS // 2
MLA_QK_PAD = LANES
_VT_PAD = LANES
BF16_SUBLANES = 16
_VT_ROWS = MLA_V_DIM + BF16_SUBLANES
FF_CHUNK = 256
ROW_TILE = 512
FFN_ROW_TILE = 1024


class _Tiling(NamedTuple):
    q: int
    k: int

    @property
    def diag(self):
        return self.q // self.k


ATTN_TILES = _Tiling(q=512, k=256)
VMEM_LIMIT = 56 * 1024 * 1024
PAIRS_PER_TRIP = 2

BF16 = jnp.bfloat16
F32 = jnp.float32
NEG_BIG = -1e30
LOG2E = 1.4426950408889634


def _dot(a, b):
    return jnp.dot(a, b, preferred_element_type=F32)


def _dot_nt(a, b):
    return lax.dot_general(a, b, (((1,), (1,)), ((), ())), preferred_element_type=F32)


def _layer_norm(y, g, b):
    mu = jnp.mean(y, axis=-1, keepdims=True)
    d = y - mu
    var = jnp.mean(d * d, axis=-1, keepdims=True)
    return d * lax.rsqrt(var + LN_EPS) * g + b


def _rms_norm(c, g):
    return c * lax.rsqrt(jnp.mean(c * c, axis=-1, keepdims=True) + RMS_EPS) * g


def _params(*sem):
    return pltpu.CompilerParams(dimension_semantics=sem, vmem_limit_bytes=VMEM_LIMIT)


def _resident(shape):
    nd = len(shape)
    return pl.BlockSpec(shape, lambda *_: (0,) * nd, pipeline_mode=pl.Buffered(1))


def _half_step_ffn(x, wg_ref, wu_ref, wd_ref, g_ref, b_ref, h_ref):
    xb = x.astype(BF16)
    for c in range(D_FF // FF_CHUNK):
        sl = slice(c * FF_CHUNK, (c + 1) * FF_CHUNK)
        gate = _dot(xb, wg_ref[:, sl])
        up = _dot(xb, wu_ref[:, sl])
        h_ref[:, sl] = (gate * (1.0 / (1.0 + jnp.exp(-gate))) * up).astype(BF16)
    f = _dot(h_ref[...], wd_ref[...])
    return _layer_norm(ALPHA * x + 0.5 * f, g_ref[...], b_ref[...])


def _ffn_kernel(x_ref, wg_ref, wu_ref, wd_ref, g_ref, b_ref, o_ref, h_ref):
    o_ref[...] = _half_step_ffn(x_ref[...], wg_ref, wu_ref, wd_ref, g_ref, b_ref, h_ref)


def _attn_out_ffn_kernel(x_ref, sb_ref, mla_ref, wo_ref, go_ref, bo_ref,
                         wg_ref, wu_ref, wd_ref, g_ref, b_ref, o_ref, h_ref):
    m = _dot(sb_ref[...], wo_ref[:SB_WIDTH, :]) + _dot(mla_ref[...], wo_ref[SB_WIDTH:, :])
    x = _layer_norm(ALPHA * x_ref[...] + m, go_ref[...], bo_ref[...])
    o_ref[...] = _half_step_ffn(x, wg_ref, wu_ref, wd_ref, g_ref, b_ref, h_ref)


def _ffn_call(body, name, rows, residents):
    n = rows[0][0].shape[0]
    row = lambda w: pl.BlockSpec((FFN_ROW_TILE, w), lambda i: (i, 0))
    return pl.pallas_call(
        body,
        out_shape=jax.ShapeDtypeStruct((n, D_MODEL), F32),
        grid=(n // FFN_ROW_TILE,),
        in_specs=[row(w) for _, w in rows] + [_resident(a.shape) for a in residents],
        out_specs=row(D_MODEL),
        scratch_shapes=[pltpu.VMEM((FFN_ROW_TILE, D_FF), BF16)],
        compiler_params=_params("parallel"),
        name=name,
    )(*[a for a, _ in rows], *residents)


def _ffn(x, wg, wu, wd, g, b):
    return _ffn_call(_ffn_kernel, "ffn", [(x, D_MODEL)], [wg, wu, wd, g, b])


def _attn_out_ffn(x, sb, mla, wo, go, bo, wg, wu, wd, g, b):
    return _ffn_call(_attn_out_ffn_kernel, "attn_out_ffn",
                     [(x, D_MODEL), (sb, SB_WIDTH), (mla, MLA_WIDTH)], [wo, go, bo, wg, wu, wd, g, b])


_X_QK = 2 * SB_WIDTH
_X_CQ = _X_QK + MLA_Q_RANK
_X_CKV = _X_CQ + MLA_KV_RANK
_X_KRA = _X_CKV + LANES
_X_KRB = _X_KRA + LANES
_QK_WIDTH = MLA_HEADS * MLA_QK_PAD


def _proj_even_kernel(x_ref, wx_ref, wsv_ref, gq_ref, wqa_ref, wqb_ref, gkv_ref, wk_ref, wv_ref,
                      cos_ref, sin_ref, sb_ref, sbv_ref, q_ref, k_ref, v_ref):
    xb = x_ref[...].astype(BF16)
    cos = cos_ref[...]
    sin = sin_ref[...]
    sb_ref[:, :SB_WIDTH] = (_dot(xb, wx_ref[:, :SB_WIDTH]) * (SB_HEAD_DIM ** -0.5)).astype(BF16)
    sb_ref[:, SB_WIDTH:] = _dot(xb, wx_ref[:, SB_WIDTH:_X_QK]).astype(BF16)
    sbv_ref[...] = _dot_nt(wsv_ref[...], xb).astype(BF16)
    cq = _rms_norm(_dot(xb, wx_ref[:, _X_QK:_X_CQ]), gq_ref[...]).astype(BF16)
    for h in range(MLA_HEADS):
        sl = slice(h * MLA_QK_PAD, (h + 1) * MLA_QK_PAD)
        q_ref[:, sl] = (_dot(cq, wqa_ref[:, sl]) * cos + _dot(cq, wqb_ref[:, sl]) * sin).astype(BF16)
    ckv = _rms_norm(_dot(xb, wx_ref[:, _X_CQ:_X_CKV]), gkv_ref[...]).astype(BF16)
    kr = _dot(xb, wx_ref[:, _X_CKV:_X_KRA]) * cos + _dot(xb, wx_ref[:, _X_KRA:_X_KRB]) * sin
    for h in range(MLA_HEADS):
        sl = slice(h * MLA_QK_PAD, (h + 1) * MLA_QK_PAD)
        k_ref[:, sl] = (_dot(ckv, wk_ref[:, sl]) + kr).astype(BF16)
    vt = _dot_nt(wv_ref[...], ckv)
    ones_row = lax.broadcasted_iota(jnp.int32, vt.shape, 0) % _VT_PAD == MLA_V_DIM
    v_ref[...] = jnp.where(ones_row, 1.0, vt).astype(BF16)


def _proj_even(x, wx, wsv, gq, wqa, wqb, gkv, wk, wv, cos_t, sin_t, seq):
    n = x.shape[0]
    tiles_per_seq = seq // ROW_TILE
    row = lambda w: pl.BlockSpec((ROW_TILE, w), lambda i: (i, 0))
    col = lambda h: pl.BlockSpec((h, ROW_TILE), lambda i: (0, i))
    pos = pl.BlockSpec((ROW_TILE, LANES), lambda i: (i % tiles_per_seq, 0))
    return pl.pallas_call(
        _proj_even_kernel,
        out_shape=(jax.ShapeDtypeStruct((n, _X_QK), BF16),
                   jax.ShapeDtypeStruct((SB_WIDTH, n), BF16),
                   jax.ShapeDtypeStruct((n, _QK_WIDTH), BF16),
                   jax.ShapeDtypeStruct((n, _QK_WIDTH), BF16),
                   jax.ShapeDtypeStruct((MLA_HEADS * _VT_PAD, n), BF16)),
        grid=(n // ROW_TILE,),
        in_specs=[row(D_MODEL), _resident(wx.shape), _resident(wsv.shape), _resident(gq.shape),
                  _resident(wqa.shape), _resident(wqb.shape), _resident(gkv.shape),
                  _resident(wk.shape), _resident(wv.shape), pos, pos],
        out_specs=(row(_X_QK), col(SB_WIDTH), row(_QK_WIDTH), row(_QK_WIDTH), col(MLA_HEADS * _VT_PAD)),
        compiler_params=_params("parallel"),
        name="proj_even",
    )(x, wx, wsv, gq, wqa, wqb, gkv, wk, wv, cos_t, sin_t)


def _sweep_keys(i, stages, tiles):
    depth = len(stages)
    blocks = (i + 1) * tiles.diag
    lead = [depth - 1 - s for s in range(depth)]
    peel = max(0, tiles.diag - depth + 1)
    for n in range(1 - depth, peel):
        for s, stage in enumerate(stages):
            b = n + lead[s]
            if b >= 0:
                stage(blocks - 1 - b, b % 2, (tiles.diag - 1 - b) * tiles.k if b < tiles.diag else None)

    def step_pair(t):
        for u in range(2):
            for s in range(depth):
                b_off = peel + u + lead[s]
                stages[s](blocks - 1 - b_off - 2 * t, b_off % 2, None)

    def body(t, carry):
        for r in range(PAIRS_PER_TRIP):
            step_pair(PAIRS_PER_TRIP * t + r)
        return carry

    pairs = i * (tiles.diag // 2)
    lax.fori_loop(0, pairs // PAIRS_PER_TRIP, body, 0)
    if (tiles.diag // 2) % PAIRS_PER_TRIP:
        for r in range(PAIRS_PER_TRIP - 1):
            pl.when(r < pairs % PAIRS_PER_TRIP)(
                functools.partial(step_pair, pairs - pairs % PAIRS_PER_TRIP + r))
    for e in range(depth - 1):
        for s, stage in enumerate(stages):
            j = depth - 2 - e - lead[s]
            if j >= 0:
                stage(j, (j + 1) % 2, None)


def _hidden(tiles, i, j, q_lo, visible_when_equal):
    shape = (tiles.k, tiles.q - q_lo)
    key = j * tiles.k + lax.broadcasted_iota(jnp.int32, shape, 0)
    qry = i * tiles.q + q_lo + lax.broadcasted_iota(jnp.int32, shape, 1)
    return key > qry if visible_when_equal else key >= qry


def _key_block(tiles, j):
    return pl.ds(pl.multiple_of(j * tiles.k, tiles.k), tiles.k)


def _merge_heads(o0, o1):
    return jnp.concatenate([o0, o1], axis=0).T


class _Slots:
    def __init__(self, first, second):
        self._refs = (first, second)

    def __getitem__(self, idx):
        return self._refs[idx[0]][idx[1:]]

    def __setitem__(self, idx, value):
        self._refs[idx[0]][idx[1:]] = value


def _slot_pair(shape, dtype):
    return [pltpu.VMEM(shape, dtype)] * 2


def _tile_pair(tiles, dtype):
    return _slot_pair((2, tiles.k, tiles.q), dtype)


def _sb_stages(tiles, i, q_ref, k_ref, vt_ref, o_ref, z0, z1, e0, e1, total0, total1, acc_ref, later_ref):
    z_ref, e_ref, total_ref = _Slots(z0, z1), _Slots(e0, e1), _Slots(total0, total1)
    q = q_ref[...]
    lane = lax.broadcasted_iota(jnp.int32, (tiles.q, LANES), 1)
    qh = [jnp.where(lane < SB_HEAD_DIM, q, jnp.zeros_like(q)),
          jnp.where(lane >= SB_HEAD_DIM, q, jnp.zeros_like(q))]
    r = lax.broadcasted_iota(jnp.int32, (tiles.k, tiles.k), 0)
    c = lax.broadcasted_iota(jnp.int32, (tiles.k, tiles.k), 1)
    tri = jnp.where(c > r, 1.0, 0.0).astype(BF16)
    acc_ref[...] = jnp.zeros_like(acc_ref)
    later_ref[...] = jnp.zeros_like(later_ref)

    def scores(j, slot, q_lo):
        k = k_ref[_key_block(tiles, j), :]
        lo = q_lo or 0
        for h in range(2):
            z = _dot_nt(k, qh[h][lo:, :])
            if q_lo is not None:
                z = jnp.where(_hidden(tiles, i, j, lo, visible_when_equal=False), NEG_BIG, z)
                if lo:
                    z_ref[slot, h, :, :lo] = jnp.full((tiles.k, lo), NEG_BIG, F32)
            z_ref[slot, h, :, lo:] = z

    def within_block(j, slot, q_lo):
        lo = q_lo or 0
        for h in range(2):
            z = z_ref[slot, h, :, lo:]
            neg_abs = lax.bitcast_convert_type(
                lax.bitcast_convert_type(z, jnp.uint32) | jnp.uint32(0x80000000), F32)
            softplus = jnp.maximum(z, 0.0) + jnp.log(1.0 + jnp.exp(neg_abs))
            later = _dot(tri, softplus.astype(BF16))
            if lo:
                e_ref[slot, h, :, :lo] = jnp.full((tiles.k, lo), NEG_BIG, F32)
                total_ref[slot, h, :, :lo] = jnp.zeros((1, lo), F32)
            e_ref[slot, h, :, lo:] = (z - softplus) - later
            total_ref[slot, h, :, lo:] = later[0:1, :] + softplus[0:1, :]

    def accumulate(j, slot, q_lo):
        lo = q_lo or 0
        for h in range(2):
            w = jnp.exp(e_ref[slot, h, :, lo:])
            vt = vt_ref[h * SB_HEAD_DIM:(h + 1) * SB_HEAD_DIM, _key_block(tiles, j)]
            acc_ref[h, :, lo:] += _dot(vt, w.astype(BF16)) * jnp.exp(-later_ref[h, :, lo:])
            later_ref[h, :, lo:] += total_ref[slot, h, :, lo:]

    def finish():
        o_ref[...] = _merge_heads(acc_ref[0], acc_ref[1]).astype(o_ref.dtype)

    return [scores, within_block, accumulate], finish


def _sb_scratch(tiles):
    return (_tile_pair(tiles, F32) + _tile_pair(tiles, F32) + _slot_pair((2, 1, tiles.q), F32)
            + [pltpu.VMEM((2, SB_HEAD_DIM, tiles.q), F32), pltpu.VMEM((2, 1, tiles.q), F32)])


def _mla_stages(tiles, i, q_ref, k_ref, vt_ref, o_ref, s0, s1, acc_ref, m_ref):
    s_ref = _Slots(s0, s1)
    c2 = (MLA_NOPE_DIM + MLA_ROPE_DIM) ** -0.5 * LOG2E
    qh = [q_ref[:, h * MLA_QK_PAD:(h + 1) * MLA_QK_PAD] for h in range(2)]
    acc_ref[...] = jnp.zeros_like(acc_ref)
    m_ref[...] = jnp.full_like(m_ref, NEG_BIG)

    def scores(j, slot, q_lo):
        lo = q_lo or 0
        for h in range(2):
            s = _dot_nt(k_ref[_key_block(tiles, j), h * MLA_QK_PAD:(h + 1) * MLA_QK_PAD], qh[h][lo:, :])
            if q_lo is not None:
                s = jnp.where(_hidden(tiles, i, j, lo, visible_when_equal=True), NEG_BIG, s)
                if lo:
                    s_ref[slot, h, :, :lo] = jnp.full((tiles.k, lo), NEG_BIG, F32)
            s_ref[slot, h, :, lo:] = s

    def accumulate(j, slot, q_lo):
        lo = q_lo or 0
        for h in range(2):
            s = s_ref[slot, h, :, lo:]
            m = m_ref[h, :, lo:]
            m_new = jnp.maximum(m, jnp.max(s, axis=0, keepdims=True))
            p = jnp.exp2((s - m_new) * c2)
            rescale = jnp.exp2((m - m_new) * c2)
            m_ref[h, :, lo:] = m_new
            vt = vt_ref[h * _VT_PAD:h * _VT_PAD + _VT_ROWS, _key_block(tiles, j)]
            acc_ref[h, :, lo:] = rescale * acc_ref[h, :, lo:] + _dot(vt, p.astype(BF16))

    def finish():
        out = [acc_ref[h, :MLA_V_DIM, :] / acc_ref[h, MLA_V_DIM:MLA_V_DIM + 1, :] for h in range(2)]
        o_ref[...] = _merge_heads(*out).astype(o_ref.dtype)

    return [scores, accumulate], finish


def _mla_scratch(tiles):
    return _tile_pair(tiles, F32) + [pltpu.VMEM((2, _VT_ROWS, tiles.q), F32), pltpu.VMEM((2, 1, tiles.q), F32)]


def _attn_kernel(build_stages, tiles, q_ref, k_ref, vt_ref, o_ref, *scratch):
    i = pl.program_id(2)
    stages, finish = build_stages(tiles, i, q_ref, k_ref, vt_ref, o_ref, *scratch)
    _sweep_keys(i, stages, tiles)
    finish()


def _attn_call(build_stages, scratch, name, q, qk_width, k, k_offset, v_t, vt_rows, batch, seq, width):
    tiles = ATTN_TILES
    n = q.shape[0]
    nq = seq // tiles.q
    return pl.pallas_call(
        functools.partial(_attn_kernel, build_stages, tiles),
        out_shape=jax.ShapeDtypeStruct((n, width), BF16),
        grid=(batch, HEAD_PAIRS, nq),
        in_specs=[pl.BlockSpec((tiles.q, qk_width), lambda b, p, i: (b * nq + i, p)),
                  pl.BlockSpec((seq, qk_width), lambda b, p, i: (b, k_offset + p)),
                  pl.BlockSpec((vt_rows, seq), lambda b, p, i: (p, b))],
        out_specs=pl.BlockSpec((tiles.q, LANES), lambda b, p, i: (b * nq + i, p)),
        scratch_shapes=scratch(tiles),
        compiler_params=_params("parallel", "parallel", "parallel"),
        name=name,
    )(q, k, v_t)


def _sb_attn(qk, v_t, batch, seq):
    return _attn_call(_sb_stages, _sb_scratch, "sb_attn", qk, LANES, qk, HEAD_PAIRS, v_t, LANES,
                      batch, seq, SB_WIDTH)


def _mla_attn(q, k, v_t, batch, seq):
    return _attn_call(_mla_stages, _mla_scratch, "mla_attn", q, 2 * MLA_QK_PAD, k, 0, v_t, 2 * _VT_PAD,
                      batch, seq, MLA_WIDTH)


HALO = 8


def _conv_mixer_kernel(tiles_per_seq, x_ref, win_ref, cw_ref, wout_ref, g_ref, b_ref, o_ref, tail_ref):
    i = pl.program_id(0)
    x = x_ref[...]
    xb = x.astype(BF16)
    gate_b = _dot(xb, win_ref[:, :D_MODEL])
    u = _dot(xb, win_ref[:, D_MODEL:2 * D_MODEL]) * _dot(xb, win_ref[:, 2 * D_MODEL:])

    @pl.when(i % tiles_per_seq == 0)
    def _():
        tail_ref[...] = jnp.zeros_like(tail_ref)

    prev = tail_ref[...]
    tail_ref[...] = u[ROW_TILE - HALO:, :]
    ext = jnp.concatenate([prev, u], axis=0)
    cw = cw_ref[...]
    conv = (cw[0:1, :] * ext[HALO - 2:HALO - 2 + ROW_TILE, :]
            + cw[1:2, :] * ext[HALO - 1:HALO - 1 + ROW_TILE, :]
            + cw[2:3, :] * u)
    m = _dot((gate_b * conv).astype(BF16), wout_ref[...])
    o_ref[...] = _layer_norm(ALPHA * x + m, g_ref[...], b_ref[...])


def _conv_mixer(x, w_in, conv_w, w_out, g, b, seq):
    n = x.shape[0]
    row = pl.BlockSpec((ROW_TILE, D_MODEL), lambda i: (i, 0))
    return pl.pallas_call(
        functools.partial(_conv_mixer_kernel, seq // ROW_TILE),
        out_shape=jax.ShapeDtypeStruct((n, D_MODEL), F32),
        grid=(n // ROW_TILE,),
        in_specs=[row, _resident(w_in.shape), _resident(conv_w.shape), _resident(w_out.shape),
                  _resident(g.shape), _resident(b.shape)],
        out_specs=row,
        scratch_shapes=[pltpu.VMEM((HALO, D_MODEL), F32)],
        compiler_params=_params("arbitrary"),
        name="conv_mixer",
    )(x, w_in, conv_w, w_out, g, b)


def _rope_tables(seq):
    half = MLA_ROPE_DIM // 2
    inv_freq = ROPE_THETA ** (-jnp.arange(0, MLA_ROPE_DIM, 2, dtype=F32) / MLA_ROPE_DIM)
    ang = jnp.arange(seq, dtype=F32)[:, None] * inv_freq[None, :]
    cos, sin = jnp.cos(ang), jnp.sin(ang)
    ones = jnp.ones((seq, MLA_NOPE_DIM), F32)
    pad = jnp.zeros((seq, MLA_QK_PAD - MLA_NOPE_DIM - 2 * half), F32)
    cos_t = jnp.concatenate([ones, cos, cos, pad], axis=1)
    sin_t = jnp.concatenate([0.0 * ones, -sin, sin, pad], axis=1)
    return cos_t, sin_t


def _even_weights(w_in, w_uq, w_ukv):
    half = MLA_ROPE_DIM // 2
    qk = MLA_NOPE_DIM + MLA_ROPE_DIM
    zeros = lambda r, c: jnp.zeros((r, c), F32)
    sb_v = slice(_X_QK, _X_QK + SB_WIDTH)
    lat = slice(_X_QK + SB_WIDTH, _X_QK + SB_WIDTH + MLA_Q_RANK + MLA_KV_RANK)
    w_kr = w_in[:, lat.stop:]
    k1, k2 = w_kr[:, :half], w_kr[:, half:]
    nope_pad = zeros(D_MODEL, MLA_NOPE_DIM)
    tail_pad = zeros(D_MODEL, MLA_QK_PAD - qk)
    wx = jnp.concatenate([w_in[:, :_X_QK], w_in[:, lat], nope_pad, k1, k2, tail_pad,
                          nope_pad, k2, k1, tail_pad], axis=1)
    wsv = w_in[:, sb_v].T
    wq = w_uq.reshape(MLA_Q_RANK, MLA_HEADS, qk)
    q_nope, q1, q2 = wq[..., :MLA_NOPE_DIM], wq[..., MLA_NOPE_DIM:MLA_NOPE_DIM + half], wq[..., MLA_NOPE_DIM + half:]
    zq = lambda c: jnp.zeros((MLA_Q_RANK, MLA_HEADS, c), F32)
    wqa = jnp.concatenate([q_nope, q1, q2, zq(MLA_QK_PAD - qk)], axis=-1).reshape(MLA_Q_RANK, _QK_WIDTH)
    wqb = jnp.concatenate([zq(MLA_NOPE_DIM), q2, q1, zq(MLA_QK_PAD - qk)], axis=-1).reshape(MLA_Q_RANK, _QK_WIDTH)
    wkv = w_ukv.reshape(MLA_KV_RANK, MLA_HEADS, MLA_NOPE_DIM + MLA_V_DIM)
    wk = jnp.concatenate([wkv[..., :MLA_NOPE_DIM],
                          jnp.zeros((MLA_KV_RANK, MLA_HEADS, MLA_QK_PAD - MLA_NOPE_DIM), F32)],
                         axis=-1).reshape(MLA_KV_RANK, _QK_WIDTH)
    wv = jnp.concatenate([wkv[..., MLA_NOPE_DIM:],
                          jnp.zeros((MLA_KV_RANK, MLA_HEADS, _VT_PAD - MLA_V_DIM), F32)],
                         axis=-1).reshape(MLA_KV_RANK, MLA_HEADS * _VT_PAD).T
    return [w.astype(BF16) for w in (wx, wsv, wqa, wqb, wk, wv)]


def kernel(x, ln_g, ln_b, ffn_w_gate, ffn_w_up, ffn_w_down, mix_w_in, mla_q_norm_g, mla_w_uq,
           mla_kv_norm_g, mla_w_ukv, mix_w_out, conv_w_in, conv_w, conv_w_out):
    batch, seq, _ = x.shape
    assert seq % FFN_ROW_TILE == 0 and seq % ROW_TILE == 0
    for tiles in (ATTN_TILES,):
        assert seq % tiles.q == 0 and tiles.q % (2 * tiles.k) == 0
    cos_t, sin_t = _rope_tables(seq)
    h = x.reshape(batch * seq, D_MODEL)
    vec = lambda a: a.reshape(1, -1)
    for layer in range(DEPTH):
        j = layer // 2
        ln = lambda s: (vec(ln_g[layer, s]), vec(ln_b[layer, s]))
        ffn_w = lambda s: (ffn_w_gate[layer, s].astype(BF16), ffn_w_up[layer, s].astype(BF16),
                           ffn_w_down[layer, s].astype(BF16))
        h = _ffn(h, *ffn_w(0), *ln(0))
        if layer % 2 == 0:
            wx, wsv, wqa, wqb, wk, wv = _even_weights(mix_w_in[j], mla_w_uq[j], mla_w_ukv[j])
            sb_qk, sb_vt, q, k, v_t = _proj_even(h, wx, wsv, vec(mla_q_norm_g[j]), wqa, wqb,
                                                 vec(mla_kv_norm_g[j]), wk, wv, cos_t, sin_t, seq)
            out_sb = _sb_attn(sb_qk, sb_vt, batch, seq)
            out_mla = _mla_attn(q, k, v_t, batch, seq)
            h = _attn_out_ffn(h, out_sb, out_mla, mix_w_out[j].astype(BF16), *ln(1), *ffn_w(1), *ln(2))
        else:
            h = _conv_mixer(h, conv_w_in[j].astype(BF16), conv_w[j], conv_w_out[j].astype(BF16), *ln(1), seq)
            h = _ffn(h, *ffn_w(1), *ln(2))
    return h.reshape(batch, seq, D_MODEL)
```

```python
import functools
from typing import NamedTuple

import jax
import jax.numpy as jnp
from jax import lax
from jax.experimental import pallas as pl
from jax.experimental.pallas import tpu as pltpu

D_MODEL = 1024
DEPTH = 4
SB_HEADS = 8
SB_HEAD_DIM = 64
MLA_HEADS = 8
MLA_NOPE_DIM = 64
MLA_ROPE_DIM = 32
MLA_V_DIM = 64
MLA_Q_RANK = 768
MLA_KV_RANK = 256
ROPE_THETA = 10000.0
CONV_K = 3
D_FF = 2816
LN_EPS = 1e-5
RMS_EPS = 1e-6
ALPHA = (2 * DEPTH) ** 0.25
SB_WIDTH = SB_HEADS * SB_HEAD_DIM
MLA_WIDTH = MLA_HEADS * MLA_V_DIM

LANES = 128
HEAD_PAIRS = SB_HEADS // 2
MLA_QK_PAD = LANES
_VT_PAD = LANES
BF16_SUBLANES = 16
_VT_ROWS = MLA_V_DIM + BF16_SUBLANES
FF_CHUNK = 256
ROW_TILE = 512
FFN_ROW_TILE = 1024


class _Tiling(NamedTuple):
    q: int
    k: int

    @property
    def diag(self):
        return self.q // self.k


ATTN_TILES = _Tiling(q=512, k=256)
VMEM_LIMIT = 56 * 1024 * 1024
PAIRS_PER_TRIP = 2

BF16 = jnp.bfloat16
F32 = jnp.float32
NEG_BIG = -1e30
LOG2E = 1.4426950408889634


def _dot(a, b):
    return jnp.dot(a, b, preferred_element_type=F32)


def _dot_nt(a, b):
    return lax.dot_general(a, b, (((1,), (1,)), ((), ())), preferred_element_type=F32)


def _layer_norm(y, g, b):
    mu = jnp.mean(y, axis=-1, keepdims=True)
    d = y - mu
    var = jnp.mean(d * d, axis=-1, keepdims=True)
    return d * lax.rsqrt(var + LN_EPS) * g + b


def _rms_norm(c, g):
    return c * lax.rsqrt(jnp.mean(c * c, axis=-1, keepdims=True) + RMS_EPS) * g


def _params(*sem):
    return pltpu.CompilerParams(dimension_semantics=sem, vmem_limit_bytes=VMEM_LIMIT)


def _resident(shape):
    nd = len(shape)
    return pl.BlockSpec(shape, lambda *_: (0,) * nd, pipeline_mode=pl.Buffered(1))


def _half_step_ffn(x, wg_ref, wu_ref, wd_ref, g_ref, b_ref, h_ref):
    xb = x.astype(BF16)
    for c in range(D_FF // FF_CHUNK):
        sl = slice(c * FF_CHUNK, (c + 1) * FF_CHUNK)
        gate = _dot(xb, wg_ref[:, sl])
        up = _dot(xb, wu_ref[:, sl])
        h_ref[:, sl] = (gate * (1.0 / (1.0 + jnp.exp(-gate))) * up).astype(BF16)
    f = _dot(h_ref[...], wd_ref[...])
    return _layer_norm(ALPHA * x + 0.5 * f, g_ref[...], b_ref[...])


def _ffn_kernel(x_ref, wg_ref, wu_ref, wd_ref, g_ref, b_ref, o_ref, h_ref):
    o_ref[...] = _half_step_ffn(x_ref[...], wg_ref, wu_ref, wd_ref, g_ref, b_ref, h_ref)


def _attn_out_ffn_kernel(x_ref, sb_ref, mla_ref, wo_ref, go_ref, bo_ref,
                         wg_ref, wu_ref, wd_ref, g_ref, b_ref, o_ref, h_ref):
    m = _dot(sb_ref[...], wo_ref[:SB_WIDTH, :]) + _dot(mla_ref[...], wo_ref[SB_WIDTH:, :])
    x = _layer_norm(ALPHA * x_ref[...] + m, go_ref[...], bo_ref[...])
    o_ref[...] = _half_step_ffn(x, wg_ref, wu_ref, wd_ref, g_ref, b_ref, h_ref)


def _ffn_call(body, name, rows, residents):
    n = rows[0][0].shape[0]
    row = lambda w: pl.BlockSpec((FFN_ROW_TILE, w), lambda i: (i, 0))
    return pl.pallas_call(
        body,
        out_shape=jax.ShapeDtypeStruct((n, D_MODEL), F32),
        grid=(n // FFN_ROW_TILE,),
        in_specs=[row(w) for _, w in rows] + [_resident(a.shape) for a in residents],
        out_specs=row(D_MODEL),
        scratch_shapes=[pltpu.VMEM((FFN_ROW_TILE, D_FF), BF16)],
        compiler_params=_params("parallel"),
        name=name,
    )(*[a for a, _ in rows], *residents)


def _ffn(x, wg, wu, wd, g, b):
    return _ffn_call(_ffn_kernel, "ffn", [(x, D_MODEL)], [wg, wu, wd, g, b])


def _attn_out_ffn(x, sb, mla, wo, go, bo, wg, wu, wd, g, b):
    return _ffn_call(_attn_out_ffn_kernel, "attn_out_ffn",
                     [(x, D_MODEL), (sb, SB_WIDTH), (mla, MLA_WIDTH)], [wo, go, bo, wg, wu, wd, g, b])


_X_QK = 2 * SB_WIDTH
_X_CQ = _X_QK + MLA_Q_RANK
_X_CKV = _X_CQ + MLA_KV_RANK
_X_KRA = _X_CKV + LANES
_X_KRB = _X_KRA + LANES
_QK_WIDTH = MLA_HEADS * MLA_QK_PAD


def _proj_even_kernel(x_ref, wx_ref, wsv_ref, gq_ref, wqa_ref, wqb_ref, gkv_ref, wk_ref, wv_ref,
                      cos_ref, sin_ref, sb_ref, sbv_ref, q_ref, k_ref, v_ref):
    xb = x_ref[...].astype(BF16)
    cos = cos_ref[...]
    sin = sin_ref[...]
    sb_ref[:, :SB_WIDTH] = (_dot(xb, wx_ref[:, :SB_WIDTH]) * (SB_HEAD_DIM ** -0.5)).astype(BF16)
    sb_ref[:, SB_WIDTH:] = _dot(xb, wx_ref[:, SB_WIDTH:_X_QK]).astype(BF16)
    sbv_ref[...] = _dot_nt(wsv_ref[...], xb).astype(BF16)
    cq = _rms_norm(_dot(xb, wx_ref[:, _X_QK:_X_CQ]), gq_ref[...]).astype(BF16)
    for h in range(MLA_HEADS):
        sl = slice(h * MLA_QK_PAD, (h + 1) * MLA_QK_PAD)
        q_ref[:, sl] = (_dot(cq, wqa_ref[:, sl]) * cos + _dot(cq, wqb_ref[:, sl]) * sin).astype(BF16)
    ckv = _rms_norm(_dot(xb, wx_ref[:, _X_CQ:_X_CKV]), gkv_ref[...]).astype(BF16)
    kr = _dot(xb, wx_ref[:, _X_CKV:_X_KRA]) * cos + _dot(xb, wx_ref[:, _X_KRA:_X_KRB]) * sin
    for h in range(MLA_HEADS):
        sl = slice(h * MLA_QK_PAD, (h + 1) * MLA_QK_PAD)
        k_ref[:, sl] = (_dot(ckv, wk_ref[:, sl]) + kr).astype(BF16)
    vt = _dot_nt(wv_ref[...], ckv)
    ones_row = lax.broadcasted_iota(jnp.int32, vt.shape, 0) % _VT_PAD == MLA_V_DIM
    v_ref[...] = jnp.where(ones_row, 1.0, vt).astype(BF16)


def _proj_even(x, wx, wsv, gq, wqa, wqb, gkv, wk, wv, cos_t, sin_t, seq):
    n = x.shape[0]
    tiles_per_seq = seq // ROW_TILE
    row = lambda w: pl.BlockSpec((ROW_TILE, w), lambda i: (i, 0))
    col = lambda h: pl.BlockSpec((h, ROW_TILE), lambda i: (0, i))
    pos = pl.BlockSpec((ROW_TILE, LANES), lambda i: (i % tiles_per_seq, 0))
    return pl.pallas_call(
        _proj_even_kernel,
        out_shape=(jax.ShapeDtypeStruct((n, _X_QK), BF16),
                   jax.ShapeDtypeStruct((SB_WIDTH, n), BF16),
                   jax.ShapeDtypeStruct((n, _QK_WIDTH), BF16),
                   jax.ShapeDtypeStruct((n, _QK_WIDTH), BF16),
                   jax.ShapeDtypeStruct((MLA_HEADS * _VT_PAD, n), BF16)),
        grid=(n // ROW_TILE,),
        in_specs=[row(D_MODEL), _resident(wx.shape), _resident(wsv.shape), _resident(gq.shape),
                  _resident(wqa.shape), _resident(wqb.shape), _resident(gkv.shape),
                  _resident(wk.shape), _resident(wv.shape), pos, pos],
        out_specs=(row(_X_QK), col(SB_WIDTH), row(_QK_WIDTH), row(_QK_WIDTH), col(MLA_HEADS * _VT_PAD)),
        compiler_params=_params("parallel"),
        name="proj_even",
    )(x, wx, wsv, gq, wqa, wqb, gkv, wk, wv, cos_t, sin_t)


def _sweep_keys(i, stages, tiles):
    depth = len(stages)
    blocks = (i + 1) * tiles.diag
    lead = [depth - 1 - s for s in range(depth)]
    peel = max(0, tiles.diag - depth + 1)
    for n in range(1 - depth, peel):
        for s, stage in enumerate(stages):
            b = n + lead[s]
            if b >= 0:
                stage(blocks - 1 - b, b % 2, (tiles.diag - 1 - b) * tiles.k if b < tiles.diag else None)

    def step_pair(t):
        for u in range(2):
            for s in range(depth):
                b_off = peel + u + lead[s]
                stages[s](blocks - 1 - b_off - 2 * t, b_off % 2, None)

    def body(t, carry):
        for r in range(PAIRS_PER_TRIP):
            step_pair(PAIRS_PER_TRIP * t + r)
        return carry

    pairs = i * (tiles.diag // 2)
    lax.fori_loop(0, pairs // PAIRS_PER_TRIP, body, 0)
    if (tiles.diag // 2) % PAIRS_PER_TRIP:
        for r in range(PAIRS_PER_TRIP - 1):
            pl.when(r < pairs % PAIRS_PER_TRIP)(
                functools.partial(step_pair, pairs - pairs % PAIRS_PER_TRIP + r))
    for e in range(depth - 1):
        for s, stage in enumerate(stages):
            j = depth - 2 - e - lead[s]
            if j >= 0:
                stage(j, (j + 1) % 2, None)


def _hidden(tiles, i, j, q_lo, visible_when_equal):
    shape = (tiles.k, tiles.q - q_lo)
    key = j * tiles.k + lax.broadcasted_iota(jnp.int32, shape, 0)
    qry = i * tiles.q + q_lo + lax.broadcasted_iota(jnp.int32, shape, 1)
    return key > qry if visible_when_equal else key >= qry


def _key_block(tiles, j):
    return pl.ds(pl.multiple_of(j * tiles.k, tiles.k), tiles.k)


def _merge_heads(o0, o1):
    return jnp.concatenate([o0, o1], axis=0).T


class _Slots:
    def __init__(self, first, second):
        self._refs = (first, second)

    def __getitem__(self, idx):
        return self._refs[idx[0]][idx[1:]]

    def __setitem__(self, idx, value):
        self._refs[idx[0]][idx[1:]] = value


def _slot_pair(shape, dtype):
    return [pltpu.VMEM(shape, dtype)] * 2


def _tile_pair(tiles, dtype):
    return _slot_pair((2, tiles.k, tiles.q), dtype)


def _sb_stages(tiles, i, q_ref, k_ref, vt_ref, o_ref, z0, z1, e0, e1, total0, total1, acc_ref, later_ref):
    z_ref, e_ref, total_ref = _Slots(z0, z1), _Slots(e0, e1), _Slots(total0, total1)
    q = q_ref[...]
    lane = lax.broadcasted_iota(jnp.int32, (tiles.q, LANES), 1)
    qh = [jnp.where(lane < SB_HEAD_DIM, q, jnp.zeros_like(q)),
          jnp.where(lane >= SB_HEAD_DIM, q, jnp.zeros_like(q))]
    r = lax.broadcasted_iota(jnp.int32, (tiles.k, tiles.k), 0)
    c = lax.broadcasted_iota(jnp.int32, (tiles.k, tiles.k), 1)
    tri = jnp.where(c > r, 1.0, 0.0).astype(BF16)
    acc_ref[...] = jnp.zeros_like(acc_ref)
    later_ref[...] = jnp.zeros_like(later_ref)

    def scores(j, slot, q_lo):
        k = k_ref[_key_block(tiles, j), :]
        lo = q_lo or 0
        for h in range(2):
            z = _dot_nt(k, qh[h][lo:, :])
            if q_lo is not None:
                z = jnp.where(_hidden(tiles, i, j, lo, visible_when_equal=False), NEG_BIG, z)
                if lo:
                    z_ref[slot, h, :, :lo] = jnp.full((tiles.k, lo), NEG_BIG, F32)
            z_ref[slot, h, :, lo:] = z

    def within_block(j, slot, q_lo):
        lo = q_lo or 0
        for h in range(2):
            z = z_ref[slot, h, :, lo:]
            neg_abs = lax.bitcast_convert_type(
                lax.bitcast_convert_type(z, jnp.uint32) | jnp.uint32(0x80000000), F32)
            softplus = jnp.maximum(z, 0.0) + jnp.log(1.0 + jnp.exp(neg_abs))
            later = _dot(tri, softplus.astype(BF16))
            if lo:
                e_ref[slot, h, :, :lo] = jnp.full((tiles.k, lo), NEG_BIG, F32)
                total_ref[slot, h, :, :lo] = jnp.zeros((1, lo), F32)
            e_ref[slot, h, :, lo:] = (z - softplus) - later
            total_ref[slot, h, :, lo:] = later[0:1, :] + softplus[0:1, :]

    def accumulate(j, slot, q_lo):
        lo = q_lo or 0
        for h in range(2):
            w = jnp.exp(e_ref[slot, h, :, lo:])
            vt = vt_ref[h * SB_HEAD_DIM:(h + 1) * SB_HEAD_DIM, _key_block(tiles, j)]
            acc_ref[h, :, lo:] += _dot(vt, w.astype(BF16)) * jnp.exp(-later_ref[h, :, lo:])
            later_ref[h, :, lo:] += total_ref[slot, h, :, lo:]

    def finish():
        o_ref[...] = _merge_heads(acc_ref[0], acc_ref[1]).astype(o_ref.dtype)

    return [scores, within_block, accumulate], finish


def _sb_scratch(tiles):
    return (_tile_pair(tiles, F32) + _tile_pair(tiles, F32) + _slot_pair((2, 1, tiles.q), F32)
            + [pltpu.VMEM((2, SB_HEAD_DIM, tiles.q), F32), pltpu.VMEM((2, 1, tiles.q), F32)])


def _mla_stages(tiles, i, q_ref, k_ref, vt_ref, o_ref, s0, s1, acc_ref, m_ref):
    s_ref = _Slots(s0, s1)
    c2 = (MLA_NOPE_DIM + MLA_ROPE_DIM) ** -0.5 * LOG2E
    qh = [q_ref[:, h * MLA_QK_PAD:(h + 1) * MLA_QK_PAD] for h in range(2)]
    acc_ref[...] = jnp.zeros_like(acc_ref)
    m_ref[...] = jnp.full_like(m_ref, NEG_BIG)

    def scores(j, slot, q_lo):
        lo = q_lo or 0
        for h in range(2):
            s = _dot_nt(k_ref[_key_block(tiles, j), h * MLA_QK_PAD:(h + 1) * MLA_QK_PAD], qh[h][lo:, :])
            if q_lo is not None:
                s = jnp.where(_hidden(tiles, i, j, lo, visible_when_equal=True), NEG_BIG, s)
                if lo:
                    s_ref[slot, h, :, :lo] = jnp.full((tiles.k, lo), NEG_BIG, F32)
            s_ref[slot, h, :, lo:] = s

    def accumulate(j, slot, q_lo):
        lo = q_lo or 0
        for h in range(2):
            s = s_ref[slot, h, :, lo:]
            m = m_ref[h, :, lo:]
            m_new = jnp.maximum(m, jnp.max(s, axis=0, keepdims=True))
            p = jnp.exp2((s - m_new) * c2)
            rescale = jnp.exp2((m - m_new) * c2)
            m_ref[h, :, lo:] = m_new
            vt = vt_ref[h * _VT_PAD:h * _VT_PAD + _VT_ROWS, _key_block(tiles, j)]
            acc_ref[h, :, lo:] = rescale * acc_ref[h, :, lo:] + _dot(vt, p.astype(BF16))

    def finish():
        out = [acc_ref[h, :MLA_V_DIM, :] / acc_ref[h, MLA_V_DIM:MLA_V_DIM + 1, :] for h in range(2)]
        o_ref[...] = _merge_heads(*out).astype(o_ref.dtype)

    return [scores, accumulate], finish


def _mla_scratch(tiles):
    return _tile_pair(tiles, F32) + [pltpu.VMEM((2, _VT_ROWS, tiles.q), F32), pltpu.VMEM((2, 1, tiles.q), F32)]


def _attn_kernel(build_stages, tiles, q_ref, k_ref, vt_ref, o_ref, *scratch):
    def query_block(i, carry):
        rows = pl.ds(pl.multiple_of(i * tiles.q, tiles.q), tiles.q)
        stages, finish = build_stages(tiles, i, q_ref.at[rows], k_ref, vt_ref, o_ref.at[rows], *scratch)
        _sweep_keys(i, stages, tiles)
        finish()
        return carry

    lax.fori_loop(0, q_ref.shape[0] // tiles.q, query_block, 0)


def _attn_call(build_stages, scratch, name, q, qk_width, k, k_offset, v_t, vt_rows, batch, seq, width):
    tiles = ATTN_TILES
    return pl.pallas_call(
        functools.partial(_attn_kernel, build_stages, tiles),
        out_shape=jax.ShapeDtypeStruct((q.shape[0], width), BF16),
        grid=(batch, HEAD_PAIRS),
        in_specs=[pl.BlockSpec((seq, qk_width), lambda b, p: (b, p)),
                  pl.BlockSpec((seq, qk_width), lambda b, p: (b, k_offset + p)),
                  pl.BlockSpec((vt_rows, seq), lambda b, p: (p, b))],
        out_specs=pl.BlockSpec((seq, LANES), lambda b, p: (b, p)),
        scratch_shapes=scratch(tiles),
        compiler_params=_params("parallel", "parallel"),
        name=name,
    )(q, k, v_t)


def _sb_attn(qk, v_t, batch, seq):
    return _attn_call(_sb_stages, _sb_scratch, "sb_attn", qk, LANES, qk, HEAD_PAIRS, v_t, LANES,
                      batch, seq, SB_WIDTH)


def _mla_attn(q, k, v_t, batch, seq):
    return _attn_call(_mla_stages, _mla_scratch, "mla_attn", q, 2 * MLA_QK_PAD, k, 0, v_t, 2 * _VT_PAD,
                      batch, seq, MLA_WIDTH)


HALO = 8


def _conv_mixer_kernel(tiles_per_seq, x_ref, win_ref, cw_ref, wout_ref, g_ref, b_ref, o_ref, tail_ref):
    i = pl.program_id(0)
    x = x_ref[...]
    xb = x.astype(BF16)
    gate_b = _dot(xb, win_ref[:, :D_MODEL])
    u = _dot(xb, win_ref[:, D_MODEL:2 * D_MODEL]) * _dot(xb, win_ref[:, 2 * D_MODEL:])

    @pl.when(i % tiles_per_seq == 0)
    def _():
        tail_ref[...] = jnp.zeros_like(tail_ref)

    prev = tail_ref[...]
    tail_ref[...] = u[ROW_TILE - HALO:, :]
    ext = jnp.concatenate([prev, u], axis=0)
    cw = cw_ref[...]
    conv = (cw[0:1, :] * ext[HALO - 2:HALO - 2 + ROW_TILE, :]
            + cw[1:2, :] * ext[HALO - 1:HALO - 1 + ROW_TILE, :]
            + cw[2:3, :] * u)
    m = _dot((gate_b * conv).astype(BF16), wout_ref[...])
    o_ref[...] = _layer_norm(ALPHA * x + m, g_ref[...], b_ref[...])


def _conv_mixer(x, w_in, conv_w, w_out, g, b, seq):
    n = x.shape[0]
    row = pl.BlockSpec((ROW_TILE, D_MODEL), lambda i: (i, 0))
    return pl.pallas_call(
        functools.partial(_conv_mixer_kernel, seq // ROW_TILE),
        out_shape=jax.ShapeDtypeStruct((n, D_MODEL), F32),
        grid=(n // ROW_TILE,),
        in_specs=[row, _resident(w_in.shape), _resident(conv_w.shape), _resident(w_out.shape),
                  _resident(g.shape), _resident(b.shape)],
        out_specs=row,
        scratch_shapes=[pltpu.VMEM((HALO, D_MODEL), F32)],
        compiler_params=_params("arbitrary"),
        name="conv_mixer",
    )(x, w_in, conv_w, w_out, g, b)


def _rope_tables(seq):
    half = MLA_ROPE_DIM // 2
    inv_freq = ROPE_THETA ** (-jnp.arange(0, MLA_ROPE_DIM, 2, dtype=F32) / MLA_ROPE_DIM)
    ang = jnp.arange(seq, dtype=F32)[:, None] * inv_freq[None, :]
    cos, sin = jnp.cos(ang), jnp.sin(ang)
    ones = jnp.ones((seq, MLA_NOPE_DIM), F32)
    pad = jnp.zeros((seq, MLA_QK_PAD - MLA_NOPE_DIM - 2 * half), F32)
    cos_t = jnp.concatenate([ones, cos, cos, pad], axis=1)
    sin_t = jnp.concatenate([0.0 * ones, -sin, sin, pad], axis=1)
    return cos_t, sin_t


def _even_weights(w_in, w_uq, w_ukv):
    half = MLA_ROPE_DIM // 2
    qk = MLA_NOPE_DIM + MLA_ROPE_DIM
    zeros = lambda r, c: jnp.zeros((r, c), F32)
    sb_v = slice(_X_QK, _X_QK + SB_WIDTH)
    lat = slice(_X_QK + SB_WIDTH, _X_QK + SB_WIDTH + MLA_Q_RANK + MLA_KV_RANK)
    w_kr = w_in[:, lat.stop:]
    k1, k2 = w_kr[:, :half], w_kr[:, half:]
    nope_pad = zeros(D_MODEL, MLA_NOPE_DIM)
    tail_pad = zeros(D_MODEL, MLA_QK_PAD - qk)
    wx = jnp.concatenate([w_in[:, :_X_QK], w_in[:, lat], nope_pad, k1, k2, tail_pad,
                          nope_pad, k2, k1, tail_pad], axis=1)
    wsv = w_in[:, sb_v].T
    wq = w_uq.reshape(MLA_Q_RANK, MLA_HEADS, qk)
    q_nope, q1, q2 = wq[..., :MLA_NOPE_DIM], wq[..., MLA_NOPE_DIM:MLA_NOPE_DIM + half], wq[..., MLA_NOPE_DIM + half:]
    zq = lambda c: jnp.zeros((MLA_Q_RANK, MLA_HEADS, c), F32)
    wqa = jnp.concatenate([q_nope, q1, q2, zq(MLA_QK_PAD - qk)], axis=-1).reshape(MLA_Q_RANK, _QK_WIDTH)
    wqb = jnp.concatenate([zq(MLA_NOPE_DIM), q2, q1, zq(MLA_QK_PAD - qk)], axis=-1).reshape(MLA_Q_RANK, _QK_WIDTH)
    wkv = w_ukv.reshape(MLA_KV_RANK, MLA_HEADS, MLA_NOPE_DIM + MLA_V_DIM)
    wk = jnp.concatenate([wkv[..., :MLA_NOPE_DIM],
                          jnp.zeros((MLA_KV_RANK, MLA_HEADS, MLA_QK_PAD - MLA_NOPE_DIM), F32)],
                         axis=-1).reshape(MLA_KV_RANK, _QK_WIDTH)
    wv = jnp.concatenate([wkv[..., MLA_NOPE_DIM:],
                          jnp.zeros((MLA_KV_RANK, MLA_HEADS, _VT_PAD - MLA_V_DIM), F32)],
                         axis=-1).reshape(MLA_KV_RANK, MLA_HEADS * _VT_PAD).T
    return [w.astype(BF16) for w in (wx, wsv, wqa, wqb, wk, wv)]


def kernel(x, ln_g, ln_b, ffn_w_gate, ffn_w_up, ffn_w_down, mix_w_in, mla_q_norm_g, mla_w_uq,
           mla_kv_norm_g, mla_w_ukv, mix_w_out, conv_w_in, conv_w, conv_w_out):
    batch, seq, _ = x.shape
    assert seq % FFN_ROW_TILE == 0 and seq % ROW_TILE == 0
    for tiles in (ATTN_TILES,):
        assert seq % tiles.q == 0 and tiles.q % (2 * tiles.k) == 0
    cos_t, sin_t = _rope_tables(seq)
    h = x.reshape(batch * seq, D_MODEL)
    vec = lambda a: a.reshape(1, -1)
    for layer in range(DEPTH):
        j = layer // 2
        ln = lambda s: (vec(ln_g[layer, s]), vec(ln_b[layer, s]))
        ffn_w = lambda s: (ffn_w_gate[layer, s].astype(BF16), ffn_w_up[layer, s].astype(BF16),
                           ffn_w_down[layer, s].astype(BF16))
        h = _ffn(h, *ffn_w(0), *ln(0))
        if layer % 2 == 0:
            wx, wsv, wqa, wqb, wk, wv = _even_weights(mix_w_in[j], mla_w_uq[j], mla_w_ukv[j])
            sb_qk, sb_vt, q, k, v_t = _proj_even(h, wx, wsv, vec(mla_q_norm_g[j]), wqa, wqb,
                                                 vec(mla_kv_norm_g[j]), wk, wv, cos_t, sin_t, seq)
            out_sb = _sb_attn(sb_qk, sb_vt, batch, seq)
            out_mla = _mla_attn(q, k, v_t, batch, seq)
            h = _attn_out_ffn(h, out_sb, out_mla, mix_w_out[j].astype(BF16), *ln(1), *ffn_w(1), *ln(2))
        else:
            h = _conv_mixer(h, conv_w_in[j].astype(BF16), conv_w[j], conv_w_out[j].astype(BF16), *ln(1), seq)
            h = _ffn(h, *ffn_w(1), *ln(2))
    return h.reshape(batch, seq, D_MODEL)
```

```python
import functools
from typing import NamedTuple

import jax
import jax.numpy as jnp
from jax import lax
from jax.experimental import pallas as pl
from jax.experimental.pallas import tpu as pltpu

D_MODEL = 1024
DEPTH = 4
SB_HEADS = 8
SB_HEAD_DIM = 64
MLA_HEADS = 8
MLA_NOPE_DIM = 64
MLA_ROPE_DIM = 32
MLA_V_DIM = 64
MLA_Q_RANK = 768
MLA_KV_RANK = 256
ROPE_THETA = 10000.0
CONV_K = 3
D_FF = 2816
LN_EPS = 1e-5
RMS_EPS = 1e-6
ALPHA = (2 * DEPTH) ** 0.25
SB_WIDTH = SB_HEADS * SB_HEAD_DIM
MLA_WIDTH = MLA_HEADS * MLA_V_DIM

LANES = 128
HEAD_PAIRS = SB_HEADS // 2
MLA_QK_PAD = LANES
_VT_PAD = LANES
BF16_SUBLANES = 16
_VT_ROWS = MLA_V_DIM + BF16_SUBLANES
FF_CHUNK = 256
ROW_TILE = 512
FFN_ROW_TILE = 1024


class _Tiling(NamedTuple):
    q: int
    k: int

    @property
    def diag(self):
        return self.q // self.k


ATTN_TILES = _Tiling(q=512, k=256)
VMEM_LIMIT = 56 * 1024 * 1024
PAIRS_PER_TRIP = 2

BF16 = jnp.bfloat16
F32 = jnp.float32
NEG_BIG = -1e30
LOG2E = 1.4426950408889634


def _dot(a, b):
    return jnp.dot(a, b, preferred_element_type=F32)


def _dot_nt(a, b):
    return lax.dot_general(a, b, (((1,), (1,)), ((), ())), preferred_element_type=F32)


def _layer_norm(y, g, b):
    mu = jnp.mean(y, axis=-1, keepdims=True)
    d = y - mu
    var = jnp.mean(d * d, axis=-1, keepdims=True)
    return d * lax.rsqrt(var + LN_EPS) * g + b


def _rms_norm(c, g):
    return c * lax.rsqrt(jnp.mean(c * c, axis=-1, keepdims=True) + RMS_EPS) * g


def _params(*sem):
    return pltpu.CompilerParams(dimension_semantics=sem, vmem_limit_bytes=VMEM_LIMIT)


def _resident(shape):
    nd = len(shape)
    return pl.BlockSpec(shape, lambda *_: (0,) * nd, pipeline_mode=pl.Buffered(1))


def _half_step_ffn(x, wg_ref, wu_ref, wd_ref, g_ref, b_ref, h_ref):
    xb = x.astype(BF16)
    for c in range(D_FF // FF_CHUNK):
        sl = slice(c * FF_CHUNK, (c + 1) * FF_CHUNK)
        gate = _dot(xb, wg_ref[:, sl])
        up = _dot(xb, wu_ref[:, sl])
        h_ref[:, sl] = (gate * (1.0 / (1.0 + jnp.exp(-gate))) * up).astype(BF16)
    f = _dot(h_ref[...], wd_ref[...])
    return _layer_norm(ALPHA * x + 0.5 * f, g_ref[...], b_ref[...])


def _ffn_kernel(x_ref, wg_ref, wu_ref, wd_ref, g_ref, b_ref, o_ref, h_ref):
    o_ref[...] = _half_step_ffn(x_ref[...], wg_ref, wu_ref, wd_ref, g_ref, b_ref, h_ref)


def _attn_out_ffn_kernel(x_ref, sb_ref, mla_ref, wo_ref, go_ref, bo_ref,
                         wg_ref, wu_ref, wd_ref, g_ref, b_ref, o_ref, h_ref):
    m = _dot(sb_ref[...], wo_ref[:SB_WIDTH, :]) + _dot(mla_ref[...], wo_ref[SB_WIDTH:, :])
    x = _layer_norm(ALPHA * x_ref[...] + m, go_ref[...], bo_ref[...])
    o_ref[...] = _half_step_ffn(x, wg_ref, wu_ref, wd_ref, g_ref, b_ref, h_ref)


def _ffn_call(body, name, rows, residents):
    n = rows[0][0].shape[0]
    row = lambda w: pl.BlockSpec((FFN_ROW_TILE, w), lambda i: (i, 0))
    return pl.pallas_call(
        body,
        out_shape=jax.ShapeDtypeStruct((n, D_MODEL), F32),
        grid=(n // FFN_ROW_TILE,),
        in_specs=[row(w) for _, w in rows] + [_resident(a.shape) for a in residents],
        out_specs=row(D_MODEL),
        scratch_shapes=[pltpu.VMEM((FFN_ROW_TILE, D_FF), BF16)],
        compiler_params=_params("parallel"),
        name=name,
    )(*[a for a, _ in rows], *residents)


def _ffn(x, wg, wu, wd, g, b):
    return _ffn_call(_ffn_kernel, "ffn", [(x, D_MODEL)], [wg, wu, wd, g, b])


def _attn_out_ffn(x, sb, mla, wo, go, bo, wg, wu, wd, g, b):
    return _ffn_call(_attn_out_ffn_kernel, "attn_out_ffn",
                     [(x, D_MODEL), (sb, SB_WIDTH), (mla, MLA_WIDTH)], [wo, go, bo, wg, wu, wd, g, b])


_X_K = SB_WIDTH
_X_CQ = _X_K + MLA_Q_RANK
_X_CKV = _X_CQ + MLA_KV_RANK
_X_KRA = _X_CKV + LANES
_X_KRB = _X_KRA + LANES
_QK_WIDTH = MLA_HEADS * MLA_QK_PAD


def _proj_even_kernel(x_ref, wx_ref, wsqv_ref, gq_ref, wqa_ref, wqb_ref, gkv_ref, wk_ref, wv_ref,
                      cos_ref, sin_ref, cos_t_ref, sin_t_ref,
                      sbk_ref, sbq_ref, sbv_ref, q_ref, k_ref, v_ref):
    xb = x_ref[...].astype(BF16)
    qv = _dot_nt(wsqv_ref[...], xb)
    sbq_ref[...] = (qv[:SB_WIDTH, :] * (SB_HEAD_DIM ** -0.5)).astype(BF16)
    sbv_ref[...] = qv[SB_WIDTH:, :].astype(BF16)
    sbk_ref[...] = _dot(xb, wx_ref[:, :_X_K]).astype(BF16)
    cq = _rms_norm(_dot(xb, wx_ref[:, _X_K:_X_CQ]), gq_ref[...]).astype(BF16)
    cos_t = cos_t_ref[...]
    sin_t = sin_t_ref[...]
    for h in range(MLA_HEADS):
        sl = slice(h * MLA_QK_PAD, (h + 1) * MLA_QK_PAD)
        q_ref[sl, :] = (_dot_nt(wqa_ref[sl, :], cq) * cos_t + _dot_nt(wqb_ref[sl, :], cq) * sin_t).astype(BF16)
    ckv = _rms_norm(_dot(xb, wx_ref[:, _X_CQ:_X_CKV]), gkv_ref[...]).astype(BF16)
    kr = (_dot(xb, wx_ref[:, _X_CKV:_X_KRA]) * cos_ref[...]
          + _dot(xb, wx_ref[:, _X_KRA:_X_KRB]) * sin_ref[...])
    for h in range(MLA_HEADS):
        sl = slice(h * MLA_QK_PAD, (h + 1) * MLA_QK_PAD)
        k_ref[:, sl] = (_dot(ckv, wk_ref[:, sl]) + kr).astype(BF16)
    vt = _dot_nt(wv_ref[...], ckv)
    ones_row = lax.broadcasted_iota(jnp.int32, vt.shape, 0) % _VT_PAD == MLA_V_DIM
    v_ref[...] = jnp.where(ones_row, 1.0, vt).astype(BF16)


def _proj_even(x, wx, wsqv, gq, wqa, wqb, gkv, wk, wv, cos, sin, seq):
    n = x.shape[0]
    tiles_per_seq = seq // ROW_TILE
    row = lambda w: pl.BlockSpec((ROW_TILE, w), lambda i: (i, 0))
    col = lambda h: pl.BlockSpec((h, ROW_TILE), lambda i: (0, i))
    pos = pl.BlockSpec((ROW_TILE, LANES), lambda i: (i % tiles_per_seq, 0))
    pos_t = pl.BlockSpec((LANES, ROW_TILE), lambda i: (0, i % tiles_per_seq))
    rows = lambda w: jax.ShapeDtypeStruct((n, w), BF16)
    cols = lambda h: jax.ShapeDtypeStruct((h, n), BF16)
    return pl.pallas_call(
        _proj_even_kernel,
        out_shape=(rows(SB_WIDTH), cols(SB_WIDTH), cols(SB_WIDTH),
                   cols(_QK_WIDTH), rows(_QK_WIDTH), cols(MLA_HEADS * _VT_PAD)),
        grid=(n // ROW_TILE,),
        in_specs=[row(D_MODEL), _resident(wx.shape), _resident(wsqv.shape), _resident(gq.shape),
                  _resident(wqa.shape), _resident(wqb.shape), _resident(gkv.shape),
                  _resident(wk.shape), _resident(wv.shape), pos, pos, pos_t, pos_t],
        out_specs=(row(SB_WIDTH), col(SB_WIDTH), col(SB_WIDTH),
                   col(_QK_WIDTH), row(_QK_WIDTH), col(MLA_HEADS * _VT_PAD)),
        compiler_params=_params("parallel"),
        name="proj_even",
    )(x, wx, wsqv, gq, wqa, wqb, gkv, wk, wv, cos, sin, cos.T, sin.T)


def _sweep_keys(i, stages, tiles):
    depth = len(stages)
    blocks = (i + 1) * tiles.diag
    lead = [depth - 1 - s for s in range(depth)]
    peel = max(0, tiles.diag - depth + 1)
    for n in range(1 - depth, peel):
        for s, stage in enumerate(stages):
            b = n + lead[s]
            if b >= 0:
                stage(blocks - 1 - b, b % 2, (tiles.diag - 1 - b) * tiles.k if b < tiles.diag else None)

    def step_pair(t):
        for u in range(2):
            for s in range(depth):
                b_off = peel + u + lead[s]
                stages[s](blocks - 1 - b_off - 2 * t, b_off % 2, None)

    def body(t, carry):
        for r in range(PAIRS_PER_TRIP):
            step_pair(PAIRS_PER_TRIP * t + r)
        return carry

    pairs = i * (tiles.diag // 2)
    lax.fori_loop(0, pairs // PAIRS_PER_TRIP, body, 0)
    if (tiles.diag // 2) % PAIRS_PER_TRIP:
        for r in range(PAIRS_PER_TRIP - 1):
            pl.when(r < pairs % PAIRS_PER_TRIP)(
                functools.partial(step_pair, pairs - pairs % PAIRS_PER_TRIP + r))
    for e in range(depth - 1):
        for s, stage in enumerate(stages):
            j = depth - 2 - e - lead[s]
            if j >= 0:
                stage(j, (j + 1) % 2, None)


def _hidden(tiles, i, j, q_lo, visible_when_equal):
    shape = (tiles.k, tiles.q - q_lo)
    key = j * tiles.k + lax.broadcasted_iota(jnp.int32, shape, 0)
    qry = i * tiles.q + q_lo + lax.broadcasted_iota(jnp.int32, shape, 1)
    return key > qry if visible_when_equal else key >= qry


def _key_block(tiles, j):
    return pl.ds(pl.multiple_of(j * tiles.k, tiles.k), tiles.k)


def _merge_heads(o0, o1):
    return jnp.concatenate([o0, o1], axis=0).T


class _Slots:
    def __init__(self, first, second):
        self._refs = (first, second)

    def __getitem__(self, idx):
        return self._refs[idx[0]][idx[1:]]

    def __setitem__(self, idx, value):
        self._refs[idx[0]][idx[1:]] = value


def _slot_pair(shape, dtype):
    return [pltpu.VMEM(shape, dtype)] * 2


def _tile_pair(tiles, dtype):
    return _slot_pair((2, tiles.k, tiles.q), dtype)


def _sb_stages(tiles, i, qt_ref, k_ref, vt_ref, o_ref, z0, z1, e0, e1, total0, total1, acc_ref, later_ref):
    z_ref, e_ref, total_ref = _Slots(z0, z1), _Slots(e0, e1), _Slots(total0, total1)
    qt = qt_ref[...]
    dim = lax.broadcasted_iota(jnp.int32, (LANES, tiles.q), 0)
    qh = [jnp.where(dim < SB_HEAD_DIM, qt, jnp.zeros_like(qt)),
          jnp.where(dim >= SB_HEAD_DIM, qt, jnp.zeros_like(qt))]
    r = lax.broadcasted_iota(jnp.int32, (tiles.k, tiles.k), 0)
    c = lax.broadcasted_iota(jnp.int32, (tiles.k, tiles.k), 1)
    tri = jnp.where(c > r, 1.0, 0.0).astype(BF16)
    acc_ref[...] = jnp.zeros_like(acc_ref)
    later_ref[...] = jnp.zeros_like(later_ref)

    def scores(j, slot, q_lo):
        k = k_ref[_key_block(tiles, j), :]
        lo = q_lo or 0
        for h in range(2):
            z = _dot(k, qh[h][:, lo:])
            if q_lo is not None:
                z = jnp.where(_hidden(tiles, i, j, lo, visible_when_equal=False), NEG_BIG, z)
                if lo:
                    z_ref[slot, h, :, :lo] = jnp.full((tiles.k, lo), NEG_BIG, F32)
            z_ref[slot, h, :, lo:] = z

    def within_block(j, slot, q_lo):
        lo = q_lo or 0
        for h in range(2):
            z = z_ref[slot, h, :, lo:]
            neg_abs = lax.bitcast_convert_type(
                lax.bitcast_convert_type(z, jnp.uint32) | jnp.uint32(0x80000000), F32)
            softplus = jnp.maximum(z, 0.0) + jnp.log(1.0 + jnp.exp(neg_abs))
            later = _dot(tri, softplus.astype(BF16))
            if lo:
                e_ref[slot, h, :, :lo] = jnp.full((tiles.k, lo), NEG_BIG, F32)
                total_ref[slot, h, :, :lo] = jnp.zeros((1, lo), F32)
            e_ref[slot, h, :, lo:] = (z - softplus) - later
            total_ref[slot, h, :, lo:] = later[0:1, :] + softplus[0:1, :]

    def accumulate(j, slot, q_lo):
        lo = q_lo or 0
        for h in range(2):
            w = jnp.exp(e_ref[slot, h, :, lo:])
            vt = vt_ref[h * SB_HEAD_DIM:(h + 1) * SB_HEAD_DIM, _key_block(tiles, j)]
            acc_ref[h, :, lo:] += _dot(vt, w.astype(BF16)) * jnp.exp(-later_ref[h, :, lo:])
            later_ref[h, :, lo:] += total_ref[slot, h, :, lo:]

    def finish():
        o_ref[...] = _merge_heads(acc_ref[0], acc_ref[1]).astype(o_ref.dtype)

    return [scores, within_block, accumulate], finish


def _sb_scratch(tiles):
    return (_tile_pair(tiles, F32) + _tile_pair(tiles, F32) + _slot_pair((2, 1, tiles.q), F32)
            + [pltpu.VMEM((2, SB_HEAD_DIM, tiles.q), F32), pltpu.VMEM((2, 1, tiles.q), F32)])


def _mla_stages(tiles, i, qt_ref, k_ref, vt_ref, o_ref, s0, s1, acc_ref, m_ref):
    s_ref = _Slots(s0, s1)
    c2 = (MLA_NOPE_DIM + MLA_ROPE_DIM) ** -0.5 * LOG2E
    qh = [qt_ref[h * MLA_QK_PAD:(h + 1) * MLA_QK_PAD, :] for h in range(2)]
    acc_ref[...] = jnp.zeros_like(acc_ref)
    m_ref[...] = jnp.full_like(m_ref, NEG_BIG)

    def scores(j, slot, q_lo):
        lo = q_lo or 0
        for h in range(2):
            s = _dot(k_ref[_key_block(tiles, j), h * MLA_QK_PAD:(h + 1) * MLA_QK_PAD], qh[h][:, lo:])
            if q_lo is not None:
                s = jnp.where(_hidden(tiles, i, j, lo, visible_when_equal=True), NEG_BIG, s)
                if lo:
                    s_ref[slot, h, :, :lo] = jnp.full((tiles.k, lo), NEG_BIG, F32)
            s_ref[slot, h, :, lo:] = s

    def accumulate(j, slot, q_lo):
        lo = q_lo or 0
        for h in range(2):
            s = s_ref[slot, h, :, lo:]
            m = m_ref[h, :, lo:]
            m_new = jnp.maximum(m, jnp.max(s, axis=0, keepdims=True))
            p = jnp.exp2((s - m_new) * c2)
            rescale = jnp.exp2((m - m_new) * c2)
            m_ref[h, :, lo:] = m_new
            vt = vt_ref[h * _VT_PAD:h * _VT_PAD + _VT_ROWS, _key_block(tiles, j)]
            acc_ref[h, :, lo:] = rescale * acc_ref[h, :, lo:] + _dot(vt, p.astype(BF16))

    def finish():
        out = [acc_ref[h, :MLA_V_DIM, :] / acc_ref[h, MLA_V_DIM:MLA_V_DIM + 1, :] for h in range(2)]
        o_ref[...] = _merge_heads(*out).astype(o_ref.dtype)

    return [scores, accumulate], finish


def _mla_scratch(tiles):
    return _tile_pair(tiles, F32) + [pltpu.VMEM((2, _VT_ROWS, tiles.q), F32), pltpu.VMEM((2, 1, tiles.q), F32)]


def _attn_kernel(build_stages, tiles, qt_ref, k_ref, vt_ref, o_ref, *scratch):
    def query_block(i, carry):
        block = pl.ds(pl.multiple_of(i * tiles.q, tiles.q), tiles.q)
        stages, finish = build_stages(tiles, i, qt_ref.at[:, block], k_ref, vt_ref, o_ref.at[block], *scratch)
        _sweep_keys(i, stages, tiles)
        finish()
        return carry

    lax.fori_loop(0, qt_ref.shape[1] // tiles.q, query_block, 0)


def _attn_call(build_stages, scratch, name, q_t, k, v_t, qk_width, vt_rows, batch, seq, width):
    tiles = ATTN_TILES
    return pl.pallas_call(
        functools.partial(_attn_kernel, build_stages, tiles),
        out_shape=jax.ShapeDtypeStruct((k.shape[0], width), BF16),
        grid=(batch, HEAD_PAIRS),
        in_specs=[pl.BlockSpec((qk_width, seq), lambda b, p: (p, b)),
                  pl.BlockSpec((seq, qk_width), lambda b, p: (b, p)),
                  pl.BlockSpec((vt_rows, seq), lambda b, p: (p, b))],
        out_specs=pl.BlockSpec((seq, LANES), lambda b, p: (b, p)),
        scratch_shapes=scratch(tiles),
        compiler_params=_params("parallel", "parallel"),
        name=name,
    )(q_t, k, v_t)


def _sb_attn(q_t, k, v_t, batch, seq):
    return _attn_call(_sb_stages, _sb_scratch, "sb_attn", q_t, k, v_t, LANES, LANES, batch, seq, SB_WIDTH)


def _mla_attn(q_t, k, v_t, batch, seq):
    return _attn_call(_mla_stages, _mla_scratch, "mla_attn", q_t, k, v_t, 2 * MLA_QK_PAD, 2 * _VT_PAD,
                      batch, seq, MLA_WIDTH)


HALO = 8


def _conv_mixer_kernel(tiles_per_seq, x_ref, win_ref, cw_ref, wout_ref, g_ref, b_ref, o_ref, tail_ref):
    i = pl.program_id(0)
    x = x_ref[...]
    xb = x.astype(BF16)
    gate_b = _dot(xb, win_ref[:, :D_MODEL])
    u = _dot(xb, win_ref[:, D_MODEL:2 * D_MODEL]) * _dot(xb, win_ref[:, 2 * D_MODEL:])

    @pl.when(i % tiles_per_seq == 0)
    def _():
        tail_ref[...] = jnp.zeros_like(tail_ref)

    prev = tail_ref[...]
    tail_ref[...] = u[ROW_TILE - HALO:, :]
    ext = jnp.concatenate([prev, u], axis=0)
    cw = cw_ref[...]
    conv = (cw[0:1, :] * ext[HALO - 2:HALO - 2 + ROW_TILE, :]
            + cw[1:2, :] * ext[HALO - 1:HALO - 1 + ROW_TILE, :]
            + cw[2:3, :] * u)
    m = _dot((gate_b * conv).astype(BF16), wout_ref[...])
    o_ref[...] = _layer_norm(ALPHA * x + m, g_ref[...], b_ref[...])


def _conv_mixer(x, w_in, conv_w, w_out, g, b, seq):
    n = x.shape[0]
    row = pl.BlockSpec((ROW_TILE, D_MODEL), lambda i: (i, 0))
    return pl.pallas_call(
        functools.partial(_conv_mixer_kernel, seq // ROW_TILE),
        out_shape=jax.ShapeDtypeStruct((n, D_MODEL), F32),
        grid=(n // ROW_TILE,),
        in_specs=[row, _resident(w_in.shape), _resident(conv_w.shape), _resident(w_out.shape),
                  _resident(g.shape), _resident(b.shape)],
        out_specs=row,
        scratch_shapes=[pltpu.VMEM((HALO, D_MODEL), F32)],
        compiler_params=_params("arbitrary"),
        name="conv_mixer",
    )(x, w_in, conv_w, w_out, g, b)


def _rope_tables(seq):
    half = MLA_ROPE_DIM // 2
    inv_freq = ROPE_THETA ** (-jnp.arange(0, MLA_ROPE_DIM, 2, dtype=F32) / MLA_ROPE_DIM)
    ang = jnp.arange(seq, dtype=F32)[:, None] * inv_freq[None, :]
    cos, sin = jnp.cos(ang), jnp.sin(ang)
    ones = jnp.ones((seq, MLA_NOPE_DIM), F32)
    pad = jnp.zeros((seq, MLA_QK_PAD - MLA_NOPE_DIM - 2 * half), F32)
    cos_t = jnp.concatenate([ones, cos, cos, pad], axis=1)
    sin_t = jnp.concatenate([0.0 * ones, -sin, sin, pad], axis=1)
    return cos_t, sin_t


def _even_weights(w_in, w_uq, w_ukv):
    half = MLA_ROPE_DIM // 2
    qk = MLA_NOPE_DIM + MLA_ROPE_DIM
    zeros = lambda r, c: jnp.zeros((r, c), F32)
    sb_q, sb_k, sb_v = (slice(g * SB_WIDTH, (g + 1) * SB_WIDTH) for g in range(3))
    lat = slice(3 * SB_WIDTH, 3 * SB_WIDTH + MLA_Q_RANK + MLA_KV_RANK)
    w_kr = w_in[:, lat.stop:]
    k1, k2 = w_kr[:, :half], w_kr[:, half:]
    nope_pad = zeros(D_MODEL, MLA_NOPE_DIM)
    tail_pad = zeros(D_MODEL, MLA_QK_PAD - qk)
    wx = jnp.concatenate([w_in[:, sb_k], w_in[:, lat], nope_pad, k1, k2, tail_pad,
                          nope_pad, k2, k1, tail_pad], axis=1)
    wsqv = jnp.concatenate([w_in[:, sb_q], w_in[:, sb_v]], axis=1).T
    wq = w_uq.reshape(MLA_Q_RANK, MLA_HEADS, qk)
    q_nope, q1, q2 = wq[..., :MLA_NOPE_DIM], wq[..., MLA_NOPE_DIM:MLA_NOPE_DIM + half], wq[..., MLA_NOPE_DIM + half:]
    zq = lambda c: jnp.zeros((MLA_Q_RANK, MLA_HEADS, c), F32)
    wqa = jnp.concatenate([q_nope, q1, q2, zq(MLA_QK_PAD - qk)], axis=-1).reshape(MLA_Q_RANK, _QK_WIDTH)
    wqb = jnp.concatenate([zq(MLA_NOPE_DIM), q2, q1, zq(MLA_QK_PAD - qk)], axis=-1).reshape(MLA_Q_RANK, _QK_WIDTH)
    wkv = w_ukv.reshape(MLA_KV_RANK, MLA_HEADS, MLA_NOPE_DIM + MLA_V_DIM)
    wk = jnp.concatenate([wkv[..., :MLA_NOPE_DIM],
                          jnp.zeros((MLA_KV_RANK, MLA_HEADS, MLA_QK_PAD - MLA_NOPE_DIM), F32)],
                         axis=-1).reshape(MLA_KV_RANK, _QK_WIDTH)
    wv = jnp.concatenate([wkv[..., MLA_NOPE_DIM:],
                          jnp.zeros((MLA_KV_RANK, MLA_HEADS, _VT_PAD - MLA_V_DIM), F32)],
                         axis=-1).reshape(MLA_KV_RANK, MLA_HEADS * _VT_PAD).T
    return [w.astype(BF16) for w in (wx, wsqv, wqa.T, wqb.T, wk, wv)]


def kernel(x, ln_g, ln_b, ffn_w_gate, ffn_w_up, ffn_w_down, mix_w_in, mla_q_norm_g, mla_w_uq,
           mla_kv_norm_g, mla_w_ukv, mix_w_out, conv_w_in, conv_w, conv_w_out):
    batch, seq, _ = x.shape
    assert seq % FFN_ROW_TILE == 0 and seq % ROW_TILE == 0
    for tiles in (ATTN_TILES,):
        assert seq % tiles.q == 0 and tiles.q % (2 * tiles.k) == 0
    cos_t, sin_t = _rope_tables(seq)
    h = x.reshape(batch * seq, D_MODEL)
    vec = lambda a: a.reshape(1, -1)
    for layer in range(DEPTH):
        j = layer // 2
        ln = lambda s: (vec(ln_g[layer, s]), vec(ln_b[layer, s]))
        ffn_w = lambda s: (ffn_w_gate[layer, s].astype(BF16), ffn_w_up[layer, s].astype(BF16),
                           ffn_w_down[layer, s].astype(BF16))
        h = _ffn(h, *ffn_w(0), *ln(0))
        if layer % 2 == 0:
            wx, wsqv, wqa, wqb, wk, wv = _even_weights(mix_w_in[j], mla_w_uq[j], mla_w_ukv[j])
            sb_k, sb_qt, sb_vt, q_t, k, v_t = _proj_even(h, wx, wsqv, vec(mla_q_norm_g[j]), wqa, wqb,
                                                         vec(mla_kv_norm_g[j]), wk, wv, cos_t, sin_t, seq)
            out_sb = _sb_attn(sb_qt, sb_k, sb_vt, batch, seq)
            out_mla = _mla_attn(q_t, k, v_t, batch, seq)
            h = _attn_out_ffn(h, out_sb, out_mla, mix_w_out[j].astype(BF16), *ln(1), *ffn_w(1), *ln(2))
        else:
            h = _conv_mixer(h, conv_w_in[j].astype(BF16), conv_w[j], conv_w_out[j].astype(BF16), *ln(1), seq)
            h = _ffn(h, *ffn_w(1), *ln(2))
    return h.reshape(batch, seq, D_MODEL)
```

```python
import functools
from typing import NamedTuple

import jax
import jax.numpy as jnp
from jax import lax
from jax.experimental import pallas as pl
from jax.experimental.pallas import tpu as pltpu

D_MODEL = 1024
DEPTH = 4
SB_HEADS = 8
SB_HEAD_DIM = 64
MLA_HEADS = 8
MLA_NOPE_DIM = 64
MLA_ROPE_DIM = 32
MLA_V_DIM = 64
MLA_Q_RANK = 768
MLA_KV_RANK = 256
ROPE_THETA = 10000.0
CONV_K = 3
D_FF = 2816
LN_EPS = 1e-5
RMS_EPS = 1e-6
ALPHA = (2 * DEPTH) ** 0.25
SB_WIDTH = SB_HEADS * SB_HEAD_DIM
MLA_WIDTH = MLA_HEADS * MLA_V_DIM

LANES = 128
HEAD_PAIRS = SB_HEADS // 2
MLA_QK_PAD = LANES
_VT_PAD = LANES
BF16_SUBLANES = 16
_VT_ROWS = MLA_V_DIM + BF16_SUBLANES
FF_CHUNK = 256
ROW_TILE = 512
FFN_ROW_TILE = 1024


class _Tiling(NamedTuple):
    q: int
    k: int

    @property
    def diag(self):
        return self.q // self.k


ATTN_TILES = _Tiling(q=512, k=256)
VMEM_LIMIT = 56 * 1024 * 1024
PAIRS_PER_TRIP = 2

BF16 = jnp.bfloat16
F32 = jnp.float32
NEG_BIG = -1e30
LOG2E = 1.4426950408889634


def _dot(a, b):
    return jnp.dot(a, b, preferred_element_type=F32)


def _dot_nt(a, b):
    return lax.dot_general(a, b, (((1,), (1,)), ((), ())), preferred_element_type=F32)


def _layer_norm(y, g, b):
    mu = jnp.mean(y, axis=-1, keepdims=True)
    d = y - mu
    var = jnp.mean(d * d, axis=-1, keepdims=True)
    return d * lax.rsqrt(var + LN_EPS) * g + b


def _rms_norm(c, g):
    return c * lax.rsqrt(jnp.mean(c * c, axis=-1, keepdims=True) + RMS_EPS) * g


def _params(*sem):
    return pltpu.CompilerParams(dimension_semantics=sem, vmem_limit_bytes=VMEM_LIMIT)


def _resident(shape):
    nd = len(shape)
    return pl.BlockSpec(shape, lambda *_: (0,) * nd, pipeline_mode=pl.Buffered(1))


def _half_step_ffn(x, wg_ref, wu_ref, wd_ref, g_ref, b_ref, h_ref):
    xb = x.astype(BF16)
    for c in range(D_FF // FF_CHUNK):
        sl = slice(c * FF_CHUNK, (c + 1) * FF_CHUNK)
        gate = _dot(xb, wg_ref[:, sl])
        up = _dot(xb, wu_ref[:, sl])
        h_ref[:, sl] = (gate * (1.0 / (1.0 + jnp.exp(-gate))) * up).astype(BF16)
    f = _dot(h_ref[...], wd_ref[...])
    return _layer_norm(ALPHA * x + 0.5 * f, g_ref[...], b_ref[...])


def _ffn_kernel(x_ref, wg_ref, wu_ref, wd_ref, g_ref, b_ref, o_ref, h_ref):
    o_ref[...] = _half_step_ffn(x_ref[...], wg_ref, wu_ref, wd_ref, g_ref, b_ref, h_ref)


def _attn_out_ffn_kernel(x_ref, sb_ref, mla_ref, wo_ref, go_ref, bo_ref,
                         wg_ref, wu_ref, wd_ref, g_ref, b_ref, o_ref, h_ref):
    m = _dot(sb_ref[...], wo_ref[:SB_WIDTH, :]) + _dot(mla_ref[...], wo_ref[SB_WIDTH:, :])
    x = _layer_norm(ALPHA * x_ref[...] + m, go_ref[...], bo_ref[...])
    o_ref[...] = _half_step_ffn(x, wg_ref, wu_ref, wd_ref, g_ref, b_ref, h_ref)


def _ffn_call(body, name, rows, residents):
    n = rows[0][0].shape[0]
    row = lambda w: pl.BlockSpec((FFN_ROW_TILE, w), lambda i: (i, 0))
    return pl.pallas_call(
        body,
        out_shape=jax.ShapeDtypeStruct((n, D_MODEL), F32),
        grid=(n // FFN_ROW_TILE,),
        in_specs=[row(w) for _, w in rows] + [_resident(a.shape) for a in residents],
        out_specs=row(D_MODEL),
        scratch_shapes=[pltpu.VMEM((FFN_ROW_TILE, D_FF), BF16)],
        compiler_params=_params("parallel"),
        name=name,
    )(*[a for a, _ in rows], *residents)


def _ffn(x, wg, wu, wd, g, b):
    return _ffn_call(_ffn_kernel, "ffn", [(x, D_MODEL)], [wg, wu, wd, g, b])


def _attn_out_ffn(x, sb, mla, wo, go, bo, wg, wu, wd, g, b):
    return _ffn_call(_attn_out_ffn_kernel, "attn_out_ffn",
                     [(x, D_MODEL), (sb, SB_WIDTH), (mla, MLA_WIDTH)], [wo, go, bo, wg, wu, wd, g, b])


_X_K = SB_WIDTH
_X_CQ = _X_K + MLA_Q_RANK
_X_CKV = _X_CQ + MLA_KV_RANK
_X_KRA = _X_CKV + LANES
_X_KRB = _X_KRA + LANES
_QK_WIDTH = MLA_HEADS * MLA_QK_PAD


def _proj_even_kernel(x_ref, wx_ref, wsqv_ref, gq_ref, wqa_ref, wqb_ref, gkv_ref, wk_ref, wv_ref,
                      cos_ref, sin_ref, cos_t_ref, sin_t_ref,
                      sbk_ref, sbq_ref, sbv_ref, q_ref, k_ref, v_ref):
    xb = x_ref[...].astype(BF16)
    qv = _dot_nt(wsqv_ref[...], xb)
    sbq_ref[...] = (qv[:SB_WIDTH, :] * (SB_HEAD_DIM ** -0.5)).astype(BF16)
    sbv_ref[...] = qv[SB_WIDTH:, :].astype(BF16)
    sbk_ref[...] = _dot(xb, wx_ref[:, :_X_K]).astype(BF16)
    cq = _rms_norm(_dot(xb, wx_ref[:, _X_K:_X_CQ]), gq_ref[...]).astype(BF16)
    cos_t = cos_t_ref[...]
    sin_t = sin_t_ref[...]
    qa = _dot_nt(wqa_ref[...], cq)
    qb = _dot_nt(wqb_ref[...], cq)
    for h in range(MLA_HEADS):
        sl = slice(h * MLA_QK_PAD, (h + 1) * MLA_QK_PAD)
        q_ref[sl, :] = (qa[sl, :] * cos_t + qb[sl, :] * sin_t).astype(BF16)
    ckv = _rms_norm(_dot(xb, wx_ref[:, _X_CQ:_X_CKV]), gkv_ref[...]).astype(BF16)
    kr = (_dot(xb, wx_ref[:, _X_CKV:_X_KRA]) * cos_ref[...]
          + _dot(xb, wx_ref[:, _X_KRA:_X_KRB]) * sin_ref[...])
    for h in range(MLA_HEADS):
        sl = slice(h * MLA_QK_PAD, (h + 1) * MLA_QK_PAD)
        k_ref[:, sl] = (_dot(ckv, wk_ref[:, sl]) + kr).astype(BF16)
    vt = _dot_nt(wv_ref[...], ckv)
    ones_row = lax.broadcasted_iota(jnp.int32, vt.shape, 0) % _VT_PAD == MLA_V_DIM
    v_ref[...] = jnp.where(ones_row, 1.0, vt).astype(BF16)


def _proj_even(x, wx, wsqv, gq, wqa, wqb, gkv, wk, wv, cos, sin, seq):
    n = x.shape[0]
    tiles_per_seq = seq // ROW_TILE
    row = lambda w: pl.BlockSpec((ROW_TILE, w), lambda i: (i, 0))
    col = lambda h: pl.BlockSpec((h, ROW_TILE), lambda i: (0, i))
    pos = pl.BlockSpec((ROW_TILE, LANES), lambda i: (i % tiles_per_seq, 0))
    pos_t = pl.BlockSpec((LANES, ROW_TILE), lambda i: (0, i % tiles_per_seq))
    rows = lambda w: jax.ShapeDtypeStruct((n, w), BF16)
    cols = lambda h: jax.ShapeDtypeStruct((h, n), BF16)
    return pl.pallas_call(
        _proj_even_kernel,
        out_shape=(rows(SB_WIDTH), cols(SB_WIDTH), cols(SB_WIDTH),
                   cols(_QK_WIDTH), rows(_QK_WIDTH), cols(MLA_HEADS * _VT_PAD)),
        grid=(n // ROW_TILE,),
        in_specs=[row(D_MODEL), _resident(wx.shape), _resident(wsqv.shape), _resident(gq.shape),
                  _resident(wqa.shape), _resident(wqb.shape), _resident(gkv.shape),
                  _resident(wk.shape), _resident(wv.shape), pos, pos, pos_t, pos_t],
        out_specs=(row(SB_WIDTH), col(SB_WIDTH), col(SB_WIDTH),
                   col(_QK_WIDTH), row(_QK_WIDTH), col(MLA_HEADS * _VT_PAD)),
        compiler_params=_params("parallel"),
        name="proj_even",
    )(x, wx, wsqv, gq, wqa, wqb, gkv, wk, wv, cos, sin, cos.T, sin.T)


def _sweep_keys(i, stages, tiles):
    depth = len(stages)
    blocks = (i + 1) * tiles.diag
    lead = [depth - 1 - s for s in range(depth)]
    peel = max(0, tiles.diag - depth + 1)
    for n in range(1 - depth, peel):
        for s, stage in enumerate(stages):
            b = n + lead[s]
            if b >= 0:
                stage(blocks - 1 - b, b % 2, (tiles.diag - 1 - b) * tiles.k if b < tiles.diag else None)

    def step_pair(t):
        for u in range(2):
            for s in range(depth):
                b_off = peel + u + lead[s]
                stages[s](blocks - 1 - b_off - 2 * t, b_off % 2, None)

    def body(t, carry):
        for r in range(PAIRS_PER_TRIP):
            step_pair(PAIRS_PER_TRIP * t + r)
        return carry

    pairs = i * (tiles.diag // 2)
    lax.fori_loop(0, pairs // PAIRS_PER_TRIP, body, 0)
    if (tiles.diag // 2) % PAIRS_PER_TRIP:
        for r in range(PAIRS_PER_TRIP - 1):
            pl.when(r < pairs % PAIRS_PER_TRIP)(
                functools.partial(step_pair, pairs - pairs % PAIRS_PER_TRIP + r))
    for e in range(depth - 1):
        for s, stage in enumerate(stages):
            j = depth - 2 - e - lead[s]
            if j >= 0:
                stage(j, (j + 1) % 2, None)


def _hidden(tiles, i, j, q_lo, visible_when_equal):
    shape = (tiles.k, tiles.q - q_lo)
    key = j * tiles.k + lax.broadcasted_iota(jnp.int32, shape, 0)
    qry = i * tiles.q + q_lo + lax.broadcasted_iota(jnp.int32, shape, 1)
    return key > qry if visible_when_equal else key >= qry


def _key_block(tiles, j):
    return pl.ds(pl.multiple_of(j * tiles.k, tiles.k), tiles.k)


def _merge_heads(o0, o1):
    return jnp.concatenate([o0, o1], axis=0).T


class _Slots:
    def __init__(self, first, second):
        self._refs = (first, second)

    def __getitem__(self, idx):
        return self._refs[idx[0]][idx[1:]]

    def __setitem__(self, idx, value):
        self._refs[idx[0]][idx[1:]] = value


def _slot_pair(shape, dtype):
    return [pltpu.VMEM(shape, dtype)] * 2


def _tile_pair(tiles, dtype):
    return _slot_pair((2, tiles.k, tiles.q), dtype)


def _sb_stages(tiles, i, qt_ref, k_ref, vt_ref, o_ref, z0, z1, e0, e1, total0, total1, acc_ref, later_ref):
    z_ref, e_ref, total_ref = _Slots(z0, z1), _Slots(e0, e1), _Slots(total0, total1)
    qt = qt_ref[...]
    dim = lax.broadcasted_iota(jnp.int32, (LANES, tiles.q), 0)
    qh = [jnp.where(dim < SB_HEAD_DIM, qt, jnp.zeros_like(qt)),
          jnp.where(dim >= SB_HEAD_DIM, qt, jnp.zeros_like(qt))]
    r = lax.broadcasted_iota(jnp.int32, (tiles.k, tiles.k), 0)
    c = lax.broadcasted_iota(jnp.int32, (tiles.k, tiles.k), 1)
    tri = jnp.where(c > r, 1.0, 0.0).astype(BF16)
    acc_ref[...] = jnp.zeros_like(acc_ref)
    later_ref[...] = jnp.zeros_like(later_ref)

    def scores(j, slot, q_lo):
        k = k_ref[_key_block(tiles, j), :]
        lo = q_lo or 0
        for h in range(2):
            z = _dot(k, qh[h][:, lo:])
            if q_lo is not None:
                z = jnp.where(_hidden(tiles, i, j, lo, visible_when_equal=False), NEG_BIG, z)
                if lo:
                    z_ref[slot, h, :, :lo] = jnp.full((tiles.k, lo), NEG_BIG, F32)
            z_ref[slot, h, :, lo:] = z

    def within_block(j, slot, q_lo):
        lo = q_lo or 0
        for h in range(2):
            z = z_ref[slot, h, :, lo:]
            neg_abs = lax.bitcast_convert_type(
                lax.bitcast_convert_type(z, jnp.uint32) | jnp.uint32(0x80000000), F32)
            softplus = jnp.maximum(z, 0.0) + jnp.log(1.0 + jnp.exp(neg_abs))
            later = _dot(tri, softplus.astype(BF16))
            if lo:
                e_ref[slot, h, :, :lo] = jnp.full((tiles.k, lo), NEG_BIG, F32)
                total_ref[slot, h, :, :lo] = jnp.zeros((1, lo), F32)
            e_ref[slot, h, :, lo:] = (z - softplus) - later
            total_ref[slot, h, :, lo:] = later[0:1, :] + softplus[0:1, :]

    def accumulate(j, slot, q_lo):
        lo = q_lo or 0
        for h in range(2):
            w = jnp.exp(e_ref[slot, h, :, lo:])
            vt = vt_ref[h * SB_HEAD_DIM:(h + 1) * SB_HEAD_DIM, _key_block(tiles, j)]
            acc_ref[h, :, lo:] += _dot(vt, w.astype(BF16)) * jnp.exp(-later_ref[h, :, lo:])
            later_ref[h, :, lo:] += total_ref[slot, h, :, lo:]

    def finish():
        o_ref[...] = _merge_heads(acc_ref[0], acc_ref[1]).astype(o_ref.dtype)

    return [scores, within_block, accumulate], finish


def _sb_scratch(tiles):
    return (_tile_pair(tiles, F32) + _tile_pair(tiles, F32) + _slot_pair((2, 1, tiles.q), F32)
            + [pltpu.VMEM((2, SB_HEAD_DIM, tiles.q), F32), pltpu.VMEM((2, 1, tiles.q), F32)])


def _mla_stages(tiles, i, qt_ref, k_ref, vt_ref, o_ref, s0, s1, acc_ref, m_ref):
    s_ref = _Slots(s0, s1)
    c2 = (MLA_NOPE_DIM + MLA_ROPE_DIM) ** -0.5 * LOG2E
    qh = [qt_ref[h * MLA_QK_PAD:(h + 1) * MLA_QK_PAD, :] for h in range(2)]
    acc_ref[...] = jnp.zeros_like(acc_ref)
    m_ref[...] = jnp.full_like(m_ref, NEG_BIG)

    def scores(j, slot, q_lo):
        lo = q_lo or 0
        for h in range(2):
            s = _dot(k_ref[_key_block(tiles, j), h * MLA_QK_PAD:(h + 1) * MLA_QK_PAD], qh[h][:, lo:])
            if q_lo is not None:
                s = jnp.where(_hidden(tiles, i, j, lo, visible_when_equal=True), NEG_BIG, s)
                if lo:
                    s_ref[slot, h, :, :lo] = jnp.full((tiles.k, lo), NEG_BIG, F32)
            s_ref[slot, h, :, lo:] = s

    def accumulate(j, slot, q_lo):
        lo = q_lo or 0
        for h in range(2):
            s = s_ref[slot, h, :, lo:]
            m = m_ref[h, :, lo:]
            m_new = jnp.maximum(m, jnp.max(s, axis=0, keepdims=True))
            p = jnp.exp2((s - m_new) * c2)
            rescale = jnp.exp2((m - m_new) * c2)
            m_ref[h, :, lo:] = m_new
            vt = vt_ref[h * _VT_PAD:h * _VT_PAD + _VT_ROWS, _key_block(tiles, j)]
            acc_ref[h, :, lo:] = rescale * acc_ref[h, :, lo:] + _dot(vt, p.astype(BF16))

    def finish():
        out = [acc_ref[h, :MLA_V_DIM, :] / acc_ref[h, MLA_V_DIM:MLA_V_DIM + 1, :] for h in range(2)]
        o_ref[...] = _merge_heads(*out).astype(o_ref.dtype)

    return [scores, accumulate], finish


def _mla_scratch(tiles):
    return _tile_pair(tiles, F32) + [pltpu.VMEM((2, _VT_ROWS, tiles.q), F32), pltpu.VMEM((2, 1, tiles.q), F32)]


def _attn_kernel(build_stages, tiles, qt_ref, k_ref, vt_ref, o_ref, *scratch):
    def query_block(i, carry):
        block = pl.ds(pl.multiple_of(i * tiles.q, tiles.q), tiles.q)
        stages, finish = build_stages(tiles, i, qt_ref.at[:, block], k_ref, vt_ref, o_ref.at[block], *scratch)
        _sweep_keys(i, stages, tiles)
        finish()
        return carry

    lax.fori_loop(0, qt_ref.shape[1] // tiles.q, query_block, 0)


def _attn_call(build_stages, scratch, name, q_t, k, v_t, qk_width, vt_rows, batch, seq, width):
    tiles = ATTN_TILES
    return pl.pallas_call(
        functools.partial(_attn_kernel, build_stages, tiles),
        out_shape=jax.ShapeDtypeStruct((k.shape[0], width), BF16),
        grid=(batch, HEAD_PAIRS),
        in_specs=[pl.BlockSpec((qk_width, seq), lambda b, p: (p, b)),
                  pl.BlockSpec((seq, qk_width), lambda b, p: (b, p)),
                  pl.BlockSpec((vt_rows, seq), lambda b, p: (p, b))],
        out_specs=pl.BlockSpec((seq, LANES), lambda b, p: (b, p)),
        scratch_shapes=scratch(tiles),
        compiler_params=_params("parallel", "parallel"),
        name=name,
    )(q_t, k, v_t)


def _sb_attn(q_t, k, v_t, batch, seq):
    return _attn_call(_sb_stages, _sb_scratch, "sb_attn", q_t, k, v_t, LANES, LANES, batch, seq, SB_WIDTH)


def _mla_attn(q_t, k, v_t, batch, seq):
    return _attn_call(_mla_stages, _mla_scratch, "mla_attn", q_t, k, v_t, 2 * MLA_QK_PAD, 2 * _VT_PAD,
                      batch, seq, MLA_WIDTH)


HALO = 8


def _conv_mixer_kernel(tiles_per_seq, x_ref, win_ref, cw_ref, wout_ref, g_ref, b_ref, o_ref, tail_ref):
    i = pl.program_id(0)
    x = x_ref[...]
    xb = x.astype(BF16)
    gate_b = _dot(xb, win_ref[:, :D_MODEL])
    u = _dot(xb, win_ref[:, D_MODEL:2 * D_MODEL]) * _dot(xb, win_ref[:, 2 * D_MODEL:])

    @pl.when(i % tiles_per_seq == 0)
    def _():
        tail_ref[...] = jnp.zeros_like(tail_ref)

    prev = tail_ref[...]
    tail_ref[...] = u[ROW_TILE - HALO:, :]
    ext = jnp.concatenate([prev, u], axis=0)
    cw = cw_ref[...]
    conv = (cw[0:1, :] * ext[HALO - 2:HALO - 2 + ROW_TILE, :]
            + cw[1:2, :] * ext[HALO - 1:HALO - 1 + ROW_TILE, :]
            + cw[2:3, :] * u)
    m = _dot((gate_b * conv).astype(BF16), wout_ref[...])
    o_ref[...] = _layer_norm(ALPHA * x + m, g_ref[...], b_ref[...])


def _conv_mixer(x, w_in, conv_w, w_out, g, b, seq):
    n = x.shape[0]
    row = pl.BlockSpec((ROW_TILE, D_MODEL), lambda i: (i, 0))
    return pl.pallas_call(
        functools.partial(_conv_mixer_kernel, seq // ROW_TILE),
        out_shape=jax.ShapeDtypeStruct((n, D_MODEL), F32),
        grid=(n // ROW_TILE,),
        in_specs=[row, _resident(w_in.shape), _resident(conv_w.shape), _resident(w_out.shape),
                  _resident(g.shape), _resident(b.shape)],
        out_specs=row,
        scratch_shapes=[pltpu.VMEM((HALO, D_MODEL), F32)],
        compiler_params=_params("arbitrary"),
        name="conv_mixer",
    )(x, w_in, conv_w, w_out, g, b)


def _rope_tables(seq):
    half = MLA_ROPE_DIM // 2
    inv_freq = ROPE_THETA ** (-jnp.arange(0, MLA_ROPE_DIM, 2, dtype=F32) / MLA_ROPE_DIM)
    ang = jnp.arange(seq, dtype=F32)[:, None] * inv_freq[None, :]
    cos, sin = jnp.cos(ang), jnp.sin(ang)
    ones = jnp.ones((seq, MLA_NOPE_DIM), F32)
    pad = jnp.zeros((seq, MLA_QK_PAD - MLA_NOPE_DIM - 2 * half), F32)
    cos_t = jnp.concatenate([ones, cos, cos, pad], axis=1)
    sin_t = jnp.concatenate([0.0 * ones, -sin, sin, pad], axis=1)
    return cos_t, sin_t


def _even_weights(w_in, w_uq, w_ukv):
    half = MLA_ROPE_DIM // 2
    qk = MLA_NOPE_DIM + MLA_ROPE_DIM
    zeros = lambda r, c: jnp.zeros((r, c), F32)
    sb_q, sb_k, sb_v = (slice(g * SB_WIDTH, (g + 1) * SB_WIDTH) for g in range(3))
    lat = slice(3 * SB_WIDTH, 3 * SB_WIDTH + MLA_Q_RANK + MLA_KV_RANK)
    w_kr = w_in[:, lat.stop:]
    k1, k2 = w_kr[:, :half], w_kr[:, half:]
    nope_pad = zeros(D_MODEL, MLA_NOPE_DIM)
    tail_pad = zeros(D_MODEL, MLA_QK_PAD - qk)
    wx = jnp.concatenate([w_in[:, sb_k], w_in[:, lat], nope_pad, k1, k2, tail_pad,
                          nope_pad, k2, k1, tail_pad], axis=1)
    wsqv = jnp.concatenate([w_in[:, sb_q], w_in[:, sb_v]], axis=1).T
    wq = w_uq.reshape(MLA_Q_RANK, MLA_HEADS, qk)
    q_nope, q1, q2 = wq[..., :MLA_NOPE_DIM], wq[..., MLA_NOPE_DIM:MLA_NOPE_DIM + half], wq[..., MLA_NOPE_DIM + half:]
    zq = lambda c: jnp.zeros((MLA_Q_RANK, MLA_HEADS, c), F32)
    wqa = jnp.concatenate([q_nope, q1, q2, zq(MLA_QK_PAD - qk)], axis=-1).reshape(MLA_Q_RANK, _QK_WIDTH)
    wqb = jnp.concatenate([zq(MLA_NOPE_DIM), q2, q1, zq(MLA_QK_PAD - qk)], axis=-1).reshape(MLA_Q_RANK, _QK_WIDTH)
    wkv = w_ukv.reshape(MLA_KV_RANK, MLA_HEADS, MLA_NOPE_DIM + MLA_V_DIM)
    wk = jnp.concatenate([wkv[..., :MLA_NOPE_DIM],
                          jnp.zeros((MLA_KV_RANK, MLA_HEADS, MLA_QK_PAD - MLA_NOPE_DIM), F32)],
                         axis=-1).reshape(MLA_KV_RANK, _QK_WIDTH)
    wv = jnp.concatenate([wkv[..., MLA_NOPE_DIM:],
                          jnp.zeros((MLA_KV_RANK, MLA_HEADS, _VT_PAD - MLA_V_DIM), F32)],
                         axis=-1).reshape(MLA_KV_RANK, MLA_HEADS * _VT_PAD).T
    return [w.astype(BF16) for w in (wx, wsqv, wqa.T, wqb.T, wk, wv)]


def kernel(x, ln_g, ln_b, ffn_w_gate, ffn_w_up, ffn_w_down, mix_w_in, mla_q_norm_g, mla_w_uq,
           mla_kv_norm_g, mla_w_ukv, mix_w_out, conv_w_in, conv_w, conv_w_out):
    batch, seq, _ = x.shape
    assert seq % FFN_ROW_TILE == 0 and seq % ROW_TILE == 0
    for tiles in (ATTN_TILES,):
        assert seq % tiles.q == 0 and tiles.q % (2 * tiles.k) == 0
    cos_t, sin_t = _rope_tables(seq)
    h = x.reshape(batch * seq, D_MODEL)
    vec = lambda a: a.reshape(1, -1)
    for layer in range(DEPTH):
        j = layer // 2
        ln = lambda s: (vec(ln_g[layer, s]), vec(ln_b[layer, s]))
        ffn_w = lambda s: (ffn_w_gate[layer, s].astype(BF16), ffn_w_up[layer, s].astype(BF16),
                           ffn_w_down[layer, s].astype(BF16))
        h = _ffn(h, *ffn_w(0), *ln(0))
        if layer % 2 == 0:
            wx, wsqv, wqa, wqb, wk, wv = _even_weights(mix_w_in[j], mla_w_uq[j], mla_w_ukv[j])
            sb_k, sb_qt, sb_vt, q_t, k, v_t = _proj_even(h, wx, wsqv, vec(mla_q_norm_g[j]), wqa, wqb,
                                                         vec(mla_kv_norm_g[j]), wk, wv, cos_t, sin_t, seq)
            out_sb = _sb_attn(sb_qt, sb_k, sb_vt, batch, seq)
            out_mla = _mla_attn(q_t, k, v_t, batch, seq)
            h = _attn_out_ffn(h, out_sb, out_mla, mix_w_out[j].astype(BF16), *ln(1), *ffn_w(1), *ln(2))
        else:
            h = _conv_mixer(h, conv_w_in[j].astype(BF16), conv_w[j], conv_w_out[j].astype(BF16), *ln(1), seq)
            h = _ffn(h, *ffn_w(1), *ln(2))
    return h.reshape(batch, seq, D_MODEL)
```

```python
import functools
from typing import NamedTuple

import jax
import jax.numpy as jnp
from jax import lax
from jax.experimental import pallas as pl
from jax.experimental.pallas import tpu as pltpu

D_MODEL = 1024
DEPTH = 4
SB_HEADS = 8
SB_HEAD_DIM = 64
MLA_HEADS = 8
MLA_NOPE_DIM = 64
MLA_ROPE_DIM = 32
MLA_V_DIM = 64
MLA_Q_RANK = 768
MLA_KV_RANK = 256
ROPE_THETA = 10000.0
CONV_K = 3
D_FF = 2816
LN_EPS = 1e-5
RMS_EPS = 1e-6
ALPHA = (2 * DEPTH) ** 0.25
SB_WIDTH = SB_HEADS * SB_HEAD_DIM
MLA_WIDTH = MLA_HEADS * MLA_V_DIM

LANES = 128
HEAD_PAIRS = SB_HEADS // 2
MLA_QK_PAD = LANES
_VT_PAD = LANES
BF16_SUBLANES = 16
_VT_ROWS = MLA_V_DIM + BF16_SUBLANES
FF_CHUNK = 256
ROW_TILE = 512
FFN_ROW_TILE = 1024


class _Tiling(NamedTuple):
    q: int
    k: int

    @property
    def diag(self):
        return self.q // self.k


ATTN_TILES = _Tiling(q=512, k=256)
VMEM_LIMIT = 56 * 1024 * 1024
PAIRS_PER_TRIP = 2

BF16 = jnp.bfloat16
F32 = jnp.float32
NEG_BIG = -1e30
LOG2E = 1.4426950408889634


def _dot(a, b):
    return jnp.dot(a, b, preferred_element_type=F32)


def _dot_nt(a, b):
    return lax.dot_general(a, b, (((1,), (1,)), ((), ())), preferred_element_type=F32)


def _layer_norm(y, g, b):
    mu = jnp.mean(y, axis=-1, keepdims=True)
    d = y - mu
    var = jnp.mean(d * d, axis=-1, keepdims=True)
    return d * lax.rsqrt(var + LN_EPS) * g + b


def _rms_norm(c, g):
    return c * lax.rsqrt(jnp.mean(c * c, axis=-1, keepdims=True) + RMS_EPS) * g


def _params(*sem):
    return pltpu.CompilerParams(dimension_semantics=sem, vmem_limit_bytes=VMEM_LIMIT)


def _resident(shape):
    nd = len(shape)
    return pl.BlockSpec(shape, lambda *_: (0,) * nd, pipeline_mode=pl.Buffered(1))


def _half_step_ffn(x, wg_ref, wu_ref, wd_ref, g_ref, b_ref, h_ref):
    xb = x.astype(BF16)
    for c in range(D_FF // FF_CHUNK):
        sl = slice(c * FF_CHUNK, (c + 1) * FF_CHUNK)
        gate = _dot(xb, wg_ref[:, sl])
        up = _dot(xb, wu_ref[:, sl])
        h_ref[:, sl] = (gate * (1.0 / (1.0 + jnp.exp(-gate))) * up).astype(BF16)
    f = _dot(h_ref[...], wd_ref[...])
    return _layer_norm(ALPHA * x + 0.5 * f, g_ref[...], b_ref[...])


def _ffn_kernel(x_ref, wg_ref, wu_ref, wd_ref, g_ref, b_ref, o_ref, h_ref):
    o_ref[...] = _half_step_ffn(x_ref[...], wg_ref, wu_ref, wd_ref, g_ref, b_ref, h_ref)


def _attn_out_ffn_kernel(x_ref, sb_ref, mla_ref, wo_ref, go_ref, bo_ref,
                         wg_ref, wu_ref, wd_ref, g_ref, b_ref, o_ref, h_ref):
    m = _dot(sb_ref[...], wo_ref[:SB_WIDTH, :]) + _dot(mla_ref[...], wo_ref[SB_WIDTH:, :])
    x = _layer_norm(ALPHA * x_ref[...] + m, go_ref[...], bo_ref[...])
    o_ref[...] = _half_step_ffn(x, wg_ref, wu_ref, wd_ref, g_ref, b_ref, h_ref)


def _ffn_call(body, name, rows, residents):
    n = rows[0][0].shape[0]
    row = lambda w: pl.BlockSpec((FFN_ROW_TILE, w), lambda i: (i, 0))
    return pl.pallas_call(
        body,
        out_shape=jax.ShapeDtypeStruct((n, D_MODEL), F32),
        grid=(n // FFN_ROW_TILE,),
        in_specs=[row(w) for _, w in rows] + [_resident(a.shape) for a in residents],
        out_specs=row(D_MODEL),
        scratch_shapes=[pltpu.VMEM((FFN_ROW_TILE, D_FF), BF16)],
        compiler_params=_params("parallel"),
        name=name,
    )(*[a for a, _ in rows], *residents)


def _ffn(x, wg, wu, wd, g, b):
    return _ffn_call(_ffn_kernel, "ffn", [(x, D_MODEL)], [wg, wu, wd, g, b])


def _attn_out_ffn(x, sb, mla, wo, go, bo, wg, wu, wd, g, b):
    return _ffn_call(_attn_out_ffn_kernel, "attn_out_ffn",
                     [(x, D_MODEL), (sb, SB_WIDTH), (mla, MLA_WIDTH)], [wo, go, bo, wg, wu, wd, g, b])


_X_K = SB_WIDTH
_X_CQ = _X_K + MLA_Q_RANK
_X_CKV = _X_CQ + MLA_KV_RANK
_X_KRA = _X_CKV + LANES
_X_KRB = _X_KRA + LANES
_QK_WIDTH = MLA_HEADS * MLA_QK_PAD


def _proj_even_kernel(x_ref, wx_ref, wsqv_ref, gq_ref, wqa_ref, wqb_ref, gkv_ref, wk_ref, wv_ref,
                      cos_ref, sin_ref, cos_t_ref, sin_t_ref,
                      sbk_ref, sbq_ref, sbv_ref, q_ref, k_ref, v_ref):
    xb = x_ref[...].astype(BF16)
    qv = _dot_nt(wsqv_ref[...], xb)
    sbq_ref[...] = (qv[:SB_WIDTH, :] * (SB_HEAD_DIM ** -0.5)).astype(BF16)
    sbv_ref[...] = qv[SB_WIDTH:, :].astype(BF16)
    sbk_ref[...] = _dot(xb, wx_ref[:, :_X_K]).astype(BF16)
    cq = _rms_norm(_dot(xb, wx_ref[:, _X_K:_X_CQ]), gq_ref[...]).astype(BF16)
    cos_t = cos_t_ref[...]
    sin_t = sin_t_ref[...]
    qa = _dot_nt(wqa_ref[...], cq)
    qb = _dot_nt(wqb_ref[...], cq)
    for h in range(MLA_HEADS):
        sl = slice(h * MLA_QK_PAD, (h + 1) * MLA_QK_PAD)
        q_ref[sl, :] = (qa[sl, :] * cos_t + qb[sl, :] * sin_t).astype(BF16)
    ckv = _rms_norm(_dot(xb, wx_ref[:, _X_CQ:_X_CKV]), gkv_ref[...]).astype(BF16)
    kr_pair = _dot(xb, wx_ref[:, _X_CKV:_X_KRB])
    kr = kr_pair[:, :LANES] * cos_ref[...] + kr_pair[:, LANES:] * sin_ref[...]
    k_nope = _dot(ckv, wk_ref[...])
    for h in range(MLA_HEADS):
        sl = slice(h * MLA_QK_PAD, (h + 1) * MLA_QK_PAD)
        k_ref[:, sl] = (k_nope[:, sl] + kr).astype(BF16)
    vt = _dot_nt(wv_ref[...], ckv)
    ones_row = lax.broadcasted_iota(jnp.int32, vt.shape, 0) % _VT_PAD == MLA_V_DIM
    v_ref[...] = jnp.where(ones_row, 1.0, vt).astype(BF16)


def _proj_even(x, wx, wsqv, gq, wqa, wqb, gkv, wk, wv, cos, sin, seq):
    n = x.shape[0]
    tiles_per_seq = seq // ROW_TILE
    row = lambda w: pl.BlockSpec((ROW_TILE, w), lambda i: (i, 0))
    col = lambda h: pl.BlockSpec((h, ROW_TILE), lambda i: (0, i))
    pos = pl.BlockSpec((ROW_TILE, LANES), lambda i: (i % tiles_per_seq, 0))
    pos_t = pl.BlockSpec((LANES, ROW_TILE), lambda i: (0, i % tiles_per_seq))
    rows = lambda w: jax.ShapeDtypeStruct((n, w), BF16)
    cols = lambda h: jax.ShapeDtypeStruct((h, n), BF16)
    return pl.pallas_call(
        _proj_even_kernel,
        out_shape=(rows(SB_WIDTH), cols(SB_WIDTH), cols(SB_WIDTH),
                   cols(_QK_WIDTH), rows(_QK_WIDTH), cols(MLA_HEADS * _VT_PAD)),
        grid=(n // ROW_TILE,),
        in_specs=[row(D_MODEL), _resident(wx.shape), _resident(wsqv.shape), _resident(gq.shape),
                  _resident(wqa.shape), _resident(wqb.shape), _resident(gkv.shape),
                  _resident(wk.shape), _resident(wv.shape), pos, pos, pos_t, pos_t],
        out_specs=(row(SB_WIDTH), col(SB_WIDTH), col(SB_WIDTH),
                   col(_QK_WIDTH), row(_QK_WIDTH), col(MLA_HEADS * _VT_PAD)),
        compiler_params=_params("parallel"),
        name="proj_even",
    )(x, wx, wsqv, gq, wqa, wqb, gkv, wk, wv, cos, sin, cos.T, sin.T)


def _sweep_keys(i, stages, tiles):
    depth = len(stages)
    blocks = (i + 1) * tiles.diag
    lead = [depth - 1 - s for s in range(depth)]
    peel = max(0, tiles.diag - depth + 1)
    for n in range(1 - depth, peel):
        for s, stage in enumerate(stages):
            b = n + lead[s]
            if b >= 0:
                stage(blocks - 1 - b, b % 2, (tiles.diag - 1 - b) * tiles.k if b < tiles.diag else None)

    def step_pair(t):
        for u in range(2):
            for s in range(depth):
                b_off = peel + u + lead[s]
                stages[s](blocks - 1 - b_off - 2 * t, b_off % 2, None)

    def body(t, carry):
        for r in range(PAIRS_PER_TRIP):
            step_pair(PAIRS_PER_TRIP * t + r)
        return carry

    pairs = i * (tiles.diag // 2)
    lax.fori_loop(0, pairs // PAIRS_PER_TRIP, body, 0)
    if (tiles.diag // 2) % PAIRS_PER_TRIP:
        for r in range(PAIRS_PER_TRIP - 1):
            pl.when(r < pairs % PAIRS_PER_TRIP)(
                functools.partial(step_pair, pairs - pairs % PAIRS_PER_TRIP + r))
    for e in range(depth - 1):
        for s, stage in enumerate(stages):
            j = depth - 2 - e - lead[s]
            if j >= 0:
                stage(j, (j + 1) % 2, None)


def _hidden(tiles, i, j, q_lo, visible_when_equal):
    shape = (tiles.k, tiles.q - q_lo)
    key = j * tiles.k + lax.broadcasted_iota(jnp.int32, shape, 0)
    qry = i * tiles.q + q_lo + lax.broadcasted_iota(jnp.int32, shape, 1)
    return key > qry if visible_when_equal else key >= qry


def _key_block(tiles, j):
    return pl.ds(pl.multiple_of(j * tiles.k, tiles.k), tiles.k)


def _merge_heads(o0, o1):
    return jnp.concatenate([o0, o1], axis=0).T


class _Slots:
    def __init__(self, first, second):
        self._refs = (first, second)

    def __getitem__(self, idx):
        return self._refs[idx[0]][idx[1:]]

    def __setitem__(self, idx, value):
        self._refs[idx[0]][idx[1:]] = value


def _slot_pair(shape, dtype):
    return [pltpu.VMEM(shape, dtype)] * 2


def _tile_pair(tiles, dtype):
    return _slot_pair((2, tiles.k, tiles.q), dtype)


def _sb_stages(tiles, i, qt_ref, k_ref, vt_ref, o_ref, z0, z1, e0, e1, total0, total1, acc_ref, later_ref):
    z_ref, e_ref, total_ref = _Slots(z0, z1), _Slots(e0, e1), _Slots(total0, total1)
    qt = qt_ref[...]
    dim = lax.broadcasted_iota(jnp.int32, (LANES, tiles.q), 0)
    qh = [jnp.where(dim < SB_HEAD_DIM, qt, jnp.zeros_like(qt)),
          jnp.where(dim >= SB_HEAD_DIM, qt, jnp.zeros_like(qt))]
    r = lax.broadcasted_iota(jnp.int32, (tiles.k, tiles.k), 0)
    c = lax.broadcasted_iota(jnp.int32, (tiles.k, tiles.k), 1)
    tri = jnp.where(c > r, 1.0, 0.0).astype(BF16)
    acc_ref[...] = jnp.zeros_like(acc_ref)
    later_ref[...] = jnp.zeros_like(later_ref)

    def scores(j, slot, q_lo):
        k = k_ref[_key_block(tiles, j), :]
        lo = q_lo or 0
        for h in range(2):
            z = _dot(k, qh[h][:, lo:])
            if q_lo is not None:
                z = jnp.where(_hidden(tiles, i, j, lo, visible_when_equal=False), NEG_BIG, z)
                if lo:
                    z_ref[slot, h, :, :lo] = jnp.full((tiles.k, lo), NEG_BIG, F32)
            z_ref[slot, h, :, lo:] = z

    def within_block(j, slot, q_lo):
        lo = q_lo or 0
        for h in range(2):
            z = z_ref[slot, h, :, lo:]
            neg_abs = lax.bitcast_convert_type(
                lax.bitcast_convert_type(z, jnp.uint32) | jnp.uint32(0x80000000), F32)
            softplus = jnp.maximum(z, 0.0) + jnp.log(1.0 + jnp.exp(neg_abs))
            later = _dot(tri, softplus.astype(BF16))
            if lo:
                e_ref[slot, h, :, :lo] = jnp.full((tiles.k, lo), NEG_BIG, F32)
                total_ref[slot, h, :, :lo] = jnp.zeros((1, lo), F32)
            e_ref[slot, h, :, lo:] = (z - softplus) - later
            total_ref[slot, h, :, lo:] = later[0:1, :] + softplus[0:1, :]

    def accumulate(j, slot, q_lo):
        lo = q_lo or 0
        for h in range(2):
            w = jnp.exp(e_ref[slot, h, :, lo:])
            vt = vt_ref[h * SB_HEAD_DIM:(h + 1) * SB_HEAD_DIM, _key_block(tiles, j)]
            acc_ref[h, :, lo:] += _dot(vt, w.astype(BF16)) * jnp.exp(-later_ref[h, :, lo:])
            later_ref[h, :, lo:] += total_ref[slot, h, :, lo:]

    def finish():
        o_ref[...] = _merge_heads(acc_ref[0], acc_ref[1]).astype(o_ref.dtype)

    return [scores, within_block, accumulate], finish


def _sb_scratch(tiles):
    return (_tile_pair(tiles, F32) + _tile_pair(tiles, F32) + _slot_pair((2, 1, tiles.q), F32)
            + [pltpu.VMEM((2, SB_HEAD_DIM, tiles.q), F32), pltpu.VMEM((2, 1, tiles.q), F32)])


def _mla_stages(tiles, i, qt_ref, k_ref, vt_ref, o_ref, s0, s1, acc_ref, m_ref):
    s_ref = _Slots(s0, s1)
    c2 = (MLA_NOPE_DIM + MLA_ROPE_DIM) ** -0.5 * LOG2E
    qh = [qt_ref[h * MLA_QK_PAD:(h + 1) * MLA_QK_PAD, :] for h in range(2)]
    acc_ref[...] = jnp.zeros_like(acc_ref)
    m_ref[...] = jnp.full_like(m_ref, NEG_BIG)

    def scores(j, slot, q_lo):
        lo = q_lo or 0
        for h in range(2):
            s = _dot(k_ref[_key_block(tiles, j), h * MLA_QK_PAD:(h + 1) * MLA_QK_PAD], qh[h][:, lo:])
            if q_lo is not None:
                s = jnp.where(_hidden(tiles, i, j, lo, visible_when_equal=True), NEG_BIG, s)
                if lo:
                    s_ref[slot, h, :, :lo] = jnp.full((tiles.k, lo), NEG_BIG, F32)
            s_ref[slot, h, :, lo:] = s

    def accumulate(j, slot, q_lo):
        lo = q_lo or 0
        for h in range(2):
            s = s_ref[slot, h, :, lo:]
            m = m_ref[h, :, lo:]
            m_new = jnp.maximum(m, jnp.max(s, axis=0, keepdims=True))
            p = jnp.exp2((s - m_new) * c2)
            rescale = jnp.exp2((m - m_new) * c2)
            m_ref[h, :, lo:] = m_new
            vt = vt_ref[h * _VT_PAD:h * _VT_PAD + _VT_ROWS, _key_block(tiles, j)]
            acc_ref[h, :, lo:] = rescale * acc_ref[h, :, lo:] + _dot(vt, p.astype(BF16))

    def finish():
        out = [acc_ref[h, :MLA_V_DIM, :] / acc_ref[h, MLA_V_DIM:MLA_V_DIM + 1, :] for h in range(2)]
        o_ref[...] = _merge_heads(*out).astype(o_ref.dtype)

    return [scores, accumulate], finish


def _mla_scratch(tiles):
    return _tile_pair(tiles, F32) + [pltpu.VMEM((2, _VT_ROWS, tiles.q), F32), pltpu.VMEM((2, 1, tiles.q), F32)]


def _attn_kernel(build_stages, tiles, qt_ref, k_ref, vt_ref, o_ref, *scratch):
    def query_block(i, carry):
        block = pl.ds(pl.multiple_of(i * tiles.q, tiles.q), tiles.q)
        stages, finish = build_stages(tiles, i, qt_ref.at[:, block], k_ref, vt_ref, o_ref.at[block], *scratch)
        _sweep_keys(i, stages, tiles)
        finish()
        return carry

    lax.fori_loop(0, qt_ref.shape[1] // tiles.q, query_block, 0)


def _attn_call(build_stages, scratch, name, q_t, k, v_t, qk_width, vt_rows, batch, seq, width):
    tiles = ATTN_TILES
    return pl.pallas_call(
        functools.partial(_attn_kernel, build_stages, tiles),
        out_shape=jax.ShapeDtypeStruct((k.shape[0], width), BF16),
        grid=(batch, HEAD_PAIRS),
        in_specs=[pl.BlockSpec((qk_width, seq), lambda b, p: (p, b)),
                  pl.BlockSpec((seq, qk_width), lambda b, p: (b, p)),
                  pl.BlockSpec((vt_rows, seq), lambda b, p: (p, b))],
        out_specs=pl.BlockSpec((seq, LANES), lambda b, p: (b, p)),
        scratch_shapes=scratch(tiles),
        compiler_params=_params("parallel", "parallel"),
        name=name,
    )(q_t, k, v_t)


def _sb_attn(q_t, k, v_t, batch, seq):
    return _attn_call(_sb_stages, _sb_scratch, "sb_attn", q_t, k, v_t, LANES, LANES, batch, seq, SB_WIDTH)


def _mla_attn(q_t, k, v_t, batch, seq):
    return _attn_call(_mla_stages, _mla_scratch, "mla_attn", q_t, k, v_t, 2 * MLA_QK_PAD, 2 * _VT_PAD,
                      batch, seq, MLA_WIDTH)


HALO = 8


def _conv_mixer_kernel(tiles_per_seq, x_ref, win_ref, cw_ref, wout_ref, g_ref, b_ref, o_ref, tail_ref):
    i = pl.program_id(0)
    x = x_ref[...]
    xb = x.astype(BF16)
    gate_b = _dot(xb, win_ref[:, :D_MODEL])
    u = _dot(xb, win_ref[:, D_MODEL:2 * D_MODEL]) * _dot(xb, win_ref[:, 2 * D_MODEL:])

    @pl.when(i % tiles_per_seq == 0)
    def _():
        tail_ref[...] = jnp.zeros_like(tail_ref)

    prev = tail_ref[...]
    tail_ref[...] = u[ROW_TILE - HALO:, :]
    ext = jnp.concatenate([prev, u], axis=0)
    cw = cw_ref[...]
    conv = (cw[0:1, :] * ext[HALO - 2:HALO - 2 + ROW_TILE, :]
            + cw[1:2, :] * ext[HALO - 1:HALO - 1 + ROW_TILE, :]
            + cw[2:3, :] * u)
    m = _dot((gate_b * conv).astype(BF16), wout_ref[...])
    o_ref[...] = _layer_norm(ALPHA * x + m, g_ref[...], b_ref[...])


def _conv_mixer(x, w_in, conv_w, w_out, g, b, seq):
    n = x.shape[0]
    row = pl.BlockSpec((ROW_TILE, D_MODEL), lambda i: (i, 0))
    return pl.pallas_call(
        functools.partial(_conv_mixer_kernel, seq // ROW_TILE),
        out_shape=jax.ShapeDtypeStruct((n, D_MODEL), F32),
        grid=(n // ROW_TILE,),
        in_specs=[row, _resident(w_in.shape), _resident(conv_w.shape), _resident(w_out.shape),
                  _resident(g.shape), _resident(b.shape)],
        out_specs=row,
        scratch_shapes=[pltpu.VMEM((HALO, D_MODEL), F32)],
        compiler_params=_params("arbitrary"),
        name="conv_mixer",
    )(x, w_in, conv_w, w_out, g, b)


def _rope_tables(seq):
    half = MLA_ROPE_DIM // 2
    inv_freq = ROPE_THETA ** (-jnp.arange(0, MLA_ROPE_DIM, 2, dtype=F32) / MLA_ROPE_DIM)
    ang = jnp.arange(seq, dtype=F32)[:, None] * inv_freq[None, :]
    cos, sin = jnp.cos(ang), jnp.sin(ang)
    ones = jnp.ones((seq, MLA_NOPE_DIM), F32)
    pad = jnp.zeros((seq, MLA_QK_PAD - MLA_NOPE_DIM - 2 * half), F32)
    cos_t = jnp.concatenate([ones, cos, cos, pad], axis=1)
    sin_t = jnp.concatenate([0.0 * ones, -sin, sin, pad], axis=1)
    return cos_t, sin_t


def _even_weights(w_in, w_uq, w_ukv):
    half = MLA_ROPE_DIM // 2
    qk = MLA_NOPE_DIM + MLA_ROPE_DIM
    zeros = lambda r, c: jnp.zeros((r, c), F32)
    sb_q, sb_k, sb_v = (slice(g * SB_WIDTH, (g + 1) * SB_WIDTH) for g in range(3))
    lat = slice(3 * SB_WIDTH, 3 * SB_WIDTH + MLA_Q_RANK + MLA_KV_RANK)
    w_kr = w_in[:, lat.stop:]
    k1, k2 = w_kr[:, :half], w_kr[:, half:]
    nope_pad = zeros(D_MODEL, MLA_NOPE_DIM)
    tail_pad = zeros(D_MODEL, MLA_QK_PAD - qk)
    wx = jnp.concatenate([w_in[:, sb_k], w_in[:, lat], nope_pad, k1, k2, tail_pad,
                          nope_pad, k2, k1, tail_pad], axis=1)
    wsqv = jnp.concatenate([w_in[:, sb_q], w_in[:, sb_v]], axis=1).T
    wq = w_uq.reshape(MLA_Q_RANK, MLA_HEADS, qk)
    q_nope, q1, q2 = wq[..., :MLA_NOPE_DIM], wq[..., MLA_NOPE_DIM:MLA_NOPE_DIM + half], wq[..., MLA_NOPE_DIM + half:]
    zq = lambda c: jnp.zeros((MLA_Q_RANK, MLA_HEADS, c), F32)
    wqa = jnp.concatenate([q_nope, q1, q2, zq(MLA_QK_PAD - qk)], axis=-1).reshape(MLA_Q_RANK, _QK_WIDTH)
    wqb = jnp.concatenate([zq(MLA_NOPE_DIM), q2, q1, zq(MLA_QK_PAD - qk)], axis=-1).reshape(MLA_Q_RANK, _QK_WIDTH)
    wkv = w_ukv.reshape(MLA_KV_RANK, MLA_HEADS, MLA_NOPE_DIM + MLA_V_DIM)
    wk = jnp.concatenate([wkv[..., :MLA_NOPE_DIM],
                          jnp.zeros((MLA_KV_RANK, MLA_HEADS, MLA_QK_PAD - MLA_NOPE_DIM), F32)],
                         axis=-1).reshape(MLA_KV_RANK, _QK_WIDTH)
    wv = jnp.concatenate([wkv[..., MLA_NOPE_DIM:],
                          jnp.zeros((MLA_KV_RANK, MLA_HEADS, _VT_PAD - MLA_V_DIM), F32)],
                         axis=-1).reshape(MLA_KV_RANK, MLA_HEADS * _VT_PAD).T
    return [w.astype(BF16) for w in (wx, wsqv, wqa.T, wqb.T, wk, wv)]


def kernel(x, ln_g, ln_b, ffn_w_gate, ffn_w_up, ffn_w_down, mix_w_in, mla_q_norm_g, mla_w_uq,
           mla_kv_norm_g, mla_w_ukv, mix_w_out, conv_w_in, conv_w, conv_w_out):
    batch, seq, _ = x.shape
    assert seq % FFN_ROW_TILE == 0 and seq % ROW_TILE == 0
    for tiles in (ATTN_TILES,):
        assert seq % tiles.q == 0 and tiles.q % (2 * tiles.k) == 0
    cos_t, sin_t = _rope_tables(seq)
    h = x.reshape(batch * seq, D_MODEL)
    vec = lambda a: a.reshape(1, -1)
    for layer in range(DEPTH):
        j = layer // 2
        ln = lambda s: (vec(ln_g[layer, s]), vec(ln_b[layer, s]))
        ffn_w = lambda s: (ffn_w_gate[layer, s].astype(BF16), ffn_w_up[layer, s].astype(BF16),
                           ffn_w_down[layer, s].astype(BF16))
        h = _ffn(h, *ffn_w(0), *ln(0))
        if layer % 2 == 0:
            wx, wsqv, wqa, wqb, wk, wv = _even_weights(mix_w_in[j], mla_w_uq[j], mla_w_ukv[j])
            sb_k, sb_qt, sb_vt, q_t, k, v_t = _proj_even(h, wx, wsqv, vec(mla_q_norm_g[j]), wqa, wqb,
                                                         vec(mla_kv_norm_g[j]), wk, wv, cos_t, sin_t, seq)
            out_sb = _sb_attn(sb_qt, sb_k, sb_vt, batch, seq)
            out_mla = _mla_attn(q_t, k, v_t, batch, seq)
            h = _attn_out_ffn(h, out_sb, out_mla, mix_w_out[j].astype(BF16), *ln(1), *ffn_w(1), *ln(2))
        else:
            h = _conv_mixer(h, conv_w_in[j].astype(BF16), conv_w[j], conv_w_out[j].astype(BF16), *ln(1), seq)
            h = _ffn(h, *ffn_w(1), *ln(2))
    return h.reshape(batch, seq, D_MODEL)
```

```python
import functools
from typing import NamedTuple

import jax
import jax.numpy as jnp
from jax import lax
from jax.experimental import pallas as pl
from jax.experimental.pallas import tpu as pltpu

D_MODEL = 1024
DEPTH = 4
SB_HEADS = 8
SB_HEAD_DIM = 64
MLA_HEADS = 8
MLA_NOPE_DIM = 64
MLA_ROPE_DIM = 32
MLA_V_DIM = 64
MLA_Q_RANK = 768
MLA_KV_RANK = 256
ROPE_THETA = 10000.0
CONV_K = 3
D_FF = 2816
LN_EPS = 1e-5
RMS_EPS = 1e-6
ALPHA = (2 * DEPTH) ** 0.25
SB_WIDTH = SB_HEADS * SB_HEAD_DIM
MLA_WIDTH = MLA_HEADS * MLA_V_DIM

LANES = 128
HEAD_PAIRS = SB_HEADS // 2
MLA_QK_PAD = LANES
_VT_PAD = LANES
BF16_SUBLANES = 16
_VT_ROWS = MLA_V_DIM + BF16_SUBLANES
FF_CHUNK = 256
ROW_TILE = 1024
FFN_ROW_TILE = 1024


class _Tiling(NamedTuple):
    q: int
    k: int

    @property
    def diag(self):
        return self.q // self.k


ATTN_TILES = _Tiling(q=512, k=256)
VMEM_LIMIT = 56 * 1024 * 1024
PAIRS_PER_TRIP = 2

BF16 = jnp.bfloat16
F32 = jnp.float32
NEG_BIG = -1e30
LOG2E = 1.4426950408889634


def _dot(a, b):
    return jnp.dot(a, b, preferred_element_type=F32)


def _dot_nt(a, b):
    return lax.dot_general(a, b, (((1,), (1,)), ((), ())), preferred_element_type=F32)


def _layer_norm(y, g, b):
    mu = jnp.mean(y, axis=-1, keepdims=True)
    d = y - mu
    var = jnp.mean(d * d, axis=-1, keepdims=True)
    return d * lax.rsqrt(var + LN_EPS) * g + b


def _rms_norm(c, g):
    return c * lax.rsqrt(jnp.mean(c * c, axis=-1, keepdims=True) + RMS_EPS) * g


def _params(*sem):
    return pltpu.CompilerParams(dimension_semantics=sem, vmem_limit_bytes=VMEM_LIMIT)


def _resident(shape):
    nd = len(shape)
    return pl.BlockSpec(shape, lambda *_: (0,) * nd, pipeline_mode=pl.Buffered(1))


def _half_step_ffn(x, wg_ref, wu_ref, wd_ref, g_ref, b_ref, h_ref):
    xb = x.astype(BF16)
    for c in range(D_FF // FF_CHUNK):
        sl = slice(c * FF_CHUNK, (c + 1) * FF_CHUNK)
        gate = _dot(xb, wg_ref[:, sl])
        up = _dot(xb, wu_ref[:, sl])
        h_ref[:, sl] = (gate * (1.0 / (1.0 + jnp.exp(-gate))) * up).astype(BF16)
    f = _dot(h_ref[...], wd_ref[...])
    return _layer_norm(ALPHA * x + 0.5 * f, g_ref[...], b_ref[...])


def _ffn_kernel(x_ref, wg_ref, wu_ref, wd_ref, g_ref, b_ref, o_ref, h_ref):
    o_ref[...] = _half_step_ffn(x_ref[...], wg_ref, wu_ref, wd_ref, g_ref, b_ref, h_ref)


def _attn_out_ffn_kernel(x_ref, sb_ref, mla_ref, wo_ref, go_ref, bo_ref,
                         wg_ref, wu_ref, wd_ref, g_ref, b_ref, o_ref, h_ref):
    m = _dot(sb_ref[...], wo_ref[:SB_WIDTH, :]) + _dot(mla_ref[...], wo_ref[SB_WIDTH:, :])
    x = _layer_norm(ALPHA * x_ref[...] + m, go_ref[...], bo_ref[...])
    o_ref[...] = _half_step_ffn(x, wg_ref, wu_ref, wd_ref, g_ref, b_ref, h_ref)


def _ffn_call(body, name, rows, residents):
    n = rows[0][0].shape[0]
    row = lambda w: pl.BlockSpec((FFN_ROW_TILE, w), lambda i: (i, 0))
    return pl.pallas_call(
        body,
        out_shape=jax.ShapeDtypeStruct((n, D_MODEL), F32),
        grid=(n // FFN_ROW_TILE,),
        in_specs=[row(w) for _, w in rows] + [_resident(a.shape) for a in residents],
        out_specs=row(D_MODEL),
        scratch_shapes=[pltpu.VMEM((FFN_ROW_TILE, D_FF), BF16)],
        compiler_params=_params("parallel"),
        name=name,
    )(*[a for a, _ in rows], *residents)


def _ffn(x, wg, wu, wd, g, b):
    return _ffn_call(_ffn_kernel, "ffn", [(x, D_MODEL)], [wg, wu, wd, g, b])


def _attn_out_ffn(x, sb, mla, wo, go, bo, wg, wu, wd, g, b):
    return _ffn_call(_attn_out_ffn_kernel, "attn_out_ffn",
                     [(x, D_MODEL), (sb, SB_WIDTH), (mla, MLA_WIDTH)], [wo, go, bo, wg, wu, wd, g, b])


_X_K = SB_WIDTH
_X_CQ = _X_K + MLA_Q_RANK
_X_CKV = _X_CQ + MLA_KV_RANK
_X_KRA = _X_CKV + LANES
_X_KRB = _X_KRA + LANES
_QK_WIDTH = MLA_HEADS * MLA_QK_PAD


def _proj_even_kernel(x_ref, wx_ref, wsqv_ref, gq_ref, wqa_ref, wqb_ref, gkv_ref, wk_ref, wv_ref,
                      cos_ref, sin_ref, cos_t_ref, sin_t_ref,
                      sbk_ref, sbq_ref, sbv_ref, q_ref, k_ref, v_ref):
    xb = x_ref[...].astype(BF16)
    qv = _dot_nt(wsqv_ref[...], xb)
    sbq_ref[...] = (qv[:SB_WIDTH, :] * (SB_HEAD_DIM ** -0.5)).astype(BF16)
    sbv_ref[...] = qv[SB_WIDTH:, :].astype(BF16)
    sbk_ref[...] = _dot(xb, wx_ref[:, :_X_K]).astype(BF16)
    cq = _rms_norm(_dot(xb, wx_ref[:, _X_K:_X_CQ]), gq_ref[...]).astype(BF16)
    cos_t = cos_t_ref[...]
    sin_t = sin_t_ref[...]
    qa = _dot_nt(wqa_ref[...], cq)
    qb = _dot_nt(wqb_ref[...], cq)
    for h in range(MLA_HEADS):
        sl = slice(h * MLA_QK_PAD, (h + 1) * MLA_QK_PAD)
        q_ref[sl, :] = (qa[sl, :] * cos_t + qb[sl, :] * sin_t).astype(BF16)
    ckv = _rms_norm(_dot(xb, wx_ref[:, _X_CQ:_X_CKV]), gkv_ref[...]).astype(BF16)
    kr_pair = _dot(xb, wx_ref[:, _X_CKV:_X_KRB])
    kr = kr_pair[:, :LANES] * cos_ref[...] + kr_pair[:, LANES:] * sin_ref[...]
    k_nope = _dot(ckv, wk_ref[...])
    for h in range(MLA_HEADS):
        sl = slice(h * MLA_QK_PAD, (h + 1) * MLA_QK_PAD)
        k_ref[:, sl] = (k_nope[:, sl] + kr).astype(BF16)
    vt = _dot_nt(wv_ref[...], ckv)
    ones_row = lax.broadcasted_iota(jnp.int32, vt.shape, 0) % _VT_PAD == MLA_V_DIM
    v_ref[...] = jnp.where(ones_row, 1.0, vt).astype(BF16)


def _proj_even(x, wx, wsqv, gq, wqa, wqb, gkv, wk, wv, cos, sin, seq):
    n = x.shape[0]
    tiles_per_seq = seq // ROW_TILE
    row = lambda w: pl.BlockSpec((ROW_TILE, w), lambda i: (i, 0))
    col = lambda h: pl.BlockSpec((h, ROW_TILE), lambda i: (0, i))
    pos = pl.BlockSpec((ROW_TILE, LANES), lambda i: (i % tiles_per_seq, 0))
    pos_t = pl.BlockSpec((LANES, ROW_TILE), lambda i: (0, i % tiles_per_seq))
    rows = lambda w: jax.ShapeDtypeStruct((n, w), BF16)
    cols = lambda h: jax.ShapeDtypeStruct((h, n), BF16)
    return pl.pallas_call(
        _proj_even_kernel,
        out_shape=(rows(SB_WIDTH), cols(SB_WIDTH), cols(SB_WIDTH),
                   cols(_QK_WIDTH), rows(_QK_WIDTH), cols(MLA_HEADS * _VT_PAD)),
        grid=(n // ROW_TILE,),
        in_specs=[row(D_MODEL), _resident(wx.shape), _resident(wsqv.shape), _resident(gq.shape),
                  _resident(wqa.shape), _resident(wqb.shape), _resident(gkv.shape),
                  _resident(wk.shape), _resident(wv.shape), pos, pos, pos_t, pos_t],
        out_specs=(row(SB_WIDTH), col(SB_WIDTH), col(SB_WIDTH),
                   col(_QK_WIDTH), row(_QK_WIDTH), col(MLA_HEADS * _VT_PAD)),
        compiler_params=_params("parallel"),
        name="proj_even",
    )(x, wx, wsqv, gq, wqa, wqb, gkv, wk, wv, cos, sin, cos.T, sin.T)


def _sweep_keys(i, stages, tiles):
    depth = len(stages)
    blocks = (i + 1) * tiles.diag
    lead = [depth - 1 - s for s in range(depth)]
    peel = max(0, tiles.diag - depth + 1)
    for n in range(1 - depth, peel):
        for s, stage in enumerate(stages):
            b = n + lead[s]
            if b >= 0:
                stage(blocks - 1 - b, b % 2, (tiles.diag - 1 - b) * tiles.k if b < tiles.diag else None)

    def step_pair(t):
        for u in range(2):
            for s in range(depth):
                b_off = peel + u + lead[s]
                stages[s](blocks - 1 - b_off - 2 * t, b_off % 2, None)

    def body(t, carry):
        for r in range(PAIRS_PER_TRIP):
            step_pair(PAIRS_PER_TRIP * t + r)
        return carry

    pairs = i * (tiles.diag // 2)
    lax.fori_loop(0, pairs // PAIRS_PER_TRIP, body, 0)
    if (tiles.diag // 2) % PAIRS_PER_TRIP:
        for r in range(PAIRS_PER_TRIP - 1):
            pl.when(r < pairs % PAIRS_PER_TRIP)(
                functools.partial(step_pair, pairs - pairs % PAIRS_PER_TRIP + r))
    for e in range(depth - 1):
        for s, stage in enumerate(stages):
            j = depth - 2 - e - lead[s]
            if j >= 0:
                stage(j, (j + 1) % 2, None)


def _hidden(tiles, i, j, q_lo, visible_when_equal):
    shape = (tiles.k, tiles.q - q_lo)
    key = j * tiles.k + lax.broadcasted_iota(jnp.int32, shape, 0)
    qry = i * tiles.q + q_lo + lax.broadcasted_iota(jnp.int32, shape, 1)
    return key > qry if visible_when_equal else key >= qry


def _key_block(tiles, j):
    return pl.ds(pl.multiple_of(j * tiles.k, tiles.k), tiles.k)


def _merge_heads(o0, o1):
    return jnp.concatenate([o0, o1], axis=0).T


class _Slots:
    def __init__(self, first, second):
        self._refs = (first, second)

    def __getitem__(self, idx):
        return self._refs[idx[0]][idx[1:]]

    def __setitem__(self, idx, value):
        self._refs[idx[0]][idx[1:]] = value


def _slot_pair(shape, dtype):
    return [pltpu.VMEM(shape, dtype)] * 2


def _tile_pair(tiles, dtype):
    return _slot_pair((2, tiles.k, tiles.q), dtype)


def _sb_stages(tiles, i, qt_ref, k_ref, vt_ref, o_ref, z0, z1, e0, e1, total0, total1, acc_ref, later_ref):
    z_ref, e_ref, total_ref = _Slots(z0, z1), _Slots(e0, e1), _Slots(total0, total1)
    qt = qt_ref[...]
    dim = lax.broadcasted_iota(jnp.int32, (LANES, tiles.q), 0)
    qh = [jnp.where(dim < SB_HEAD_DIM, qt, jnp.zeros_like(qt)),
          jnp.where(dim >= SB_HEAD_DIM, qt, jnp.zeros_like(qt))]
    r = lax.broadcasted_iota(jnp.int32, (tiles.k, tiles.k), 0)
    c = lax.broadcasted_iota(jnp.int32, (tiles.k, tiles.k), 1)
    tri = jnp.where(c > r, 1.0, 0.0).astype(BF16)
    acc_ref[...] = jnp.zeros_like(acc_ref)
    later_ref[...] = jnp.zeros_like(later_ref)

    def scores(j, slot, q_lo):
        k = k_ref[_key_block(tiles, j), :]
        lo = q_lo or 0
        for h in range(2):
            z = _dot(k, qh[h][:, lo:])
            if q_lo is not None:
                z = jnp.where(_hidden(tiles, i, j, lo, visible_when_equal=False), NEG_BIG, z)
                if lo:
                    z_ref[slot, h, :, :lo] = jnp.full((tiles.k, lo), NEG_BIG, F32)
            z_ref[slot, h, :, lo:] = z

    def within_block(j, slot, q_lo):
        lo = q_lo or 0
        for h in range(2):
            z = z_ref[slot, h, :, lo:]
            neg_abs = lax.bitcast_convert_type(
                lax.bitcast_convert_type(z, jnp.uint32) | jnp.uint32(0x80000000), F32)
            softplus = jnp.maximum(z, 0.0) + jnp.log(1.0 + jnp.exp(neg_abs))
            later = _dot(tri, softplus.astype(BF16))
            if lo:
                e_ref[slot, h, :, :lo] = jnp.full((tiles.k, lo), NEG_BIG, F32)
                total_ref[slot, h, :, :lo] = jnp.zeros((1, lo), F32)
            e_ref[slot, h, :, lo:] = (z - softplus) - later
            total_ref[slot, h, :, lo:] = later[0:1, :] + softplus[0:1, :]

    def accumulate(j, slot, q_lo):
        lo = q_lo or 0
        for h in range(2):
            w = jnp.exp(e_ref[slot, h, :, lo:])
            vt = vt_ref[h * SB_HEAD_DIM:(h + 1) * SB_HEAD_DIM, _key_block(tiles, j)]
            acc_ref[h, :, lo:] += _dot(vt, w.astype(BF16)) * jnp.exp(-later_ref[h, :, lo:])
            later_ref[h, :, lo:] += total_ref[slot, h, :, lo:]

    def finish():
        o_ref[...] = _merge_heads(acc_ref[0], acc_ref[1]).astype(o_ref.dtype)

    return [scores, within_block, accumulate], finish


def _sb_scratch(tiles):
    return (_tile_pair(tiles, F32) + _tile_pair(tiles, F32) + _slot_pair((2, 1, tiles.q), F32)
            + [pltpu.VMEM((2, SB_HEAD_DIM, tiles.q), F32), pltpu.VMEM((2, 1, tiles.q), F32)])


def _mla_stages(tiles, i, qt_ref, k_ref, vt_ref, o_ref, s0, s1, acc_ref, m_ref):
    s_ref = _Slots(s0, s1)
    c2 = (MLA_NOPE_DIM + MLA_ROPE_DIM) ** -0.5 * LOG2E
    qh = [qt_ref[h * MLA_QK_PAD:(h + 1) * MLA_QK_PAD, :] for h in range(2)]
    acc_ref[...] = jnp.zeros_like(acc_ref)
    m_ref[...] = jnp.full_like(m_ref, NEG_BIG)

    def scores(j, slot, q_lo):
        lo = q_lo or 0
        for h in range(2):
            s = _dot(k_ref[_key_block(tiles, j), h * MLA_QK_PAD:(h + 1) * MLA_QK_PAD], qh[h][:, lo:])
            if q_lo is not None:
                s = jnp.where(_hidden(tiles, i, j, lo, visible_when_equal=True), NEG_BIG, s)
                if lo:
                    s_ref[slot, h, :, :lo] = jnp.full((tiles.k, lo), NEG_BIG, F32)
            s_ref[slot, h, :, lo:] = s

    def accumulate(j, slot, q_lo):
        lo = q_lo or 0
        for h in range(2):
            s = s_ref[slot, h, :, lo:]
            m = m_ref[h, :, lo:]
            m_new = jnp.maximum(m, jnp.max(s, axis=0, keepdims=True))
            p = jnp.exp2((s - m_new) * c2)
            rescale = jnp.exp2((m - m_new) * c2)
            m_ref[h, :, lo:] = m_new
            vt = vt_ref[h * _VT_PAD:h * _VT_PAD + _VT_ROWS, _key_block(tiles, j)]
            acc_ref[h, :, lo:] = rescale * acc_ref[h, :, lo:] + _dot(vt, p.astype(BF16))

    def finish():
        out = [acc_ref[h, :MLA_V_DIM, :] / acc_ref[h, MLA_V_DIM:MLA_V_DIM + 1, :] for h in range(2)]
        o_ref[...] = _merge_heads(*out).astype(o_ref.dtype)

    return [scores, accumulate], finish


def _mla_scratch(tiles):
    return _tile_pair(tiles, F32) + [pltpu.VMEM((2, _VT_ROWS, tiles.q), F32), pltpu.VMEM((2, 1, tiles.q), F32)]


def _attn_kernel(build_stages, tiles, qt_ref, k_ref, vt_ref, o_ref, *scratch):
    def query_block(i, carry):
        block = pl.ds(pl.multiple_of(i * tiles.q, tiles.q), tiles.q)
        stages, finish = build_stages(tiles, i, qt_ref.at[:, block], k_ref, vt_ref, o_ref.at[block], *scratch)
        _sweep_keys(i, stages, tiles)
        finish()
        return carry

    lax.fori_loop(0, qt_ref.shape[1] // tiles.q, query_block, 0)


def _attn_call(build_stages, scratch, name, q_t, k, v_t, qk_width, vt_rows, batch, seq, width):
    tiles = ATTN_TILES
    return pl.pallas_call(
        functools.partial(_attn_kernel, build_stages, tiles),
        out_shape=jax.ShapeDtypeStruct((k.shape[0], width), BF16),
        grid=(batch, HEAD_PAIRS),
        in_specs=[pl.BlockSpec((qk_width, seq), lambda b, p: (p, b)),
                  pl.BlockSpec((seq, qk_width), lambda b, p: (b, p)),
                  pl.BlockSpec((vt_rows, seq), lambda b, p: (p, b))],
        out_specs=pl.BlockSpec((seq, LANES), lambda b, p: (b, p)),
        scratch_shapes=scratch(tiles),
        compiler_params=_params("parallel", "parallel"),
        name=name,
    )(q_t, k, v_t)


def _sb_attn(q_t, k, v_t, batch, seq):
    return _attn_call(_sb_stages, _sb_scratch, "sb_attn", q_t, k, v_t, LANES, LANES, batch, seq, SB_WIDTH)


def _mla_attn(q_t, k, v_t, batch, seq):
    return _attn_call(_mla_stages, _mla_scratch, "mla_attn", q_t, k, v_t, 2 * MLA_QK_PAD, 2 * _VT_PAD,
                      batch, seq, MLA_WIDTH)


HALO = 8


def _conv_mixer_kernel(tiles_per_seq, x_ref, win_ref, cw_ref, wout_ref, g_ref, b_ref, o_ref, tail_ref):
    i = pl.program_id(0)
    x = x_ref[...]
    xb = x.astype(BF16)
    gate_b = _dot(xb, win_ref[:, :D_MODEL])
    u = _dot(xb, win_ref[:, D_MODEL:2 * D_MODEL]) * _dot(xb, win_ref[:, 2 * D_MODEL:])

    @pl.when(i % tiles_per_seq == 0)
    def _():
        tail_ref[...] = jnp.zeros_like(tail_ref)

    prev = tail_ref[...]
    tail_ref[...] = u[ROW_TILE - HALO:, :]
    ext = jnp.concatenate([prev, u], axis=0)
    cw = cw_ref[...]
    conv = (cw[0:1, :] * ext[HALO - 2:HALO - 2 + ROW_TILE, :]
            + cw[1:2, :] * ext[HALO - 1:HALO - 1 + ROW_TILE, :]
            + cw[2:3, :] * u)
    m = _dot((gate_b * conv).astype(BF16), wout_ref[...])
    o_ref[...] = _layer_norm(ALPHA * x + m, g_ref[...], b_ref[...])


def _conv_mixer(x, w_in, conv_w, w_out, g, b, seq):
    n = x.shape[0]
    row = pl.BlockSpec((ROW_TILE, D_MODEL), lambda i: (i, 0))
    return pl.pallas_call(
        functools.partial(_conv_mixer_kernel, seq // ROW_TILE),
        out_shape=jax.ShapeDtypeStruct((n, D_MODEL), F32),
        grid=(n // ROW_TILE,),
        in_specs=[row, _resident(w_in.shape), _resident(conv_w.shape), _resident(w_out.shape),
                  _resident(g.shape), _resident(b.shape)],
        out_specs=row,
        scratch_shapes=[pltpu.VMEM((HALO, D_MODEL), F32)],
        compiler_params=_params("arbitrary"),
        name="conv_mixer",
    )(x, w_in, conv_w, w_out, g, b)


def _rope_tables(seq):
    half = MLA_ROPE_DIM // 2
    inv_freq = ROPE_THETA ** (-jnp.arange(0, MLA_ROPE_DIM, 2, dtype=F32) / MLA_ROPE_DIM)
    ang = jnp.arange(seq, dtype=F32)[:, None] * inv_freq[None, :]
    cos, sin = jnp.cos(ang), jnp.sin(ang)
    ones = jnp.ones((seq, MLA_NOPE_DIM), F32)
    pad = jnp.zeros((seq, MLA_QK_PAD - MLA_NOPE_DIM - 2 * half), F32)
    cos_t = jnp.concatenate([ones, cos, cos, pad], axis=1)
    sin_t = jnp.concatenate([0.0 * ones, -sin, sin, pad], axis=1)
    return cos_t, sin_t


def _even_weights(w_in, w_uq, w_ukv):
    half = MLA_ROPE_DIM // 2
    qk = MLA_NOPE_DIM + MLA_ROPE_DIM
    zeros = lambda r, c: jnp.zeros((r, c), F32)
    sb_q, sb_k, sb_v = (slice(g * SB_WIDTH, (g + 1) * SB_WIDTH) for g in range(3))
    lat = slice(3 * SB_WIDTH, 3 * SB_WIDTH + MLA_Q_RANK + MLA_KV_RANK)
    w_kr = w_in[:, lat.stop:]
    k1, k2 = w_kr[:, :half], w_kr[:, half:]
    nope_pad = zeros(D_MODEL, MLA_NOPE_DIM)
    tail_pad = zeros(D_MODEL, MLA_QK_PAD - qk)
    wx = jnp.concatenate([w_in[:, sb_k], w_in[:, lat], nope_pad, k1, k2, tail_pad,
                          nope_pad, k2, k1, tail_pad], axis=1)
    wsqv = jnp.concatenate([w_in[:, sb_q], w_in[:, sb_v]], axis=1).T
    wq = w_uq.reshape(MLA_Q_RANK, MLA_HEADS, qk)
    q_nope, q1, q2 = wq[..., :MLA_NOPE_DIM], wq[..., MLA_NOPE_DIM:MLA_NOPE_DIM + half], wq[..., MLA_NOPE_DIM + half:]
    zq = lambda c: jnp.zeros((MLA_Q_RANK, MLA_HEADS, c), F32)
    wqa = jnp.concatenate([q_nope, q1, q2, zq(MLA_QK_PAD - qk)], axis=-1).reshape(MLA_Q_RANK, _QK_WIDTH)
    wqb = jnp.concatenate([zq(MLA_NOPE_DIM), q2, q1, zq(MLA_QK_PAD - qk)], axis=-1).reshape(MLA_Q_RANK, _QK_WIDTH)
    wkv = w_ukv.reshape(MLA_KV_RANK, MLA_HEADS, MLA_NOPE_DIM + MLA_V_DIM)
    wk = jnp.concatenate([wkv[..., :MLA_NOPE_DIM],
                          jnp.zeros((MLA_KV_RANK, MLA_HEADS, MLA_QK_PAD - MLA_NOPE_DIM), F32)],
                         axis=-1).reshape(MLA_KV_RANK, _QK_WIDTH)
    wv = jnp.concatenate([wkv[..., MLA_NOPE_DIM:],
                          jnp.zeros((MLA_KV_RANK, MLA_HEADS, _VT_PAD - MLA_V_DIM), F32)],
                         axis=-1).reshape(MLA_KV_RANK, MLA_HEADS * _VT_PAD).T
    return [w.astype(BF16) for w in (wx, wsqv, wqa.T, wqb.T, wk, wv)]


def kernel(x, ln_g, ln_b, ffn_w_gate, ffn_w_up, ffn_w_down, mix_w_in, mla_q_norm_g, mla_w_uq,
           mla_kv_norm_g, mla_w_ukv, mix_w_out, conv_w_in, conv_w, conv_w_out):
    batch, seq, _ = x.shape
    assert seq % FFN_ROW_TILE == 0 and seq % ROW_TILE == 0
    for tiles in (ATTN_TILES,):
        assert seq % tiles.q == 0 and tiles.q % (2 * tiles.k) == 0
    cos_t, sin_t = _rope_tables(seq)
    h = x.reshape(batch * seq, D_MODEL)
    vec = lambda a: a.reshape(1, -1)
    for layer in range(DEPTH):
        j = layer // 2
        ln = lambda s: (vec(ln_g[layer, s]), vec(ln_b[layer, s]))
        ffn_w = lambda s: (ffn_w_gate[layer, s].astype(BF16), ffn_w_up[layer, s].astype(BF16),
                           ffn_w_down[layer, s].astype(BF16))
        h = _ffn(h, *ffn_w(0), *ln(0))
        if layer % 2 == 0:
            wx, wsqv, wqa, wqb, wk, wv = _even_weights(mix_w_in[j], mla_w_uq[j], mla_w_ukv[j])
            sb_k, sb_qt, sb_vt, q_t, k, v_t = _proj_even(h, wx, wsqv, vec(mla_q_norm_g[j]), wqa, wqb,
                                                         vec(mla_kv_norm_g[j]), wk, wv, cos_t, sin_t, seq)
            out_sb = _sb_attn(sb_qt, sb_k, sb_vt, batch, seq)
            out_mla = _mla_attn(q_t, k, v_t, batch, seq)
            h = _attn_out_ffn(h, out_sb, out_mla, mix_w_out[j].astype(BF16), *ln(1), *ffn_w(1), *ln(2))
        else:
            h = _conv_mixer(h, conv_w_in[j].astype(BF16), conv_w[j], conv_w_out[j].astype(BF16), *ln(1), seq)
            h = _ffn(h, *ffn_w(1), *ln(2))
    return h.reshape(batch, seq, D_MODEL)
```

```python
import functools
from typing import NamedTuple

import jax
import jax.numpy as jnp
from jax import lax
from jax.experimental import pallas as pl
from jax.experimental.pallas import tpu as pltpu

D_MODEL = 1024
DEPTH = 4
SB_HEADS = 8
SB_HEAD_DIM = 64
MLA_HEADS = 8
MLA_NOPE_DIM = 64
MLA_ROPE_DIM = 32
MLA_V_DIM = 64
MLA_Q_RANK = 768
MLA_KV_RANK = 256
ROPE_THETA = 10000.0
CONV_K = 3
D_FF = 2816
LN_EPS = 1e-5
RMS_EPS = 1e-6
ALPHA = (2 * DEPTH) ** 0.25
SB_WIDTH = SB_HEADS * SB_HEAD_DIM
MLA_WIDTH = MLA_HEADS * MLA_V_DIM

LANES = 128
HEAD_PAIRS = SB_HEADS // 2
MLA_QK_PAD = LANES
_VT_PAD = LANES
BF16_SUBLANES = 16
_VT_ROWS = MLA_V_DIM + BF16_SUBLANES
FF_CHUNK = 256
ROW_TILE = 1024
FFN_ROW_TILE = 1024


class _Tiling(NamedTuple):
    q: int
    k: int

    @property
    def diag(self):
        return self.q // self.k


ATTN_TILES = _Tiling(q=512, k=256)
VMEM_LIMIT = 56 * 1024 * 1024
PAIRS_PER_TRIP = 4

BF16 = jnp.bfloat16
F32 = jnp.float32
NEG_BIG = -1e30
LOG2E = 1.4426950408889634


def _dot(a, b):
    return jnp.dot(a, b, preferred_element_type=F32)


def _dot_nt(a, b):
    return lax.dot_general(a, b, (((1,), (1,)), ((), ())), preferred_element_type=F32)


def _layer_norm(y, g, b):
    mu = jnp.mean(y, axis=-1, keepdims=True)
    d = y - mu
    var = jnp.mean(d * d, axis=-1, keepdims=True)
    return d * lax.rsqrt(var + LN_EPS) * g + b


def _rms_norm(c, g):
    return c * lax.rsqrt(jnp.mean(c * c, axis=-1, keepdims=True) + RMS_EPS) * g


def _params(*sem):
    return pltpu.CompilerParams(dimension_semantics=sem, vmem_limit_bytes=VMEM_LIMIT)


def _resident(shape):
    nd = len(shape)
    return pl.BlockSpec(shape, lambda *_: (0,) * nd, pipeline_mode=pl.Buffered(1))


def _half_step_ffn(x, wg_ref, wu_ref, wd_ref, g_ref, b_ref, h_ref):
    xb = x.astype(BF16)
    for c in range(D_FF // FF_CHUNK):
        sl = slice(c * FF_CHUNK, (c + 1) * FF_CHUNK)
        gate = _dot(xb, wg_ref[:, sl])
        up = _dot(xb, wu_ref[:, sl])
        h_ref[:, sl] = (gate * (1.0 / (1.0 + jnp.exp(-gate))) * up).astype(BF16)
    f = _dot(h_ref[...], wd_ref[...])
    return _layer_norm(ALPHA * x + 0.5 * f, g_ref[...], b_ref[...])


def _ffn_kernel(x_ref, wg_ref, wu_ref, wd_ref, g_ref, b_ref, o_ref, h_ref):
    o_ref[...] = _half_step_ffn(x_ref[...], wg_ref, wu_ref, wd_ref, g_ref, b_ref, h_ref)


def _attn_out_ffn_kernel(x_ref, sb_ref, mla_ref, wo_ref, go_ref, bo_ref,
                         wg_ref, wu_ref, wd_ref, g_ref, b_ref, o_ref, h_ref):
    m = _dot(sb_ref[...], wo_ref[:SB_WIDTH, :]) + _dot(mla_ref[...], wo_ref[SB_WIDTH:, :])
    x = _layer_norm(ALPHA * x_ref[...] + m, go_ref[...], bo_ref[...])
    o_ref[...] = _half_step_ffn(x, wg_ref, wu_ref, wd_ref, g_ref, b_ref, h_ref)


def _ffn_call(body, name, rows, residents):
    n = rows[0][0].shape[0]
    row = lambda w: pl.BlockSpec((FFN_ROW_TILE, w), lambda i: (i, 0))
    return pl.pallas_call(
        body,
        out_shape=jax.ShapeDtypeStruct((n, D_MODEL), F32),
        grid=(n // FFN_ROW_TILE,),
        in_specs=[row(w) for _, w in rows] + [_resident(a.shape) for a in residents],
        out_specs=row(D_MODEL),
        scratch_shapes=[pltpu.VMEM((FFN_ROW_TILE, D_FF), BF16)],
        compiler_params=_params("parallel"),
        name=name,
    )(*[a for a, _ in rows], *residents)


def _ffn(x, wg, wu, wd, g, b):
    return _ffn_call(_ffn_kernel, "ffn", [(x, D_MODEL)], [wg, wu, wd, g, b])


def _attn_out_ffn(x, sb, mla, wo, go, bo, wg, wu, wd, g, b):
    return _ffn_call(_attn_out_ffn_kernel, "attn_out_ffn",
                     [(x, D_MODEL), (sb, SB_WIDTH), (mla, MLA_WIDTH)], [wo, go, bo, wg, wu, wd, g, b])


_X_K = SB_WIDTH
_X_CQ = _X_K + MLA_Q_RANK
_X_CKV = _X_CQ + MLA_KV_RANK
_X_KRA = _X_CKV + LANES
_X_KRB = _X_KRA + LANES
_QK_WIDTH = MLA_HEADS * MLA_QK_PAD


def _proj_even_kernel(x_ref, wx_ref, wsqv_ref, gq_ref, wqa_ref, wqb_ref, gkv_ref, wk_ref, wv_ref,
                      cos_ref, sin_ref, cos_t_ref, sin_t_ref,
                      sbk_ref, sbq_ref, sbv_ref, q_ref, k_ref, v_ref):
    xb = x_ref[...].astype(BF16)
    qv = _dot_nt(wsqv_ref[...], xb)
    sbq_ref[...] = (qv[:SB_WIDTH, :] * (SB_HEAD_DIM ** -0.5)).astype(BF16)
    sbv_ref[...] = qv[SB_WIDTH:, :].astype(BF16)
    sbk_ref[...] = _dot(xb, wx_ref[:, :_X_K]).astype(BF16)
    cq = _rms_norm(_dot(xb, wx_ref[:, _X_K:_X_CQ]), gq_ref[...]).astype(BF16)
    cos_t = cos_t_ref[...]
    sin_t = sin_t_ref[...]
    qa = _dot_nt(wqa_ref[...], cq)
    qb = _dot_nt(wqb_ref[...], cq)
    for h in range(MLA_HEADS):
        sl = slice(h * MLA_QK_PAD, (h + 1) * MLA_QK_PAD)
        q_ref[sl, :] = (qa[sl, :] * cos_t + qb[sl, :] * sin_t).astype(BF16)
    ckv = _rms_norm(_dot(xb, wx_ref[:, _X_CQ:_X_CKV]), gkv_ref[...]).astype(BF16)
    kr_pair = _dot(xb, wx_ref[:, _X_CKV:_X_KRB])
    kr = kr_pair[:, :LANES] * cos_ref[...] + kr_pair[:, LANES:] * sin_ref[...]
    k_nope = _dot(ckv, wk_ref[...])
    for h in range(MLA_HEADS):
        sl = slice(h * MLA_QK_PAD, (h + 1) * MLA_QK_PAD)
        k_ref[:, sl] = (k_nope[:, sl] + kr).astype(BF16)
    vt = _dot_nt(wv_ref[...], ckv)
    ones_row = lax.broadcasted_iota(jnp.int32, vt.shape, 0) % _VT_PAD == MLA_V_DIM
    v_ref[...] = jnp.where(ones_row, 1.0, vt).astype(BF16)


def _proj_even(x, wx, wsqv, gq, wqa, wqb, gkv, wk, wv, cos, sin, seq):
    n = x.shape[0]
    tiles_per_seq = seq // ROW_TILE
    row = lambda w: pl.BlockSpec((ROW_TILE, w), lambda i: (i, 0))
    col = lambda h: pl.BlockSpec((h, ROW_TILE), lambda i: (0, i))
    pos = pl.BlockSpec((ROW_TILE, LANES), lambda i: (i % tiles_per_seq, 0))
    pos_t = pl.BlockSpec((LANES, ROW_TILE), lambda i: (0, i % tiles_per_seq))
    rows = lambda w: jax.ShapeDtypeStruct((n, w), BF16)
    cols = lambda h: jax.ShapeDtypeStruct((h, n), BF16)
    return pl.pallas_call(
        _proj_even_kernel,
        out_shape=(rows(SB_WIDTH), cols(SB_WIDTH), cols(SB_WIDTH),
                   cols(_QK_WIDTH), rows(_QK_WIDTH), cols(MLA_HEADS * _VT_PAD)),
        grid=(n // ROW_TILE,),
        in_specs=[row(D_MODEL), _resident(wx.shape), _resident(wsqv.shape), _resident(gq.shape),
                  _resident(wqa.shape), _resident(wqb.shape), _resident(gkv.shape),
                  _resident(wk.shape), _resident(wv.shape), pos, pos, pos_t, pos_t],
        out_specs=(row(SB_WIDTH), col(SB_WIDTH), col(SB_WIDTH),
                   col(_QK_WIDTH), row(_QK_WIDTH), col(MLA_HEADS * _VT_PAD)),
        compiler_params=_params("parallel"),
        name="proj_even",
    )(x, wx, wsqv, gq, wqa, wqb, gkv, wk, wv, cos, sin, cos.T, sin.T)


def _sweep_keys(i, stages, tiles):
    depth = len(stages)
    blocks = (i + 1) * tiles.diag
    lead = [depth - 1 - s for s in range(depth)]
    peel = max(0, tiles.diag - depth + 1)
    for n in range(1 - depth, peel):
        for s, stage in enumerate(stages):
            b = n + lead[s]
            if b >= 0:
                stage(blocks - 1 - b, b % 2, (tiles.diag - 1 - b) * tiles.k if b < tiles.diag else None)

    def step_pair(t):
        for u in range(2):
            for s in range(depth):
                b_off = peel + u + lead[s]
                stages[s](blocks - 1 - b_off - 2 * t, b_off % 2, None)

    def body(t, carry):
        for r in range(PAIRS_PER_TRIP):
            step_pair(PAIRS_PER_TRIP * t + r)
        return carry

    pairs = i * (tiles.diag // 2)
    lax.fori_loop(0, pairs // PAIRS_PER_TRIP, body, 0)
    if (tiles.diag // 2) % PAIRS_PER_TRIP:
        for r in range(PAIRS_PER_TRIP - 1):
            pl.when(r < pairs % PAIRS_PER_TRIP)(
                functools.partial(step_pair, pairs - pairs % PAIRS_PER_TRIP + r))
    for e in range(depth - 1):
        for s, stage in enumerate(stages):
            j = depth - 2 - e - lead[s]
            if j >= 0:
                stage(j, (j + 1) % 2, None)


def _hidden(tiles, i, j, q_lo, visible_when_equal):
    shape = (tiles.k, tiles.q - q_lo)
    key = j * tiles.k + lax.broadcasted_iota(jnp.int32, shape, 0)
    qry = i * tiles.q + q_lo + lax.broadcasted_iota(jnp.int32, shape, 1)
    return key > qry if visible_when_equal else key >= qry


def _key_block(tiles, j):
    return pl.ds(pl.multiple_of(j * tiles.k, tiles.k), tiles.k)


def _merge_heads(o0, o1):
    return jnp.concatenate([o0, o1], axis=0).T


class _Slots:
    def __init__(self, first, second):
        self._refs = (first, second)

    def __getitem__(self, idx):
        return self._refs[idx[0]][idx[1:]]

    def __setitem__(self, idx, value):
        self._refs[idx[0]][idx[1:]] = value


def _slot_pair(shape, dtype):
    return [pltpu.VMEM(shape, dtype)] * 2


def _tile_pair(tiles, dtype):
    return _slot_pair((2, tiles.k, tiles.q), dtype)


def _sb_stages(tiles, i, qt_ref, k_ref, vt_ref, o_ref, z0, z1, e0, e1, total0, total1, acc_ref, later_ref):
    z_ref, e_ref, total_ref = _Slots(z0, z1), _Slots(e0, e1), _Slots(total0, total1)
    qt = qt_ref[...]
    dim = lax.broadcasted_iota(jnp.int32, (LANES, tiles.q), 0)
    qh = [jnp.where(dim < SB_HEAD_DIM, qt, jnp.zeros_like(qt)),
          jnp.where(dim >= SB_HEAD_DIM, qt, jnp.zeros_like(qt))]
    r = lax.broadcasted_iota(jnp.int32, (tiles.k, tiles.k), 0)
    c = lax.broadcasted_iota(jnp.int32, (tiles.k, tiles.k), 1)
    tri = jnp.where(c > r, 1.0, 0.0).astype(BF16)
    acc_ref[...] = jnp.zeros_like(acc_ref)
    later_ref[...] = jnp.zeros_like(later_ref)

    def scores(j, slot, q_lo):
        k = k_ref[_key_block(tiles, j), :]
        lo = q_lo or 0
        for h in range(2):
            z = _dot(k, qh[h][:, lo:])
            if q_lo is not None:
                z = jnp.where(_hidden(tiles, i, j, lo, visible_when_equal=False), NEG_BIG, z)
                if lo:
                    z_ref[slot, h, :, :lo] = jnp.full((tiles.k, lo), NEG_BIG, F32)
            z_ref[slot, h, :, lo:] = z

    def within_block(j, slot, q_lo):
        lo = q_lo or 0
        for h in range(2):
            z = z_ref[slot, h, :, lo:]
            neg_abs = lax.bitcast_convert_type(
                lax.bitcast_convert_type(z, jnp.uint32) | jnp.uint32(0x80000000), F32)
            softplus = jnp.maximum(z, 0.0) + jnp.log(1.0 + jnp.exp(neg_abs))
            later = _dot(tri, softplus.astype(BF16))
            if lo:
                e_ref[slot, h, :, :lo] = jnp.full((tiles.k, lo), NEG_BIG, F32)
                total_ref[slot, h, :, :lo] = jnp.zeros((1, lo), F32)
            e_ref[slot, h, :, lo:] = (z - softplus) - later
            total_ref[slot, h, :, lo:] = later[0:1, :] + softplus[0:1, :]

    def accumulate(j, slot, q_lo):
        lo = q_lo or 0
        for h in range(2):
            w = jnp.exp(e_ref[slot, h, :, lo:])
            vt = vt_ref[h * SB_HEAD_DIM:(h + 1) * SB_HEAD_DIM, _key_block(tiles, j)]
            acc_ref[h, :, lo:] += _dot(vt, w.astype(BF16)) * jnp.exp(-later_ref[h, :, lo:])
            later_ref[h, :, lo:] += total_ref[slot, h, :, lo:]

    def finish():
        o_ref[...] = _merge_heads(acc_ref[0], acc_ref[1]).astype(o_ref.dtype)

    return [scores, within_block, accumulate], finish


def _sb_scratch(tiles):
    return (_tile_pair(tiles, F32) + _tile_pair(tiles, F32) + _slot_pair((2, 1, tiles.q), F32)
            + [pltpu.VMEM((2, SB_HEAD_DIM, tiles.q), F32), pltpu.VMEM((2, 1, tiles.q), F32)])


def _mla_stages(tiles, i, qt_ref, k_ref, vt_ref, o_ref, s0, s1, acc_ref, m_ref):
    s_ref = _Slots(s0, s1)
    c2 = (MLA_NOPE_DIM + MLA_ROPE_DIM) ** -0.5 * LOG2E
    qh = [qt_ref[h * MLA_QK_PAD:(h + 1) * MLA_QK_PAD, :] for h in range(2)]
    acc_ref[...] = jnp.zeros_like(acc_ref)
    m_ref[...] = jnp.full_like(m_ref, NEG_BIG)

    def scores(j, slot, q_lo):
        lo = q_lo or 0
        for h in range(2):
            s = _dot(k_ref[_key_block(tiles, j), h * MLA_QK_PAD:(h + 1) * MLA_QK_PAD], qh[h][:, lo:])
            if q_lo is not None:
                s = jnp.where(_hidden(tiles, i, j, lo, visible_when_equal=True), NEG_BIG, s)
                if lo:
                    s_ref[slot, h, :, :lo] = jnp.full((tiles.k, lo), NEG_BIG, F32)
            s_ref[slot, h, :, lo:] = s

    def accumulate(j, slot, q_lo):
        lo = q_lo or 0
        for h in range(2):
            s = s_ref[slot, h, :, lo:]
            m = m_ref[h, :, lo:]
            m_new = jnp.maximum(m, jnp.max(s, axis=0, keepdims=True))
            p = jnp.exp2((s - m_new) * c2)
            rescale = jnp.exp2((m - m_new) * c2)
            m_ref[h, :, lo:] = m_new
            vt = vt_ref[h * _VT_PAD:h * _VT_PAD + _VT_ROWS, _key_block(tiles, j)]
            acc_ref[h, :, lo:] = rescale * acc_ref[h, :, lo:] + _dot(vt, p.astype(BF16))

    def finish():
        out = [acc_ref[h, :MLA_V_DIM, :] / acc_ref[h, MLA_V_DIM:MLA_V_DIM + 1, :] for h in range(2)]
        o_ref[...] = _merge_heads(*out).astype(o_ref.dtype)

    return [scores, accumulate], finish


def _mla_scratch(tiles):
    return _tile_pair(tiles, F32) + [pltpu.VMEM((2, _VT_ROWS, tiles.q), F32), pltpu.VMEM((2, 1, tiles.q), F32)]


def _attn_kernel(build_stages, tiles, qt_ref, k_ref, vt_ref, o_ref, *scratch):
    def query_block(i, carry):
        block = pl.ds(pl.multiple_of(i * tiles.q, tiles.q), tiles.q)
        stages, finish = build_stages(tiles, i, qt_ref.at[:, block], k_ref, vt_ref, o_ref.at[block], *scratch)
        _sweep_keys(i, stages, tiles)
        finish()
        return carry

    lax.fori_loop(0, qt_ref.shape[1] // tiles.q, query_block, 0)


def _attn_call(build_stages, scratch, name, q_t, k, v_t, qk_width, vt_rows, batch, seq, width):
    tiles = ATTN_TILES
    return pl.pallas_call(
        functools.partial(_attn_kernel, build_stages, tiles),
        out_shape=jax.ShapeDtypeStruct((k.shape[0], width), BF16),
        grid=(batch, HEAD_PAIRS),
        in_specs=[pl.BlockSpec((qk_width, seq), lambda b, p: (p, b)),
                  pl.BlockSpec((seq, qk_width), lambda b, p: (b, p)),
                  pl.BlockSpec((vt_rows, seq), lambda b, p: (p, b))],
        out_specs=pl.BlockSpec((seq, LANES), lambda b, p: (b, p)),
        scratch_shapes=scratch(tiles),
        compiler_params=_params("parallel", "parallel"),
        name=name,
    )(q_t, k, v_t)


def _sb_attn(q_t, k, v_t, batch, seq):
    return _attn_call(_sb_stages, _sb_scratch, "sb_attn", q_t, k, v_t, LANES, LANES, batch, seq, SB_WIDTH)


def _mla_attn(q_t, k, v_t, batch, seq):
    return _attn_call(_mla_stages, _mla_scratch, "mla_attn", q_t, k, v_t, 2 * MLA_QK_PAD, 2 * _VT_PAD,
                      batch, seq, MLA_WIDTH)


HALO = 8


def _conv_mixer_kernel(tiles_per_seq, x_ref, win_ref, cw_ref, wout_ref, g_ref, b_ref, o_ref, tail_ref):
    i = pl.program_id(0)
    x = x_ref[...]
    xb = x.astype(BF16)
    gate_b = _dot(xb, win_ref[:, :D_MODEL])
    u = _dot(xb, win_ref[:, D_MODEL:2 * D_MODEL]) * _dot(xb, win_ref[:, 2 * D_MODEL:])

    @pl.when(i % tiles_per_seq == 0)
    def _():
        tail_ref[...] = jnp.zeros_like(tail_ref)

    prev = tail_ref[...]
    tail_ref[...] = u[ROW_TILE - HALO:, :]
    ext = jnp.concatenate([prev, u], axis=0)
    cw = cw_ref[...]
    conv = (cw[0:1, :] * ext[HALO - 2:HALO - 2 + ROW_TILE, :]
            + cw[1:2, :] * ext[HALO - 1:HALO - 1 + ROW_TILE, :]
            + cw[2:3, :] * u)
    m = _dot((gate_b * conv).astype(BF16), wout_ref[...])
    o_ref[...] = _layer_norm(ALPHA * x + m, g_ref[...], b_ref[...])


def _conv_mixer(x, w_in, conv_w, w_out, g, b, seq):
    n = x.shape[0]
    row = pl.BlockSpec((ROW_TILE, D_MODEL), lambda i: (i, 0))
    return pl.pallas_call(
        functools.partial(_conv_mixer_kernel, seq // ROW_TILE),
        out_shape=jax.ShapeDtypeStruct((n, D_MODEL), F32),
        grid=(n // ROW_TILE,),
        in_specs=[row, _resident(w_in.shape), _resident(conv_w.shape), _resident(w_out.shape),
                  _resident(g.shape), _resident(b.shape)],
        out_specs=row,
        scratch_shapes=[pltpu.VMEM((HALO, D_MODEL), F32)],
        compiler_params=_params("arbitrary"),
        name="conv_mixer",
    )(x, w_in, conv_w, w_out, g, b)


def _rope_tables(seq):
    half = MLA_ROPE_DIM // 2
    inv_freq = ROPE_THETA ** (-jnp.arange(0, MLA_ROPE_DIM, 2, dtype=F32) / MLA_ROPE_DIM)
    ang = jnp.arange(seq, dtype=F32)[:, None] * inv_freq[None, :]
    cos, sin = jnp.cos(ang), jnp.sin(ang)
    ones = jnp.ones((seq, MLA_NOPE_DIM), F32)
    pad = jnp.zeros((seq, MLA_QK_PAD - MLA_NOPE_DIM - 2 * half), F32)
    cos_t = jnp.concatenate([ones, cos, cos, pad], axis=1)
    sin_t = jnp.concatenate([0.0 * ones, -sin, sin, pad], axis=1)
    return cos_t, sin_t


def _even_weights(w_in, w_uq, w_ukv):
    half = MLA_ROPE_DIM // 2
    qk = MLA_NOPE_DIM + MLA_ROPE_DIM
    zeros = lambda r, c: jnp.zeros((r, c), F32)
    sb_q, sb_k, sb_v = (slice(g * SB_WIDTH, (g + 1) * SB_WIDTH) for g in range(3))
    lat = slice(3 * SB_WIDTH, 3 * SB_WIDTH + MLA_Q_RANK + MLA_KV_RANK)
    w_kr = w_in[:, lat.stop:]
    k1, k2 = w_kr[:, :half], w_kr[:, half:]
    nope_pad = zeros(D_MODEL, MLA_NOPE_DIM)
    tail_pad = zeros(D_MODEL, MLA_QK_PAD - qk)
    wx = jnp.concatenate([w_in[:, sb_k], w_in[:, lat], nope_pad, k1, k2, tail_pad,
                          nope_pad, k2, k1, tail_pad], axis=1)
    wsqv = jnp.concatenate([w_in[:, sb_q], w_in[:, sb_v]], axis=1).T
    wq = w_uq.reshape(MLA_Q_RANK, MLA_HEADS, qk)
    q_nope, q1, q2 = wq[..., :MLA_NOPE_DIM], wq[..., MLA_NOPE_DIM:MLA_NOPE_DIM + half], wq[..., MLA_NOPE_DIM + half:]
    zq = lambda c: jnp.zeros((MLA_Q_RANK, MLA_HEADS, c), F32)
    wqa = jnp.concatenate([q_nope, q1, q2, zq(MLA_QK_PAD - qk)], axis=-1).reshape(MLA_Q_RANK, _QK_WIDTH)
    wqb = jnp.concatenate([zq(MLA_NOPE_DIM), q2, q1, zq(MLA_QK_PAD - qk)], axis=-1).reshape(MLA_Q_RANK, _QK_WIDTH)
    wkv = w_ukv.reshape(MLA_KV_RANK, MLA_HEADS, MLA_NOPE_DIM + MLA_V_DIM)
    wk = jnp.concatenate([wkv[..., :MLA_NOPE_DIM],
                          jnp.zeros((MLA_KV_RANK, MLA_HEADS, MLA_QK_PAD - MLA_NOPE_DIM), F32)],
                         axis=-1).reshape(MLA_KV_RANK, _QK_WIDTH)
    wv = jnp.concatenate([wkv[..., MLA_NOPE_DIM:],
                          jnp.zeros((MLA_KV_RANK, MLA_HEADS, _VT_PAD - MLA_V_DIM), F32)],
                         axis=-1).reshape(MLA_KV_RANK, MLA_HEADS * _VT_PAD).T
    return [w.astype(BF16) for w in (wx, wsqv, wqa.T, wqb.T, wk, wv)]


def kernel(x, ln_g, ln_b, ffn_w_gate, ffn_w_up, ffn_w_down, mix_w_in, mla_q_norm_g, mla_w_uq,
           mla_kv_norm_g, mla_w_ukv, mix_w_out, conv_w_in, conv_w, conv_w_out):
    batch, seq, _ = x.shape
    assert seq % FFN_ROW_TILE == 0 and seq % ROW_TILE == 0
    for tiles in (ATTN_TILES,):
        assert seq % tiles.q == 0 and tiles.q % (2 * tiles.k) == 0
    cos_t, sin_t = _rope_tables(seq)
    h = x.reshape(batch * seq, D_MODEL)
    vec = lambda a: a.reshape(1, -1)
    for layer in range(DEPTH):
        j = layer // 2
        ln = lambda s: (vec(ln_g[layer, s]), vec(ln_b[layer, s]))
        ffn_w = lambda s: (ffn_w_gate[layer, s].astype(BF16), ffn_w_up[layer, s].astype(BF16),
                           ffn_w_down[layer, s].astype(BF16))
        h = _ffn(h, *ffn_w(0), *ln(0))
        if layer % 2 == 0:
            wx, wsqv, wqa, wqb, wk, wv = _even_weights(mix_w_in[j], mla_w_uq[j], mla_w_ukv[j])
            sb_k, sb_qt, sb_vt, q_t, k, v_t = _proj_even(h, wx, wsqv, vec(mla_q_norm_g[j]), wqa, wqb,
                                                         vec(mla_kv_norm_g[j]), wk, wv, cos_t, sin_t, seq)
            out_sb = _sb_attn(sb_qt, sb_k, sb_vt, batch, seq)
            out_mla = _mla_attn(q_t, k, v_t, batch, seq)
            h = _attn_out_ffn(h, out_sb, out_mla, mix_w_out[j].astype(BF16), *ln(1), *ffn_w(1), *ln(2))
        else:
            h = _conv_mixer(h, conv_w_in[j].astype(BF16), conv_w[j], conv_w_out[j].astype(BF16), *ln(1), seq)
            h = _ffn(h, *ffn_w(1), *ln(2))
    return h.reshape(batch, seq, D_MODEL)
```

```python
import functools
from typing import NamedTuple

import jax
import jax.numpy as jnp
from jax import lax
from jax.experimental import pallas as pl
from jax.experimental.pallas import tpu as pltpu

D_MODEL = 1024
DEPTH = 4
SB_HEADS = 8
SB_HEAD_DIM = 64
MLA_HEADS = 8
MLA_NOPE_DIM = 64
MLA_ROPE_DIM = 32
MLA_V_DIM = 64
MLA_Q_RANK = 768
MLA_KV_RANK = 256
ROPE_THETA = 10000.0
CONV_K = 3
D_FF = 2816
LN_EPS = 1e-5
RMS_EPS = 1e-6
ALPHA = (2 * DEPTH) ** 0.25
SB_WIDTH = SB_HEADS * SB_HEAD_DIM
MLA_WIDTH = MLA_HEADS * MLA_V_DIM

LANES = 128
HEAD_PAIRS = SB_HEADS // 2
MLA_QK_PAD = LANES
_VT_PAD = LANES
BF16_SUBLANES = 16
_VT_ROWS = MLA_V_DIM + BF16_SUBLANES
FF_CHUNK = 256
ROW_TILE = 1024
FFN_ROW_TILE = 1024


class _Tiling(NamedTuple):
    q: int
    k: int
    unroll: int

    @property
    def diag(self):
        return self.q // self.k


SB_TILES = _Tiling(q=512, k=256, unroll=2)
MLA_TILES = _Tiling(q=512, k=256, unroll=4)
VMEM_LIMIT = 56 * 1024 * 1024

BF16 = jnp.bfloat16
F32 = jnp.float32
NEG_BIG = -1e30
LOG2E = 1.4426950408889634


def _dot(a, b):
    return jnp.dot(a, b, preferred_element_type=F32)


def _dot_nt(a, b):
    return lax.dot_general(a, b, (((1,), (1,)), ((), ())), preferred_element_type=F32)


def _layer_norm(y, g, b):
    mu = jnp.mean(y, axis=-1, keepdims=True)
    d = y - mu
    var = jnp.mean(d * d, axis=-1, keepdims=True)
    return d * lax.rsqrt(var + LN_EPS) * g + b


def _rms_norm(c, g):
    return c * lax.rsqrt(jnp.mean(c * c, axis=-1, keepdims=True) + RMS_EPS) * g


def _params(*sem):
    return pltpu.CompilerParams(dimension_semantics=sem, vmem_limit_bytes=VMEM_LIMIT)


def _resident(shape):
    nd = len(shape)
    return pl.BlockSpec(shape, lambda *_: (0,) * nd, pipeline_mode=pl.Buffered(1))


def _half_step_ffn(x, wg_ref, wu_ref, wd_ref, g_ref, b_ref, h_ref):
    xb = x.astype(BF16)
    for c in range(D_FF // FF_CHUNK):
        sl = slice(c * FF_CHUNK, (c + 1) * FF_CHUNK)
        gate = _dot(xb, wg_ref[:, sl])
        up = _dot(xb, wu_ref[:, sl])
        h_ref[:, sl] = (gate * (1.0 / (1.0 + jnp.exp(-gate))) * up).astype(BF16)
    f = _dot(h_ref[...], wd_ref[...])
    return _layer_norm(ALPHA * x + 0.5 * f, g_ref[...], b_ref[...])


def _ffn_kernel(x_ref, wg_ref, wu_ref, wd_ref, g_ref, b_ref, o_ref, h_ref):
    o_ref[...] = _half_step_ffn(x_ref[...], wg_ref, wu_ref, wd_ref, g_ref, b_ref, h_ref)


def _attn_out_ffn_kernel(x_ref, sb_ref, mla_ref, wo_ref, go_ref, bo_ref,
                         wg_ref, wu_ref, wd_ref, g_ref, b_ref, o_ref, h_ref):
    m = _dot(sb_ref[...], wo_ref[:SB_WIDTH, :]) + _dot(mla_ref[...], wo_ref[SB_WIDTH:, :])
    x = _layer_norm(ALPHA * x_ref[...] + m, go_ref[...], bo_ref[...])
    o_ref[...] = _half_step_ffn(x, wg_ref, wu_ref, wd_ref, g_ref, b_ref, h_ref)


def _ffn_call(body, name, rows, residents):
    n = rows[0][0].shape[0]
    row = lambda w: pl.BlockSpec((FFN_ROW_TILE, w), lambda i: (i, 0))
    return pl.pallas_call(
        body,
        out_shape=jax.ShapeDtypeStruct((n, D_MODEL), F32),
        grid=(n // FFN_ROW_TILE,),
        in_specs=[row(w) for _, w in rows] + [_resident(a.shape) for a in residents],
        out_specs=row(D_MODEL),
        scratch_shapes=[pltpu.VMEM((FFN_ROW_TILE, D_FF), BF16)],
        compiler_params=_params("parallel"),
        name=name,
    )(*[a for a, _ in rows], *residents)


def _ffn(x, wg, wu, wd, g, b):
    return _ffn_call(_ffn_kernel, "ffn", [(x, D_MODEL)], [wg, wu, wd, g, b])


def _attn_out_ffn(x, sb, mla, wo, go, bo, wg, wu, wd, g, b):
    return _ffn_call(_attn_out_ffn_kernel, "attn_out_ffn",
                     [(x, D_MODEL), (sb, SB_WIDTH), (mla, MLA_WIDTH)], [wo, go, bo, wg, wu, wd, g, b])


_X_K = SB_WIDTH
_X_CQ = _X_K + MLA_Q_RANK
_X_CKV = _X_CQ + MLA_KV_RANK
_X_KRA = _X_CKV + LANES
_X_KRB = _X_KRA + LANES
_QK_WIDTH = MLA_HEADS * MLA_QK_PAD


def _proj_even_kernel(x_ref, wx_ref, wsqv_ref, gq_ref, wqa_ref, wqb_ref, gkv_ref, wk_ref, wv_ref,
                      cos_ref, sin_ref, cos_t_ref, sin_t_ref,
                      sbk_ref, sbq_ref, sbv_ref, q_ref, k_ref, v_ref):
    xb = x_ref[...].astype(BF16)
    qv = _dot_nt(wsqv_ref[...], xb)
    sbq_ref[...] = (qv[:SB_WIDTH, :] * (SB_HEAD_DIM ** -0.5)).astype(BF16)
    sbv_ref[...] = qv[SB_WIDTH:, :].astype(BF16)
    sbk_ref[...] = _dot(xb, wx_ref[:, :_X_K]).astype(BF16)
    cq = _rms_norm(_dot(xb, wx_ref[:, _X_K:_X_CQ]), gq_ref[...]).astype(BF16)
    cos_t = cos_t_ref[...]
    sin_t = sin_t_ref[...]
    qa = _dot_nt(wqa_ref[...], cq)
    qb = _dot_nt(wqb_ref[...], cq)
    for h in range(MLA_HEADS):
        sl = slice(h * MLA_QK_PAD, (h + 1) * MLA_QK_PAD)
        q_ref[sl, :] = (qa[sl, :] * cos_t + qb[sl, :] * sin_t).astype(BF16)
    ckv = _rms_norm(_dot(xb, wx_ref[:, _X_CQ:_X_CKV]), gkv_ref[...]).astype(BF16)
    kr_pair = _dot(xb, wx_ref[:, _X_CKV:_X_KRB])
    kr = kr_pair[:, :LANES] * cos_ref[...] + kr_pair[:, LANES:] * sin_ref[...]
    k_nope = _dot(ckv, wk_ref[...])
    for h in range(MLA_HEADS):
        sl = slice(h * MLA_QK_PAD, (h + 1) * MLA_QK_PAD)
        k_ref[:, sl] = (k_nope[:, sl] + kr).astype(BF16)
    vt = _dot_nt(wv_ref[...], ckv)
    ones_row = lax.broadcasted_iota(jnp.int32, vt.shape, 0) % _VT_PAD == MLA_V_DIM
    v_ref[...] = jnp.where(ones_row, 1.0, vt).astype(BF16)


def _proj_even(x, wx, wsqv, gq, wqa, wqb, gkv, wk, wv, cos, sin, seq):
    n = x.shape[0]
    tiles_per_seq = seq // ROW_TILE
    row = lambda w: pl.BlockSpec((ROW_TILE, w), lambda i: (i, 0))
    col = lambda h: pl.BlockSpec((h, ROW_TILE), lambda i: (0, i))
    pos = pl.BlockSpec((ROW_TILE, LANES), lambda i: (i % tiles_per_seq, 0))
    pos_t = pl.BlockSpec((LANES, ROW_TILE), lambda i: (0, i % tiles_per_seq))
    rows = lambda w: jax.ShapeDtypeStruct((n, w), BF16)
    cols = lambda h: jax.ShapeDtypeStruct((h, n), BF16)
    return pl.pallas_call(
        _proj_even_kernel,
        out_shape=(rows(SB_WIDTH), cols(SB_WIDTH), cols(SB_WIDTH),
                   cols(_QK_WIDTH), rows(_QK_WIDTH), cols(MLA_HEADS * _VT_PAD)),
        grid=(n // ROW_TILE,),
        in_specs=[row(D_MODEL), _resident(wx.shape), _resident(wsqv.shape), _resident(gq.shape),
                  _resident(wqa.shape), _resident(wqb.shape), _resident(gkv.shape),
                  _resident(wk.shape), _resident(wv.shape), pos, pos, pos_t, pos_t],
        out_specs=(row(SB_WIDTH), col(SB_WIDTH), col(SB_WIDTH),
                   col(_QK_WIDTH), row(_QK_WIDTH), col(MLA_HEADS * _VT_PAD)),
        compiler_params=_params("parallel"),
        name="proj_even",
    )(x, wx, wsqv, gq, wqa, wqb, gkv, wk, wv, cos, sin, cos.T, sin.T)


def _sweep_keys(i, stages, tiles):
    depth = len(stages)
    blocks = (i + 1) * tiles.diag
    lead = [depth - 1 - s for s in range(depth)]
    peel = max(0, tiles.diag - depth + 1)
    for n in range(1 - depth, peel):
        for s, stage in enumerate(stages):
            b = n + lead[s]
            if b >= 0:
                stage(blocks - 1 - b, b % 2, (tiles.diag - 1 - b) * tiles.k if b < tiles.diag else None)

    def step_pair(t):
        for u in range(2):
            for s in range(depth):
                b_off = peel + u + lead[s]
                stages[s](blocks - 1 - b_off - 2 * t, b_off % 2, None)

    def body(t, carry):
        for r in range(tiles.unroll):
            step_pair(tiles.unroll * t + r)
        return carry

    pairs = i * (tiles.diag // 2)
    lax.fori_loop(0, pairs // tiles.unroll, body, 0)
    if (tiles.diag // 2) % tiles.unroll:
        for r in range(tiles.unroll - 1):
            pl.when(r < pairs % tiles.unroll)(
                functools.partial(step_pair, pairs - pairs % tiles.unroll + r))
    for e in range(depth - 1):
        for s, stage in enumerate(stages):
            j = depth - 2 - e - lead[s]
            if j >= 0:
                stage(j, (j + 1) % 2, None)


def _hidden(tiles, i, j, q_lo, visible_when_equal):
    shape = (tiles.k, tiles.q - q_lo)
    key = j * tiles.k + lax.broadcasted_iota(jnp.int32, shape, 0)
    qry = i * tiles.q + q_lo + lax.broadcasted_iota(jnp.int32, shape, 1)
    return key > qry if visible_when_equal else key >= qry


def _key_block(tiles, j):
    return pl.ds(pl.multiple_of(j * tiles.k, tiles.k), tiles.k)


def _merge_heads(o0, o1):
    return jnp.concatenate([o0, o1], axis=0).T


class _Slots:
    def __init__(self, first, second):
        self._refs = (first, second)

    def __getitem__(self, idx):
        return self._refs[idx[0]][idx[1:]]

    def __setitem__(self, idx, value):
        self._refs[idx[0]][idx[1:]] = value


def _slot_pair(shape, dtype):
    return [pltpu.VMEM(shape, dtype)] * 2


def _tile_pair(tiles, dtype):
    return _slot_pair((2, tiles.k, tiles.q), dtype)


def _sb_stages(tiles, i, qt_ref, k_ref, vt_ref, o_ref, z0, z1, e0, e1, total0, total1, acc_ref, later_ref):
    z_ref, e_ref, total_ref = _Slots(z0, z1), _Slots(e0, e1), _Slots(total0, total1)
    qt = qt_ref[...]
    dim = lax.broadcasted_iota(jnp.int32, (LANES, tiles.q), 0)
    qh = [jnp.where(dim < SB_HEAD_DIM, qt, jnp.zeros_like(qt)),
          jnp.where(dim >= SB_HEAD_DIM, qt, jnp.zeros_like(qt))]
    r = lax.broadcasted_iota(jnp.int32, (tiles.k, tiles.k), 0)
    c = lax.broadcasted_iota(jnp.int32, (tiles.k, tiles.k), 1)
    tri = jnp.where(c > r, 1.0, 0.0).astype(BF16)
    acc_ref[...] = jnp.zeros_like(acc_ref)
    later_ref[...] = jnp.zeros_like(later_ref)

    def scores(j, slot, q_lo):
        k = k_ref[_key_block(tiles, j), :]
        lo = q_lo or 0
        for h in range(2):
            z = _dot(k, qh[h][:, lo:])
            if q_lo is not None:
                z = jnp.where(_hidden(tiles, i, j, lo, visible_when_equal=False), NEG_BIG, z)
                if lo:
                    z_ref[slot, h, :, :lo] = jnp.full((tiles.k, lo), NEG_BIG, F32)
            z_ref[slot, h, :, lo:] = z

    def within_block(j, slot, q_lo):
        lo = q_lo or 0
        for h in range(2):
            z = z_ref[slot, h, :, lo:]
            neg_abs = lax.bitcast_convert_type(
                lax.bitcast_convert_type(z, jnp.uint32) | jnp.uint32(0x80000000), F32)
            softplus = jnp.maximum(z, 0.0) + jnp.log(1.0 + jnp.exp(neg_abs))
            later = _dot(tri, softplus.astype(BF16))
            if lo:
                e_ref[slot, h, :, :lo] = jnp.full((tiles.k, lo), NEG_BIG, F32)
                total_ref[slot, h, :, :lo] = jnp.zeros((1, lo), F32)
            e_ref[slot, h, :, lo:] = (z - softplus) - later
            total_ref[slot, h, :, lo:] = later[0:1, :] + softplus[0:1, :]

    def accumulate(j, slot, q_lo):
        lo = q_lo or 0
        for h in range(2):
            w = jnp.exp(e_ref[slot, h, :, lo:])
            vt = vt_ref[h * SB_HEAD_DIM:(h + 1) * SB_HEAD_DIM, _key_block(tiles, j)]
            acc_ref[h, :, lo:] += _dot(vt, w.astype(BF16)) * jnp.exp(-later_ref[h, :, lo:])
            later_ref[h, :, lo:] += total_ref[slot, h, :, lo:]

    def finish():
        o_ref[...] = _merge_heads(acc_ref[0], acc_ref[1]).astype(o_ref.dtype)

    return [scores, within_block, accumulate], finish


def _sb_scratch(tiles):
    return (_tile_pair(tiles, F32) + _tile_pair(tiles, F32) + _slot_pair((2, 1, tiles.q), F32)
            + [pltpu.VMEM((2, SB_HEAD_DIM, tiles.q), F32), pltpu.VMEM((2, 1, tiles.q), F32)])


def _mla_stages(tiles, i, qt_ref, k_ref, vt_ref, o_ref, s0, s1, acc_ref, m_ref):
    s_ref = _Slots(s0, s1)
    c2 = (MLA_NOPE_DIM + MLA_ROPE_DIM) ** -0.5 * LOG2E
    qh = [qt_ref[h * MLA_QK_PAD:(h + 1) * MLA_QK_PAD, :] for h in range(2)]
    acc_ref[...] = jnp.zeros_like(acc_ref)
    m_ref[...] = jnp.full_like(m_ref, NEG_BIG)

    def scores(j, slot, q_lo):
        lo = q_lo or 0
        for h in range(2):
            s = _dot(k_ref[_key_block(tiles, j), h * MLA_QK_PAD:(h + 1) * MLA_QK_PAD], qh[h][:, lo:])
            if q_lo is not None:
                s = jnp.where(_hidden(tiles, i, j, lo, visible_when_equal=True), NEG_BIG, s)
                if lo:
                    s_ref[slot, h, :, :lo] = jnp.full((tiles.k, lo), NEG_BIG, F32)
            s_ref[slot, h, :, lo:] = s

    def accumulate(j, slot, q_lo):
        lo = q_lo or 0
        for h in range(2):
            s = s_ref[slot, h, :, lo:]
            m = m_ref[h, :, lo:]
            m_new = jnp.maximum(m, jnp.max(s, axis=0, keepdims=True))
            p = jnp.exp2((s - m_new) * c2)
            rescale = jnp.exp2((m - m_new) * c2)
            m_ref[h, :, lo:] = m_new
            vt = vt_ref[h * _VT_PAD:h * _VT_PAD + _VT_ROWS, _key_block(tiles, j)]
            acc_ref[h, :, lo:] = rescale * acc_ref[h, :, lo:] + _dot(vt, p.astype(BF16))

    def finish():
        out = [acc_ref[h, :MLA_V_DIM, :] / acc_ref[h, MLA_V_DIM:MLA_V_DIM + 1, :] for h in range(2)]
        o_ref[...] = _merge_heads(*out).astype(o_ref.dtype)

    return [scores, accumulate], finish


def _mla_scratch(tiles):
    return _tile_pair(tiles, F32) + [pltpu.VMEM((2, _VT_ROWS, tiles.q), F32), pltpu.VMEM((2, 1, tiles.q), F32)]


def _attn_kernel(build_stages, tiles, qt_ref, k_ref, vt_ref, o_ref, *scratch):
    def query_block(i, carry):
        block = pl.ds(pl.multiple_of(i * tiles.q, tiles.q), tiles.q)
        stages, finish = build_stages(tiles, i, qt_ref.at[:, block], k_ref, vt_ref, o_ref.at[block], *scratch)
        _sweep_keys(i, stages, tiles)
        finish()
        return carry

    lax.fori_loop(0, qt_ref.shape[1] // tiles.q, query_block, 0)


def _attn_call(build_stages, scratch, name, tiles, q_t, k, v_t, qk_width, vt_rows, batch, seq, width):
    return pl.pallas_call(
        functools.partial(_attn_kernel, build_stages, tiles),
        out_shape=jax.ShapeDtypeStruct((k.shape[0], width), BF16),
        grid=(batch, HEAD_PAIRS),
        in_specs=[pl.BlockSpec((qk_width, seq), lambda b, p: (p, b)),
                  pl.BlockSpec((seq, qk_width), lambda b, p: (b, p)),
                  pl.BlockSpec((vt_rows, seq), lambda b, p: (p, b))],
        out_specs=pl.BlockSpec((seq, LANES), lambda b, p: (b, p)),
        scratch_shapes=scratch(tiles),
        compiler_params=_params("parallel", "parallel"),
        name=name,
    )(q_t, k, v_t)


def _sb_attn(q_t, k, v_t, batch, seq):
    return _attn_call(_sb_stages, _sb_scratch, "sb_attn", SB_TILES, q_t, k, v_t, LANES, LANES,
                      batch, seq, SB_WIDTH)


def _mla_attn(q_t, k, v_t, batch, seq):
    return _attn_call(_mla_stages, _mla_scratch, "mla_attn", MLA_TILES, q_t, k, v_t, 2 * MLA_QK_PAD,
                      2 * _VT_PAD, batch, seq, MLA_WIDTH)


HALO = 8


def _conv_mixer_kernel(tiles_per_seq, x_ref, win_ref, cw_ref, wout_ref, g_ref, b_ref, o_ref, tail_ref):
    i = pl.program_id(0)
    x = x_ref[...]
    xb = x.astype(BF16)
    gate_b = _dot(xb, win_ref[:, :D_MODEL])
    u = _dot(xb, win_ref[:, D_MODEL:2 * D_MODEL]) * _dot(xb, win_ref[:, 2 * D_MODEL:])

    @pl.when(i % tiles_per_seq == 0)
    def _():
        tail_ref[...] = jnp.zeros_like(tail_ref)

    prev = tail_ref[...]
    tail_ref[...] = u[ROW_TILE - HALO:, :]
    ext = jnp.concatenate([prev, u], axis=0)
    cw = cw_ref[...]
    conv = (cw[0:1, :] * ext[HALO - 2:HALO - 2 + ROW_TILE, :]
            + cw[1:2, :] * ext[HALO - 1:HALO - 1 + ROW_TILE, :]
            + cw[2:3, :] * u)
    m = _dot((gate_b * conv).astype(BF16), wout_ref[...])
    o_ref[...] = _layer_norm(ALPHA * x + m, g_ref[...], b_ref[...])


def _conv_mixer(x, w_in, conv_w, w_out, g, b, seq):
    n = x.shape[0]
    row = pl.BlockSpec((ROW_TILE, D_MODEL), lambda i: (i, 0))
    return pl.pallas_call(
        functools.partial(_conv_mixer_kernel, seq // ROW_TILE),
        out_shape=jax.ShapeDtypeStruct((n, D_MODEL), F32),
        grid=(n // ROW_TILE,),
        in_specs=[row, _resident(w_in.shape), _resident(conv_w.shape), _resident(w_out.shape),
                  _resident(g.shape), _resident(b.shape)],
        out_specs=row,
        scratch_shapes=[pltpu.VMEM((HALO, D_MODEL), F32)],
        compiler_params=_params("arbitrary"),
        name="conv_mixer",
    )(x, w_in, conv_w, w_out, g, b)


def _rope_tables(seq):
    half = MLA_ROPE_DIM // 2
    inv_freq = ROPE_THETA ** (-jnp.arange(0, MLA_ROPE_DIM, 2, dtype=F32) / MLA_ROPE_DIM)
    ang = jnp.arange(seq, dtype=F32)[:, None] * inv_freq[None, :]
    cos, sin = jnp.cos(ang), jnp.sin(ang)
    ones = jnp.ones((seq, MLA_NOPE_DIM), F32)
    pad = jnp.zeros((seq, MLA_QK_PAD - MLA_NOPE_DIM - 2 * half), F32)
    cos_t = jnp.concatenate([ones, cos, cos, pad], axis=1)
    sin_t = jnp.concatenate([0.0 * ones, -sin, sin, pad], axis=1)
    return cos_t, sin_t


def _even_weights(w_in, w_uq, w_ukv):
    half = MLA_ROPE_DIM // 2
    qk = MLA_NOPE_DIM + MLA_ROPE_DIM
    zeros = lambda r, c: jnp.zeros((r, c), F32)
    sb_q, sb_k, sb_v = (slice(g * SB_WIDTH, (g + 1) * SB_WIDTH) for g in range(3))
    lat = slice(3 * SB_WIDTH, 3 * SB_WIDTH + MLA_Q_RANK + MLA_KV_RANK)
    w_kr = w_in[:, lat.stop:]
    k1, k2 = w_kr[:, :half], w_kr[:, half:]
    nope_pad = zeros(D_MODEL, MLA_NOPE_DIM)
    tail_pad = zeros(D_MODEL, MLA_QK_PAD - qk)
    wx = jnp.concatenate([w_in[:, sb_k], w_in[:, lat], nope_pad, k1, k2, tail_pad,
                          nope_pad, k2, k1, tail_pad], axis=1)
    wsqv = jnp.concatenate([w_in[:, sb_q], w_in[:, sb_v]], axis=1).T
    wq = w_uq.reshape(MLA_Q_RANK, MLA_HEADS, qk)
    q_nope, q1, q2 = wq[..., :MLA_NOPE_DIM], wq[..., MLA_NOPE_DIM:MLA_NOPE_DIM + half], wq[..., MLA_NOPE_DIM + half:]
    zq = lambda c: jnp.zeros((MLA_Q_RANK, MLA_HEADS, c), F32)
    wqa = jnp.concatenate([q_nope, q1, q2, zq(MLA_QK_PAD - qk)], axis=-1).reshape(MLA_Q_RANK, _QK_WIDTH)
    wqb = jnp.concatenate([zq(MLA_NOPE_DIM), q2, q1, zq(MLA_QK_PAD - qk)], axis=-1).reshape(MLA_Q_RANK, _QK_WIDTH)
    wkv = w_ukv.reshape(MLA_KV_RANK, MLA_HEADS, MLA_NOPE_DIM + MLA_V_DIM)
    wk = jnp.concatenate([wkv[..., :MLA_NOPE_DIM],
                          jnp.zeros((MLA_KV_RANK, MLA_HEADS, MLA_QK_PAD - MLA_NOPE_DIM), F32)],
                         axis=-1).reshape(MLA_KV_RANK, _QK_WIDTH)
    wv = jnp.concatenate([wkv[..., MLA_NOPE_DIM:],
                          jnp.zeros((MLA_KV_RANK, MLA_HEADS, _VT_PAD - MLA_V_DIM), F32)],
                         axis=-1).reshape(MLA_KV_RANK, MLA_HEADS * _VT_PAD).T
    return [w.astype(BF16) for w in (wx, wsqv, wqa.T, wqb.T, wk, wv)]


def kernel(x, ln_g, ln_b, ffn_w_gate, ffn_w_up, ffn_w_down, mix_w_in, mla_q_norm_g, mla_w_uq,
           mla_kv_norm_g, mla_w_ukv, mix_w_out, conv_w_in, conv_w, conv_w_out):
    batch, seq, _ = x.shape
    assert seq % FFN_ROW_TILE == 0 and seq % ROW_TILE == 0
    for tiles in (SB_TILES, MLA_TILES):
        assert seq % tiles.q == 0 and tiles.q % (2 * tiles.k) == 0
    cos_t, sin_t = _rope_tables(seq)
    h = x.reshape(batch * seq, D_MODEL)
    vec = lambda a: a.reshape(1, -1)
    for layer in range(DEPTH):
        j = layer // 2
        ln = lambda s: (vec(ln_g[layer, s]), vec(ln_b[layer, s]))
        ffn_w = lambda s: (ffn_w_gate[layer, s].astype(BF16), ffn_w_up[layer, s].astype(BF16),
                           ffn_w_down[layer, s].astype(BF16))
        h = _ffn(h, *ffn_w(0), *ln(0))
        if layer % 2 == 0:
            wx, wsqv, wqa, wqb, wk, wv = _even_weights(mix_w_in[j], mla_w_uq[j], mla_w_ukv[j])
            sb_k, sb_qt, sb_vt, q_t, k, v_t = _proj_even(h, wx, wsqv, vec(mla_q_norm_g[j]), wqa, wqb,
                                                         vec(mla_kv_norm_g[j]), wk, wv, cos_t, sin_t, seq)
            out_sb = _sb_attn(sb_qt, sb_k, sb_vt, batch, seq)
            out_mla = _mla_attn(q_t, k, v_t, batch, seq)
            h = _attn_out_ffn(h, out_sb, out_mla, mix_w_out[j].astype(BF16), *ln(1), *ffn_w(1), *ln(2))
        else:
            h = _conv_mixer(h, conv_w_in[j].astype(BF16), conv_w[j], conv_w_out[j].astype(BF16), *ln(1), seq)
            h = _ffn(h, *ffn_w(1), *ln(2))
    return h.reshape(batch, seq, D_MODEL)
```

```python
import functools
from typing import NamedTuple

import jax
import jax.numpy as jnp
from jax import lax
from jax.experimental import pallas as pl
from jax.experimental.pallas import tpu as pltpu

D_MODEL = 1024
DEPTH = 4
SB_HEADS = 8
SB_HEAD_DIM = 64
MLA_HEADS = 8
MLA_NOPE_DIM = 64
MLA_ROPE_DIM = 32
MLA_V_DIM = 64
MLA_Q_RANK = 768
MLA_KV_RANK = 256
ROPE_THETA = 10000.0
CONV_K = 3
D_FF = 2816
LN_EPS = 1e-5
RMS_EPS = 1e-6
ALPHA = (2 * DEPTH) ** 0.25
SB_WIDTH = SB_HEADS * SB_HEAD_DIM
MLA_WIDTH = MLA_HEADS * MLA_V_DIM

LANES = 128
HEAD_PAIRS = SB_HEADS // 2
MLA_QK_PAD = LANES
_VT_PAD = LANES
BF16_SUBLANES = 16
_VT_ROWS = MLA_V_DIM + BF16_SUBLANES
FF_CHUNK = 256
ROW_TILE = 1024
FFN_ROW_TILE = 1024


class _Tiling(NamedTuple):
    q: int
    k: int
    unroll: int

    @property
    def diag(self):
        return self.q // self.k


SB_TILES = _Tiling(q=512, k=256, unroll=2)
MLA_TILES = _Tiling(q=512, k=256, unroll=4)
VMEM_LIMIT = 56 * 1024 * 1024

BF16 = jnp.bfloat16
F32 = jnp.float32
NEG_BIG = -1e30
LOG2E = 1.4426950408889634


def _dot(a, b):
    return jnp.dot(a, b, preferred_element_type=F32)


def _dot_nt(a, b):
    return lax.dot_general(a, b, (((1,), (1,)), ((), ())), preferred_element_type=F32)


def _layer_norm(y, g, b):
    mu = jnp.mean(y, axis=-1, keepdims=True)
    d = y - mu
    var = jnp.mean(d * d, axis=-1, keepdims=True)
    return d * lax.rsqrt(var + LN_EPS) * g + b


def _rms_norm(c, g):
    return c * lax.rsqrt(jnp.mean(c * c, axis=-1, keepdims=True) + RMS_EPS) * g


def _params(*sem):
    return pltpu.CompilerParams(dimension_semantics=sem, vmem_limit_bytes=VMEM_LIMIT)


def _resident(shape):
    nd = len(shape)
    return pl.BlockSpec(shape, lambda *_: (0,) * nd, pipeline_mode=pl.Buffered(1))


def _half_step_ffn(x, wg_ref, wu_ref, wd_ref, g_ref, b_ref, h_ref):
    xb = x.astype(BF16)
    for c in range(D_FF // FF_CHUNK):
        sl = slice(c * FF_CHUNK, (c + 1) * FF_CHUNK)
        gate = _dot(xb, wg_ref[:, sl])
        up = _dot(xb, wu_ref[:, sl])
        h_ref[:, sl] = (gate * (1.0 / (1.0 + jnp.exp(-gate))) * up).astype(BF16)
    f = _dot(h_ref[...], wd_ref[...])
    return _layer_norm(ALPHA * x + 0.5 * f, g_ref[...], b_ref[...])


def _ffn_kernel(x_ref, wg_ref, wu_ref, wd_ref, g_ref, b_ref, o_ref, h_ref):
    o_ref[...] = _half_step_ffn(x_ref[...], wg_ref, wu_ref, wd_ref, g_ref, b_ref, h_ref)


def _attn_out_ffn_kernel(x_ref, sb_ref, mla_ref, wo_ref, go_ref, bo_ref,
                         wg_ref, wu_ref, wd_ref, g_ref, b_ref, o_ref, h_ref):
    m = _dot(sb_ref[...], wo_ref[:SB_WIDTH, :]) + _dot(mla_ref[...], wo_ref[SB_WIDTH:, :])
    x = _layer_norm(ALPHA * x_ref[...] + m, go_ref[...], bo_ref[...])
    o_ref[...] = _half_step_ffn(x, wg_ref, wu_ref, wd_ref, g_ref, b_ref, h_ref)


def _ffn_call(body, name, rows, residents):
    n = rows[0][0].shape[0]
    row = lambda w: pl.BlockSpec((FFN_ROW_TILE, w), lambda i: (i, 0))
    return pl.pallas_call(
        body,
        out_shape=jax.ShapeDtypeStruct((n, D_MODEL), F32),
        grid=(n // FFN_ROW_TILE,),
        in_specs=[row(w) for _, w in rows] + [_resident(a.shape) for a in residents],
        out_specs=row(D_MODEL),
        scratch_shapes=[pltpu.VMEM((FFN_ROW_TILE, D_FF), BF16)],
        compiler_params=_params("parallel"),
        name=name,
    )(*[a for a, _ in rows], *residents)


def _ffn(x, wg, wu, wd, g, b):
    return _ffn_call(_ffn_kernel, "ffn", [(x, D_MODEL)], [wg, wu, wd, g, b])


def _attn_out_ffn(x, sb, mla, wo, go, bo, wg, wu, wd, g, b):
    return _ffn_call(_attn_out_ffn_kernel, "attn_out_ffn",
                     [(x, D_MODEL), (sb, SB_WIDTH), (mla, MLA_WIDTH)], [wo, go, bo, wg, wu, wd, g, b])


_X_K = SB_WIDTH
_X_CQ = _X_K + MLA_Q_RANK
_X_CKV = _X_CQ + MLA_KV_RANK
_X_KRA = _X_CKV + LANES
_X_KRB = _X_KRA + LANES
_QK_WIDTH = MLA_HEADS * MLA_QK_PAD


def _proj_even_kernel(x_ref, wx_ref, wsqv_ref, gq_ref, wqa_ref, wqb_ref, gkv_ref, wk_ref, wv_ref,
                      cos_ref, sin_ref, cos_t_ref, sin_t_ref,
                      sbk_ref, sbq_ref, sbv_ref, q_ref, k_ref, v_ref):
    xb = x_ref[...].astype(BF16)
    qv = _dot_nt(wsqv_ref[...], xb)
    sbq_ref[...] = (qv[:SB_WIDTH, :] * (SB_HEAD_DIM ** -0.5)).astype(BF16)
    sbv_ref[...] = qv[SB_WIDTH:, :].astype(BF16)
    sbk_ref[...] = _dot(xb, wx_ref[:, :_X_K]).astype(BF16)
    cq = _rms_norm(_dot(xb, wx_ref[:, _X_K:_X_CQ]), gq_ref[...]).astype(BF16)
    cos_t = cos_t_ref[...]
    sin_t = sin_t_ref[...]
    qa = _dot_nt(wqa_ref[...], cq)
    qb = _dot_nt(wqb_ref[...], cq)
    for h in range(MLA_HEADS):
        sl = slice(h * MLA_QK_PAD, (h + 1) * MLA_QK_PAD)
        q_ref[sl, :] = (qa[sl, :] * cos_t + qb[sl, :] * sin_t).astype(BF16)
    ckv = _rms_norm(_dot(xb, wx_ref[:, _X_CQ:_X_CKV]), gkv_ref[...]).astype(BF16)
    kr_pair = _dot(xb, wx_ref[:, _X_CKV:_X_KRB])
    kr = kr_pair[:, :LANES] * cos_ref[...] + kr_pair[:, LANES:] * sin_ref[...]
    k_nope = _dot(ckv, wk_ref[...])
    for h in range(MLA_HEADS):
        sl = slice(h * MLA_QK_PAD, (h + 1) * MLA_QK_PAD)
        k_ref[:, sl] = (k_nope[:, sl] + kr).astype(BF16)
    vt = _dot_nt(wv_ref[...], ckv)
    ones_row = lax.broadcasted_iota(jnp.int32, vt.shape, 0) % _VT_PAD == MLA_V_DIM
    v_ref[...] = jnp.where(ones_row, 1.0, vt).astype(BF16)


def _proj_even(x, wx, wsqv, gq, wqa, wqb, gkv, wk, wv, cos, sin, seq):
    n = x.shape[0]
    tiles_per_seq = seq // ROW_TILE
    row = lambda w: pl.BlockSpec((ROW_TILE, w), lambda i: (i, 0))
    col = lambda h: pl.BlockSpec((h, ROW_TILE), lambda i: (0, i))
    pos = pl.BlockSpec((ROW_TILE, LANES), lambda i: (i % tiles_per_seq, 0))
    pos_t = pl.BlockSpec((LANES, ROW_TILE), lambda i: (0, i % tiles_per_seq))
    rows = lambda w: jax.ShapeDtypeStruct((n, w), BF16)
    cols = lambda h: jax.ShapeDtypeStruct((h, n), BF16)
    return pl.pallas_call(
        _proj_even_kernel,
        out_shape=(rows(SB_WIDTH), cols(SB_WIDTH), cols(SB_WIDTH),
                   cols(_QK_WIDTH), rows(_QK_WIDTH), cols(MLA_HEADS * _VT_PAD)),
        grid=(n // ROW_TILE,),
        in_specs=[row(D_MODEL), _resident(wx.shape), _resident(wsqv.shape), _resident(gq.shape),
                  _resident(wqa.shape), _resident(wqb.shape), _resident(gkv.shape),
                  _resident(wk.shape), _resident(wv.shape), pos, pos, pos_t, pos_t],
        out_specs=(row(SB_WIDTH), col(SB_WIDTH), col(SB_WIDTH),
                   col(_QK_WIDTH), row(_QK_WIDTH), col(MLA_HEADS * _VT_PAD)),
        compiler_params=_params("parallel"),
        name="proj_even",
    )(x, wx, wsqv, gq, wqa, wqb, gkv, wk, wv, cos, sin, cos.T, sin.T)


def _sweep_keys(i, stages, tiles):
    depth = len(stages)
    blocks = (i + 1) * tiles.diag
    lead = [depth - 1 - s for s in range(depth)]
    peel = max(0, tiles.diag - depth + 1)
    for n in range(1 - depth, peel):
        for s, stage in enumerate(stages):
            b = n + lead[s]
            if b >= 0:
                stage(blocks - 1 - b, b % 2, (tiles.diag - 1 - b) * tiles.k if b < tiles.diag else None)

    def step_pair(t):
        for u in range(2):
            for s in range(depth):
                b_off = peel + u + lead[s]
                stages[s](blocks - 1 - b_off - 2 * t, b_off % 2, None)

    def body(t, carry):
        for r in range(tiles.unroll):
            step_pair(tiles.unroll * t + r)
        return carry

    pairs = i * (tiles.diag // 2)
    lax.fori_loop(0, pairs // tiles.unroll, body, 0)
    if (tiles.diag // 2) % tiles.unroll:
        for r in range(tiles.unroll - 1):
            pl.when(r < pairs % tiles.unroll)(
                functools.partial(step_pair, pairs - pairs % tiles.unroll + r))
    for e in range(depth - 1):
        for s, stage in enumerate(stages):
            j = depth - 2 - e - lead[s]
            if j >= 0:
                stage(j, (j + 1) % 2, None)


def _hidden(tiles, i, j, q_lo, visible_when_equal):
    shape = (tiles.k, tiles.q - q_lo)
    key = j * tiles.k + lax.broadcasted_iota(jnp.int32, shape, 0)
    qry = i * tiles.q + q_lo + lax.broadcasted_iota(jnp.int32, shape, 1)
    return key > qry if visible_when_equal else key >= qry


def _key_block(tiles, j):
    return pl.ds(pl.multiple_of(j * tiles.k, tiles.k), tiles.k)


def _merge_heads(o0, o1):
    return jnp.concatenate([o0, o1], axis=0).T


class _Slots:
    def __init__(self, first, second):
        self._refs = (first, second)

    def __getitem__(self, idx):
        return self._refs[idx[0]][idx[1:]]

    def __setitem__(self, idx, value):
        self._refs[idx[0]][idx[1:]] = value


def _slot_pair(shape, dtype):
    return [pltpu.VMEM(shape, dtype)] * 2


def _tile_pair(tiles, dtype):
    return _slot_pair((2, tiles.k, tiles.q), dtype)


def _sb_stages(tiles, i, qt_ref, k_ref, vt_ref, o_ref, z0, z1, e0, e1, total0, total1, acc_ref, later_ref):
    z_ref, e_ref, total_ref = _Slots(z0, z1), _Slots(e0, e1), _Slots(total0, total1)
    qt = qt_ref[...]
    dim = lax.broadcasted_iota(jnp.int32, (LANES, tiles.q), 0)
    qh = [jnp.where(dim < SB_HEAD_DIM, qt, jnp.zeros_like(qt)),
          jnp.where(dim >= SB_HEAD_DIM, qt, jnp.zeros_like(qt))]
    r = lax.broadcasted_iota(jnp.int32, (tiles.k, tiles.k), 0)
    c = lax.broadcasted_iota(jnp.int32, (tiles.k, tiles.k), 1)
    tri = jnp.where(c > r, 1.0, 0.0).astype(BF16)
    acc_ref[...] = jnp.zeros_like(acc_ref)
    later_ref[...] = jnp.zeros_like(later_ref)

    def scores(j, slot, q_lo):
        k = k_ref[_key_block(tiles, j), :]
        lo = q_lo or 0
        for h in range(2):
            z = _dot(k, qh[h][:, lo:])
            if q_lo is not None:
                z = jnp.where(_hidden(tiles, i, j, lo, visible_when_equal=False), NEG_BIG, z)
                if lo:
                    z_ref[slot, h, :, :lo] = jnp.full((tiles.k, lo), NEG_BIG, F32)
            z_ref[slot, h, :, lo:] = z

    def within_block(j, slot, q_lo):
        lo = q_lo or 0
        for h in range(2):
            z = z_ref[slot, h, :, lo:]
            neg_abs = lax.bitcast_convert_type(
                lax.bitcast_convert_type(z, jnp.uint32) | jnp.uint32(0x80000000), F32)
            softplus = jnp.maximum(z, 0.0) + jnp.log(1.0 + jnp.exp(neg_abs))
            later = _dot(tri, softplus.astype(BF16))
            if lo:
                e_ref[slot, h, :, :lo] = jnp.full((tiles.k, lo), NEG_BIG, F32)
                total_ref[slot, h, :, :lo] = jnp.zeros((1, lo), F32)
            e_ref[slot, h, :, lo:] = (z - softplus) - later
            total_ref[slot, h, :, lo:] = later[0:1, :] + softplus[0:1, :]

    def accumulate(j, slot, q_lo):
        lo = q_lo or 0
        for h in range(2):
            w = jnp.exp(e_ref[slot, h, :, lo:])
            vt = vt_ref[h * SB_HEAD_DIM:(h + 1) * SB_HEAD_DIM, _key_block(tiles, j)]
            acc_ref[h, :, lo:] += _dot(vt, w.astype(BF16)) * jnp.exp(-later_ref[h, :, lo:])
            later_ref[h, :, lo:] += total_ref[slot, h, :, lo:]

    def finish():
        o_ref[...] = _merge_heads(acc_ref[0], acc_ref[1]).astype(o_ref.dtype)

    return [scores, within_block, accumulate], finish


def _sb_scratch(tiles):
    return (_tile_pair(tiles, F32) + _tile_pair(tiles, F32) + _slot_pair((2, 1, tiles.q), F32)
            + [pltpu.VMEM((2, SB_HEAD_DIM, tiles.q), F32), pltpu.VMEM((2, 1, tiles.q), F32)])


def _mla_stages(tiles, i, qt_ref, k_ref, vt_ref, o_ref, s0, s1, acc_ref, m_ref):
    s_ref = _Slots(s0, s1)
    c2 = (MLA_NOPE_DIM + MLA_ROPE_DIM) ** -0.5 * LOG2E
    qh = [qt_ref[h * MLA_QK_PAD:(h + 1) * MLA_QK_PAD, :] for h in range(2)]
    acc_ref[...] = jnp.zeros_like(acc_ref)
    m_ref[...] = jnp.full_like(m_ref, NEG_BIG)

    def scores(j, slot, q_lo):
        lo = q_lo or 0
        for h in range(2):
            s = _dot(k_ref[_key_block(tiles, j), h * MLA_QK_PAD:(h + 1) * MLA_QK_PAD], qh[h][:, lo:])
            if q_lo is not None:
                s = jnp.where(_hidden(tiles, i, j, lo, visible_when_equal=True), NEG_BIG, s)
                if lo:
                    s_ref[slot, h, :, :lo] = jnp.full((tiles.k, lo), NEG_BIG, F32)
            s_ref[slot, h, :, lo:] = s

    def accumulate(j, slot, q_lo):
        lo = q_lo or 0
        for h in range(2):
            s = s_ref[slot, h, :, lo:]
            m = m_ref[h, :, lo:]
            m_new = jnp.maximum(m, jnp.max(s, axis=0, keepdims=True))
            p = jnp.exp2((s - m_new) * c2)
            rescale = jnp.exp2((m - m_new) * c2)
            m_ref[h, :, lo:] = m_new
            vt = vt_ref[h * _VT_PAD:h * _VT_PAD + _VT_ROWS, _key_block(tiles, j)]
            acc_ref[h, :, lo:] = rescale * acc_ref[h, :, lo:] + _dot(vt, p.astype(BF16))

    def finish():
        out = [acc_ref[h, :MLA_V_DIM, :] / acc_ref[h, MLA_V_DIM:MLA_V_DIM + 1, :] for h in range(2)]
        o_ref[...] = _merge_heads(*out).astype(o_ref.dtype)

    return [scores, accumulate], finish


def _mla_scratch(tiles):
    return _tile_pair(tiles, F32) + [pltpu.VMEM((2, _VT_ROWS, tiles.q), F32), pltpu.VMEM((2, 1, tiles.q), F32)]


def _attn_kernel(build_stages, tiles, qt_ref, k_ref, vt_ref, o_ref, *scratch):
    def query_block(i, carry):
        block = pl.ds(pl.multiple_of(i * tiles.q, tiles.q), tiles.q)
        stages, finish = build_stages(tiles, i, qt_ref.at[:, block], k_ref, vt_ref, o_ref.at[block], *scratch)
        _sweep_keys(i, stages, tiles)
        finish()
        return carry

    lax.fori_loop(0, qt_ref.shape[1] // tiles.q, query_block, 0)


def _attn_call(build_stages, scratch, name, tiles, q_t, k, v_t, qk_width, vt_rows, batch, seq, width):
    return pl.pallas_call(
        functools.partial(_attn_kernel, build_stages, tiles),
        out_shape=jax.ShapeDtypeStruct((k.shape[0], width), BF16),
        grid=(batch, HEAD_PAIRS),
        in_specs=[pl.BlockSpec((qk_width, seq), lambda b, p: (p, b)),
                  pl.BlockSpec((seq, qk_width), lambda b, p: (b, p)),
                  pl.BlockSpec((vt_rows, seq), lambda b, p: (p, b))],
        out_specs=pl.BlockSpec((seq, LANES), lambda b, p: (b, p)),
        scratch_shapes=scratch(tiles),
        compiler_params=_params("parallel", "parallel"),
        name=name,
    )(q_t, k, v_t)


def _sb_attn(q_t, k, v_t, batch, seq):
    return _attn_call(_sb_stages, _sb_scratch, "sb_attn", SB_TILES, q_t, k, v_t, LANES, LANES,
                      batch, seq, SB_WIDTH)


def _mla_attn(q_t, k, v_t, batch, seq):
    return _attn_call(_mla_stages, _mla_scratch, "mla_attn", MLA_TILES, q_t, k, v_t, 2 * MLA_QK_PAD,
                      2 * _VT_PAD, batch, seq, MLA_WIDTH)


HALO = 8


CONV_FFN_ROWS = 512


def _conv_ffn_kernel(tiles_per_seq, x_ref, win_ref, cw_ref, wout_ref, go_ref, bo_ref,
                     wg_ref, wu_ref, wd_ref, g_ref, b_ref, o_ref, h_ref, tail_ref):
    rows = CONV_FFN_ROWS
    i = pl.program_id(0)
    x = x_ref[...]
    xb = x.astype(BF16)
    gate_b = _dot(xb, win_ref[:, :D_MODEL])
    u = _dot(xb, win_ref[:, D_MODEL:2 * D_MODEL]) * _dot(xb, win_ref[:, 2 * D_MODEL:])

    @pl.when(i % tiles_per_seq == 0)
    def _():
        tail_ref[...] = jnp.zeros_like(tail_ref)

    prev = tail_ref[...]
    tail_ref[...] = u[rows - HALO:, :]
    ext = jnp.concatenate([prev, u], axis=0)
    cw = cw_ref[...]
    conv = (cw[0:1, :] * ext[HALO - 2:HALO - 2 + rows, :]
            + cw[1:2, :] * ext[HALO - 1:HALO - 1 + rows, :]
            + cw[2:3, :] * u)
    m = _dot((gate_b * conv).astype(BF16), wout_ref[...])
    mid = _layer_norm(ALPHA * x + m, go_ref[...], bo_ref[...])
    o_ref[...] = _half_step_ffn(mid, wg_ref, wu_ref, wd_ref, g_ref, b_ref, h_ref)


def _conv_ffn(x, w_in, conv_w, w_out, go, bo, wg, wu, wd, g, b, seq):
    n = x.shape[0]
    rows = CONV_FFN_ROWS
    row = pl.BlockSpec((rows, D_MODEL), lambda i: (i, 0))
    residents = [w_in, conv_w, w_out, go, bo, wg, wu, wd, g, b]
    return pl.pallas_call(
        functools.partial(_conv_ffn_kernel, seq // rows),
        out_shape=jax.ShapeDtypeStruct((n, D_MODEL), F32),
        grid=(n // rows,),
        in_specs=[row] + [_resident(a.shape) for a in residents],
        out_specs=row,
        scratch_shapes=[pltpu.VMEM((rows, D_FF), BF16), pltpu.VMEM((HALO, D_MODEL), F32)],
        compiler_params=_params("arbitrary"),
        name="conv_ffn",
    )(x, *residents)


def _rope_tables(seq):
    half = MLA_ROPE_DIM // 2
    inv_freq = ROPE_THETA ** (-jnp.arange(0, MLA_ROPE_DIM, 2, dtype=F32) / MLA_ROPE_DIM)
    ang = jnp.arange(seq, dtype=F32)[:, None] * inv_freq[None, :]
    cos, sin = jnp.cos(ang), jnp.sin(ang)
    ones = jnp.ones((seq, MLA_NOPE_DIM), F32)
    pad = jnp.zeros((seq, MLA_QK_PAD - MLA_NOPE_DIM - 2 * half), F32)
    cos_t = jnp.concatenate([ones, cos, cos, pad], axis=1)
    sin_t = jnp.concatenate([0.0 * ones, -sin, sin, pad], axis=1)
    return cos_t, sin_t


def _even_weights(w_in, w_uq, w_ukv):
    half = MLA_ROPE_DIM // 2
    qk = MLA_NOPE_DIM + MLA_ROPE_DIM
    zeros = lambda r, c: jnp.zeros((r, c), F32)
    sb_q, sb_k, sb_v = (slice(g * SB_WIDTH, (g + 1) * SB_WIDTH) for g in range(3))
    lat = slice(3 * SB_WIDTH, 3 * SB_WIDTH + MLA_Q_RANK + MLA_KV_RANK)
    w_kr = w_in[:, lat.stop:]
    k1, k2 = w_kr[:, :half], w_kr[:, half:]
    nope_pad = zeros(D_MODEL, MLA_NOPE_DIM)
    tail_pad = zeros(D_MODEL, MLA_QK_PAD - qk)
    wx = jnp.concatenate([w_in[:, sb_k], w_in[:, lat], nope_pad, k1, k2, tail_pad,
                          nope_pad, k2, k1, tail_pad], axis=1)
    wsqv = jnp.concatenate([w_in[:, sb_q], w_in[:, sb_v]], axis=1).T
    wq = w_uq.reshape(MLA_Q_RANK, MLA_HEADS, qk)
    q_nope, q1, q2 = wq[..., :MLA_NOPE_DIM], wq[..., MLA_NOPE_DIM:MLA_NOPE_DIM + half], wq[..., MLA_NOPE_DIM + half:]
    zq = lambda c: jnp.zeros((MLA_Q_RANK, MLA_HEADS, c), F32)
    wqa = jnp.concatenate([q_nope, q1, q2, zq(MLA_QK_PAD - qk)], axis=-1).reshape(MLA_Q_RANK, _QK_WIDTH)
    wqb = jnp.concatenate([zq(MLA_NOPE_DIM), q2, q1, zq(MLA_QK_PAD - qk)], axis=-1).reshape(MLA_Q_RANK, _QK_WIDTH)
    wkv = w_ukv.reshape(MLA_KV_RANK, MLA_HEADS, MLA_NOPE_DIM + MLA_V_DIM)
    wk = jnp.concatenate([wkv[..., :MLA_NOPE_DIM],
                          jnp.zeros((MLA_KV_RANK, MLA_HEADS, MLA_QK_PAD - MLA_NOPE_DIM), F32)],
                         axis=-1).reshape(MLA_KV_RANK, _QK_WIDTH)
    wv = jnp.concatenate([wkv[..., MLA_NOPE_DIM:],
                          jnp.zeros((MLA_KV_RANK, MLA_HEADS, _VT_PAD - MLA_V_DIM), F32)],
                         axis=-1).reshape(MLA_KV_RANK, MLA_HEADS * _VT_PAD).T
    return [w.astype(BF16) for w in (wx, wsqv, wqa.T, wqb.T, wk, wv)]


def kernel(x, ln_g, ln_b, ffn_w_gate, ffn_w_up, ffn_w_down, mix_w_in, mla_q_norm_g, mla_w_uq,
           mla_kv_norm_g, mla_w_ukv, mix_w_out, conv_w_in, conv_w, conv_w_out):
    batch, seq, _ = x.shape
    assert seq % FFN_ROW_TILE == 0 and seq % ROW_TILE == 0
    for tiles in (SB_TILES, MLA_TILES):
        assert seq % tiles.q == 0 and tiles.q % (2 * tiles.k) == 0
    cos_t, sin_t = _rope_tables(seq)
    h = x.reshape(batch * seq, D_MODEL)
    vec = lambda a: a.reshape(1, -1)
    for layer in range(DEPTH):
        j = layer // 2
        ln = lambda s: (vec(ln_g[layer, s]), vec(ln_b[layer, s]))
        ffn_w = lambda s: (ffn_w_gate[layer, s].astype(BF16), ffn_w_up[layer, s].astype(BF16),
                           ffn_w_down[layer, s].astype(BF16))
        h = _ffn(h, *ffn_w(0), *ln(0))
        if layer % 2 == 0:
            wx, wsqv, wqa, wqb, wk, wv = _even_weights(mix_w_in[j], mla_w_uq[j], mla_w_ukv[j])
            sb_k, sb_qt, sb_vt, q_t, k, v_t = _proj_even(h, wx, wsqv, vec(mla_q_norm_g[j]), wqa, wqb,
                                                         vec(mla_kv_norm_g[j]), wk, wv, cos_t, sin_t, seq)
            out_sb = _sb_attn(sb_qt, sb_k, sb_vt, batch, seq)
            out_mla = _mla_attn(q_t, k, v_t, batch, seq)
            h = _attn_out_ffn(h, out_sb, out_mla, mix_w_out[j].astype(BF16), *ln(1), *ffn_w(1), *ln(2))
        else:
            h = _conv_ffn(h, conv_w_in[j].astype(BF16), conv_w[j], conv_w_out[j].astype(BF16), *ln(1),
                          *ffn_w(1), *ln(2), seq)
    return h.reshape(batch, seq, D_MODEL)
```
